```python
import jax, jax.numpy as jnp
from jax import lax
import numpy as np

D_MODEL = 1024
BATCH = 2
SEQ = 8192
DEPTH = 2

GRID_W = 64
MLSTM_HEADS = 4
MLSTM_HD = 128
MLSTM_W = MLSTM_HEADS * MLSTM_HD
MLSTM_CHUNK = 64
CONV_W = 5
ATTN_HEADS = 8
KV_HEADS = 2
ATTN_HD = 64
ATTN_W = ATTN_HEADS * ATTN_HD
KV_W = KV_HEADS * ATTN_HD
Q_BLOCK = 128
ROPE_BASE = 10000.0
N_GROUPS = 4
EXPERTS_PER_GROUP = 8
N_EXPERTS = N_GROUPS * EXPERTS_PER_GROUP
TOP_K = 2
EXPERT_HIDDEN = 512
NORM_EPS = 1e-6
SPLIT_SIZES = (MLSTM_W, MLSTM_W, MLSTM_W, MLSTM_W, 4 * MLSTM_HEADS, ATTN_W, KV_W, KV_W, D_MODEL, D_MODEL)
IN_W = sum(SPLIT_SIZES)

kernel_name = 'hybrid_mlstm_gqa_hmoe_encoder'


def rms_norm(x, g):
    xf = x.astype(jnp.float32)
    y = xf * lax.rsqrt(jnp.mean(xf * xf, axis=-1, keepdims=True) + NORM_EPS)
    return (y * g.astype(jnp.float32)).astype(x.dtype)


def rope_tables(seq):
    rows = seq // GRID_W
    row = jnp.repeat(jnp.arange(rows, dtype=jnp.float32), GRID_W)
    col = jnp.tile(jnp.arange(GRID_W, dtype=jnp.float32), rows)
    half = ATTN_HD // 2
    inv_freq = ROPE_BASE ** (-jnp.arange(0, half, 2, dtype=jnp.float32) / half)
    ang_r = row[:, None] * inv_freq
    ang_c = col[:, None] * inv_freq
    return (jnp.cos(ang_r), jnp.sin(ang_r), jnp.cos(ang_c), jnp.sin(ang_c))


def rotate(x, cos, sin):
    x1, x2 = jnp.split(x, 2, axis=-1)
    return jnp.concatenate([x1 * cos - x2 * sin, x2 * cos + x1 * sin], axis=-1)


def axial_rope(x, tables):
    cr, sr, cc, sc = [t[:, None, :] for t in tables]
    xr, xc = jnp.split(x, 2, axis=-1)
    return jnp.concatenate([rotate(xr, cr, sr), rotate(xc, cc, sc)], axis=-1)


def bidirectional_gqa(q, k, v, q_g, k_g, tables):
    B, S = q.shape[0], q.shape[1]
    G = ATTN_HEADS // KV_HEADS
    q = axial_rope(rms_norm(q, q_g).astype(jnp.float32), tables)
    k = axial_rope(rms_norm(k, k_g).astype(jnp.float32), tables)
    nb = S // Q_BLOCK
    qb = q.reshape(B, nb, Q_BLOCK, KV_HEADS, G, ATTN_HD).transpose(1, 0, 3, 4, 2, 5)
    kt = k.transpose(0, 2, 1, 3)
    vt = v.transpose(0, 2, 1, 3)
    scale = ATTN_HD ** -0.5

    def block(q_blk):
        s = jnp.einsum('bkgqd,bksd->bkgqs', q_blk, kt) * scale
        p = jax.nn.softmax(s.astype(jnp.float32), axis=-1)
        return jnp.einsum('bkgqs,bksd->bkgqd', p.astype(vt.dtype), vt)

    o = lax.map(block, qb)
    return o.transpose(1, 0, 4, 2, 3, 5).reshape(B, S, ATTN_W)


def mlstm_one_direction(q, k, v, log_i, log_f):
    B, H, S, dh = q.shape
    L = MLSTM_CHUNK
    nc = S // L
    q = q.reshape(B, H, nc, L, dh)
    k = k.reshape(B, H, nc, L, dh)
    v = v.reshape(B, H, nc, L, dh)
    li = log_i.reshape(B, H, nc, L)
    lf = log_f.reshape(B, H, nc, L)
    b = jnp.cumsum(lf, axis=-1)
    b_last = b[..., -1]
    a = b_last[..., None] - b + li
    a_max = jnp.max(a, axis=-1)
    w = jnp.exp(a - a_max[..., None])
    c_chunk = jnp.einsum('bhcl,bhcld,bhcle->bhcde', w, v, k)
    n_chunk = jnp.einsum('bhcl,bhcle->bhce', w, k)

    def step(carry, inp):
        c_st, n_st, m_st = carry
        c_in, n_in, bl, am = inp
        m_new = jnp.maximum(bl + m_st, am)
        decay = jnp.exp(bl + m_st - m_new)
        inject = jnp.exp(am - m_new)
        c_new = decay[..., None, None] * c_st + inject[..., None, None] * c_in
        n_new = decay[..., None] * n_st + inject[..., None] * n_in
        return (c_new, n_new, m_new), (c_st, n_st, m_st)

    init = (jnp.zeros((B, H, dh, dh), jnp.float32), jnp.zeros((B, H, dh), jnp.float32), jnp.zeros((B, H), jnp.float32))
    xs = (jnp.moveaxis(c_chunk, 2, 0), jnp.moveaxis(n_chunk, 2, 0), jnp.moveaxis(b_last, 2, 0), jnp.moveaxis(a_max, 2, 0))
    _, (c_prev, n_prev, m_prev) = lax.scan(step, init, xs)
    c_prev = jnp.moveaxis(c_prev, 0, 2)
    n_prev = jnp.moveaxis(n_prev, 0, 2)
    m_prev = jnp.moveaxis(m_prev, 0, 2)

    lower = jnp.tril(jnp.ones((L, L), dtype=bool))
    d = jnp.where(lower, b[..., :, None] - b[..., None, :] + li[..., None, :], -jnp.inf)
    m_inter = b + m_prev[..., None]
    m_out = jnp.maximum(m_inter, jnp.max(d, axis=-1))
    dw = jnp.exp(d - m_out[..., None])
    inter_w = jnp.exp(m_inter - m_out)
    s = jnp.einsum('bhcjd,bhcld->bhcjl', q, k) * dw
    num = jnp.einsum('bhcjl,bhcld->bhcjd', s, v) + inter_w[..., None] * jnp.einsum('bhcde,bhcje->bhcjd', c_prev, q)
    den = jnp.sum(s, axis=-1) + inter_w * jnp.einsum('bhce,bhcje->bhcj', n_prev, q)
    h = num / jnp.maximum(jnp.abs(den), jnp.exp(-m_out))[..., None]
    return h.reshape(B, H, S, dh)


def mlstm_mixer(q_pre, k_pre, v, o_pre, gates, conv_w, conv_b, norm_g):
    B, S = q_pre.shape[0], q_pre.shape[1]
    H, dh = MLSTM_HEADS, MLSTM_HD
    qk = jnp.concatenate([q_pre, k_pre], axis=-1)
    qk = lax.conv_general_dilated(qk, conv_w, window_strides=(1,), padding=[(CONV_W // 2, CONV_W // 2)],
                                  dimension_numbers=('NWC', 'WIO', 'NWC'), feature_group_count=2 * MLSTM_W) + conv_b
    qk = jax.nn.silu(qk)
    q, k = jnp.split(qk, 2, axis=-1)

    def heads(t):
        return t.reshape(B, S, H, dh).transpose(0, 2, 1, 3).astype(jnp.float32)

    q = heads(q) * (MLSTM_HD ** -0.5)
    k = heads(k)
    v = heads(v)
    g = gates.astype(jnp.float32).reshape(B, S, 4, H).transpose(2, 0, 3, 1)
    h_fwd = mlstm_one_direction(q, k, v, g[0], jax.nn.log_sigmoid(g[1]))

    def flip(t):
        return jnp.flip(t, axis=2)

    h_bwd = flip(mlstm_one_direction(flip(q), flip(k), flip(v), flip(g[2]), flip(jax.nn.log_sigmoid(g[3]))))
    h = (h_fwd + h_bwd).transpose(0, 2, 1, 3)
    h = rms_norm(h, norm_g.reshape(H, dh))
    return h.reshape(B, S, MLSTM_W).astype(o_pre.dtype) * jax.nn.sigmoid(o_pre)


def hierarchical_moe(h, w_rg, b_rg, w_re, b_re, w_gate, w_up, w_down):
    B, S, D = h.shape
    t = h.reshape(B * S, D)
    tf = t.astype(jnp.float32)
    p_group = jax.nn.softmax(tf @ w_rg.astype(jnp.float32) + b_rg.astype(jnp.float32), axis=-1)
    p_top, g_idx = lax.top_k(p_group, 1)
    e_logits = (tf @ w_re.astype(jnp.float32) + b_re.astype(jnp.float32)).reshape(-1, N_GROUPS, EXPERTS_PER_GROUP)
    e_logits = jnp.take_along_axis(e_logits, g_idx[:, :, None], axis=1)[:, 0]
    p_exp = jax.nn.softmax(e_logits, axis=-1)
    w_top, e_idx = lax.top_k(p_exp, TOP_K)
    w_top = w_top / jnp.sum(w_top, axis=-1, keepdims=True) * p_top
    expert_id = g_idx * EXPERTS_PER_GROUP + e_idx
    combine = jnp.sum(jax.nn.one_hot(expert_id, N_EXPERTS, dtype=jnp.float32) * w_top[..., None], axis=1).astype(t.dtype)
    out = jnp.zeros_like(t)
    for g in range(N_GROUPS):
        sl = slice(g * EXPERTS_PER_GROUP, (g + 1) * EXPERTS_PER_GROUP)
        a = jnp.einsum('td,edf->tef', t, w_gate[sl])
        u = jnp.einsum('td,edf->tef', t, w_up[sl])
        act = jax.nn.silu(a) * u * combine[:, sl, None]
        out = out + jnp.einsum('tef,efd->td', act, w_down[sl])
    return out.reshape(B, S, D)


def setup_inputs(seed: int = 0) -> dict:
    key = jax.random.key(seed)
    ks = iter(jax.random.split(key, 32))

    def normal(shape, scale):
        return scale * jax.random.normal(next(ks), shape, jnp.float32)

    D, H, F = D_MODEL, MLSTM_HEADS, EXPERT_HIDDEN
    x = normal((BATCH, SEQ, D), 1.0)
    c = normal((BATCH, D), 1.0)
    w_ada = normal((DEPTH, D, 6 * D), 0.5 * D ** -0.5)
    b_ada = normal((DEPTH, 6 * D), 0.02)
    norm1_g = 1.0 + normal((DEPTH, D), 0.02)
    w_in = normal((DEPTH, D, IN_W), D ** -0.5)
    b_in = normal((DEPTH, IN_W), 0.02)
    f_off = 4 * MLSTM_W
    f_bias = jnp.linspace(3.0, 6.0, H, dtype=jnp.float32)
    b_in = b_in.at[:, f_off + H:f_off + 2 * H].add(f_bias).at[:, f_off + 3 * H:f_off + 4 * H].add(f_bias)
    conv_w = normal((DEPTH, CONV_W, 1, 2 * MLSTM_W), CONV_W ** -0.5)
    conv_b = normal((DEPTH, 2 * MLSTM_W), 0.02)
    mlstm_norm_g = 1.0 + normal((DEPTH, MLSTM_W), 0.02)
    q_norm_g = 1.0 + normal((DEPTH, ATTN_HD), 0.02)
    k_norm_g = 1.0 + normal((DEPTH, ATTN_HD), 0.02)
    w_branch_m = normal((DEPTH, MLSTM_W, D), MLSTM_W ** -0.5)
    w_branch_a = normal((DEPTH, ATTN_W, D), ATTN_W ** -0.5)
    w_out = normal((DEPTH, D, D), D ** -0.5)
    norm2_g = 1.0 + normal((DEPTH, D), 0.02)
    w_router_group = normal((DEPTH, D, N_GROUPS), D ** -0.5)
    b_router_group = normal((DEPTH, N_GROUPS), 0.01)
    w_router_expert = normal((DEPTH, D, N_EXPERTS), D ** -0.5)
    b_router_expert = normal((DEPTH, N_EXPERTS), 0.01)
    w_gate = normal((DEPTH, N_EXPERTS, D, F), D ** -0.5)
    w_up = normal((DEPTH, N_EXPERTS, D, F), D ** -0.5)
    w_down = normal((DEPTH, N_EXPERTS, F, D), F ** -0.5)
    final_norm_g = 1.0 + normal((D,), 0.02)
    return {'x': x, 'c': c, 'w_ada': w_ada, 'b_ada': b_ada, 'norm1_g': norm1_g, 'w_in': w_in, 'b_in': b_in,
            'conv_w': conv_w, 'conv_b': conv_b, 'mlstm_norm_g': mlstm_norm_g, 'q_norm_g': q_norm_g,
            'k_norm_g': k_norm_g, 'w_branch_m': w_branch_m, 'w_branch_a': w_branch_a, 'w_out': w_out,
            'norm2_g': norm2_g, 'w_router_group': w_router_group, 'b_router_group': b_router_group,
            'w_router_expert': w_router_expert, 'b_router_expert': b_router_expert, 'w_gate': w_gate,
            'w_up': w_up, 'w_down': w_down, 'final_norm_g': final_norm_g}


def reference(x, c, w_ada, b_ada, norm1_g, w_in, b_in, conv_w, conv_b, mlstm_norm_g, q_norm_g, k_norm_g,
              w_branch_m, w_branch_a, w_out, norm2_g, w_router_group, b_router_group, w_router_expert,
              b_router_expert, w_gate, w_up, w_down, final_norm_g):
    B, S, D = x.shape
    tables = rope_tables(S)
    offsets = np.cumsum(SPLIT_SIZES)[:-1].tolist()
    cond = jax.nn.silu(c)
    for l in range(DEPTH):
        mod = cond @ w_ada[l] + b_ada[l]
        shift1, scale1, gate1, shift2, scale2, gate2 = [m[:, None, :] for m in jnp.split(mod, 6, axis=-1)]
        h = rms_norm(x, norm1_g[l]) * (1 + scale1) + shift1
        proj = h @ w_in[l] + b_in[l]
        mq, mk, mv, mo, mgates, aq, ak, av, gm, ga = jnp.split(proj, offsets, axis=-1)
        y_m = mlstm_mixer(mq, mk, mv, mo, mgates, conv_w[l], conv_b[l], mlstm_norm_g[l])
        y_a = bidirectional_gqa(aq.reshape(B, S, ATTN_HEADS, ATTN_HD), ak.reshape(B, S, KV_HEADS, ATTN_HD),
                                av.reshape(B, S, KV_HEADS, ATTN_HD), q_norm_g[l], k_norm_g[l], tables)
        merged = jax.nn.sigmoid(gm) * (y_m @ w_branch_m[l]) + jax.nn.sigmoid(ga) * (y_a.astype(x.dtype) @ w_branch_a[l])
        x = x + gate1 * (merged @ w_out[l])
        h2 = rms_norm(x, norm2_g[l]) * (1 + scale2) + shift2
        x = x + gate2 * hierarchical_moe(h2, w_router_group[l], b_router_group[l], w_router_expert[l],
                                         b_router_expert[l], w_gate[l], w_up[l], w_down[l])
    return rms_norm(x, final_norm_g)
```

```python
import functools
import math

import jax
import jax.numpy as jnp
import numpy as np
from jax import lax
from jax.experimental import pallas as pl
from jax.experimental.pallas import tpu as pltpu

F32 = jnp.float32
BF16 = jnp.bfloat16
HIGHEST = lax.Precision.HIGHEST

GRID_W = 64
MLSTM_HEADS = 4
MLSTM_HD = 128
MLSTM_W = MLSTM_HEADS * MLSTM_HD
CONV_W = 5
ATTN_HEADS = 8
KV_HEADS = 2
ATTN_HD = 64
ATTN_W = ATTN_HEADS * ATTN_HD
KV_W = KV_HEADS * ATTN_HD
ROPE_BASE = 10000.0
N_GROUPS = 4
EXPERTS_PER_GROUP = 8
N_EXPERTS = N_GROUPS * EXPERTS_PER_GROUP
NORM_EPS = 1e-6

LANES = 128
MLSTM_CHUNK = 128
VMEM_LIMIT = 56 * 1024 * 1024

PM_W = 4 * MLSTM_W
PA_W = ATTN_W + 2 * KV_W
GG_W = None
ROUTER_ROWS = 8 + N_EXPERTS


def _cparams(sem, vmem=None):
    return pltpu.CompilerParams(dimension_semantics=sem, vmem_limit_bytes=vmem)


def _rms(x, g):
    ms = jnp.mean(x * x, axis=-1, keepdims=True)
    return x * lax.rsqrt(ms + NORM_EPS) * g


def _adaln_kernel(c_ref, w_ref, b_ref, o_ref):
    c = c_ref[...]
    cond = c * jax.nn.sigmoid(c)
    o_ref[0] = jnp.dot(cond, w_ref[0], precision=HIGHEST, preferred_element_type=F32) + b_ref[0]


def _adaln(c, w_ada, b_ada):
    depth, d, n = w_ada.shape
    b = c.shape[0]
    tn = 1536 if n % 1536 == 0 else n
    return pl.pallas_call(
        _adaln_kernel,
        out_shape=jax.ShapeDtypeStruct((depth, b, n), F32),
        grid=(depth, n // tn),
        in_specs=[
            pl.BlockSpec((b, d), lambda l, j: (0, 0)),
            pl.BlockSpec((1, d, tn), lambda l, j: (l, 0, j)),
            pl.BlockSpec((1, 1, tn), lambda l, j: (l, 0, j)),
        ],
        out_specs=pl.BlockSpec((1, b, tn), lambda l, j: (l, 0, j)),
        compiler_params=_cparams(("parallel", "parallel"), VMEM_LIMIT),
        name="adaln",
    )(c, w_ada, b_ada.reshape(depth, 1, n))


def _in_proj_kernel(x_ref, mod_ref, g_ref, w_ref, b_ref, wgt_ref, bgt_ref,
                    pm_ref, pa_ref, gg_ref, gt_ref, *, col_chunk):
    x = x_ref[...]
    y = _rms(x, g_ref[...])
    h = y * (1.0 + mod_ref[1:2, :]) + mod_ref[0:1, :]
    hb = h.astype(BF16)
    c0 = 0
    for o_ref in (pm_ref, pa_ref, gg_ref):
        width = o_ref.shape[-1]
        for a in range(0, width, col_chunk):
            e = min(a + col_chunk, width)
            acc = jnp.dot(hb, w_ref[:, c0 + a:c0 + e], preferred_element_type=F32)
            o_ref[:, a:e] = (acc + b_ref[:, c0 + a:c0 + e]).astype(o_ref.dtype)
        c0 += width
    gt = lax.dot_general(wgt_ref[...], hb, (((1,), (1,)), ((), ())), preferred_element_type=F32)
    gt_ref[...] = gt + bgt_ref[...]


def _in_proj(x, mod_l, g1, w_main, b_main, wgt, bgt, *, tm):
    b, s, d = x.shape
    nw = w_main.shape[1]
    gg_w = nw - PM_W - PA_W
    grows = wgt.shape[0]
    kern = functools.partial(_in_proj_kernel, col_chunk=512)
    return pl.pallas_call(
        kern,
        out_shape=(
            jax.ShapeDtypeStruct((b, s, PM_W), BF16),
            jax.ShapeDtypeStruct((b, s, PA_W), BF16),
            jax.ShapeDtypeStruct((b, s, gg_w), BF16),
            jax.ShapeDtypeStruct((b, grows, s), F32),
        ),
        grid=(b, s // tm),
        in_specs=[
            pl.BlockSpec((None, tm, d), lambda bi, i: (bi, i, 0)),
            pl.BlockSpec((None, 6, d), lambda bi, i: (bi, 0, 0)),
            pl.BlockSpec((1, d), lambda bi, i: (0, 0)),
            pl.BlockSpec((d, nw), lambda bi, i: (0, 0)),
            pl.BlockSpec((1, nw), lambda bi, i: (0, 0)),
            pl.BlockSpec((grows, d), lambda bi, i: (0, 0)),
            pl.BlockSpec((grows, 1), lambda bi, i: (0, 0)),
        ],
        out_specs=(
            pl.BlockSpec((None, tm, PM_W), lambda bi, i: (bi, i, 0)),
            pl.BlockSpec((None, tm, PA_W), lambda bi, i: (bi, i, 0)),
            pl.BlockSpec((None, tm, gg_w), lambda bi, i: (bi, i, 0)),
            pl.BlockSpec((None, grows, tm), lambda bi, i: (bi, 0, i)),
        ),
        compiler_params=_cparams(("parallel", "parallel"), VMEM_LIMIT),
        name="in_proj",
    )(x, mod_l, g1, w_main, b_main, wgt, bgt)


def _mlstm_kernel(q_ref, k_ref, v_ref, o_ref, gt_ref, cwq_ref, cwk_ref, cbq_ref, cbk_ref, ng_ref,
                  y_ref, xs_ref, qc_ref, kc_ref, tab_ref, hs_ref, stf_ref, stb_ref, mf_ref, mb_ref,
                  *, seq, blk):
    L = MLSTM_CHUNK
    nc = seq // L
    half = nc // 2
    pad = 8

    zeros_pad = jnp.zeros((pad, LANES), F32)
    xs_ref[0:pad, :] = zeros_pad
    xs_ref[pad + seq:pad + seq + pad, :] = zeros_pad

    def conv_pass(src_ref, w_ref, b_ref, dst_ref, scale):
        for r0 in range(0, seq, blk):
            xs_ref[pad + r0:pad + r0 + blk, :] = src_ref[r0:r0 + blk, :].astype(F32)
        for r0 in range(0, seq, blk):
            acc = jnp.zeros((blk, LANES), F32) + b_ref[...]
            for j in range(CONV_W):
                off = pad + r0 + j - CONV_W // 2
                acc = acc + w_ref[j:j + 1, :] * xs_ref[off:off + blk, :]
            yv = acc * jax.nn.sigmoid(acc) * scale
            dst_ref[r0:r0 + blk, :] = yv.astype(BF16)

    conv_pass(q_ref, cwq_ref, cbq_ref, qc_ref, MLSTM_HD ** -0.5)
    conv_pass(k_ref, cwk_ref, cbk_ref, kc_ref, 1.0)

    jj = lax.broadcasted_iota(jnp.int32, (L, L), 0)
    ll = lax.broadcasted_iota(jnp.int32, (L, L), 1)
    upper = (jj <= ll).astype(F32)
    lower = (jj >= ll).astype(F32)

    def gate_body(c, carry):
        t0 = pl.multiple_of(c * L, L)
        g = gt_ref[:, pl.ds(t0, L)]
        lf = jax.nn.log_sigmoid(g)
        cf = jnp.dot(lf, upper, precision=HIGHEST, preferred_element_type=F32)
        cb = jnp.dot(lf, lower, precision=HIGHEST, preferred_element_type=F32)
        row = lax.broadcasted_iota(jnp.int32, (8, L), 0)
        tab = jnp.where(row == 0, g, jnp.where(row == 1, cf, jnp.where(row == 2, g, cb)))
        tab = jnp.where(row < 4, tab, 0.0)
        tab_ref[:, pl.ds(t0, L)] = tab
        return carry

    lax.fori_loop(0, nc, gate_body, 0)

    srow = lax.broadcasted_iota(jnp.int32, (8, 4 * L), 0)
    scol = lax.broadcasted_iota(jnp.int32, (8, 4 * L), 1) // L
    sel = (srow == scol).astype(F32)

    stf_ref[...] = jnp.zeros_like(stf_ref)
    stb_ref[...] = jnp.zeros_like(stb_ref)
    mf_ref[...] = jnp.zeros_like(mf_ref)
    mb_ref[...] = jnp.zeros_like(mb_ref)
    ones_blk = jnp.ones((L, L), BF16)

    def chain(t0, st_ref, m_ref, fwd):
        q = qc_ref[pl.ds(t0, L), :]
        k = kc_ref[pl.ds(t0, L), :]
        v = v_ref[pl.ds(t0, L), :]
        r8 = tab_ref[:, pl.ds(t0, L)]
        cols = lax.dot_general(r8, sel, (((0,), (0,)), ((), ())), precision=HIGHEST,
                               preferred_element_type=F32)
        if fwd:
            li_c, b_c = cols[:, 0:L], cols[:, L:2 * L]
            li_r, b_r = r8[0:1, :], r8[1:2, :]
            bl = b_c[L - 1:L, :]
            mask = ll <= jj
        else:
            li_c, b_c = cols[:, 2 * L:3 * L], cols[:, 3 * L:4 * L]
            li_r, b_r = r8[2:3, :], r8[3:4, :]
            bl = b_c[0:1, :]
            mask = ll >= jj
        m_prev = m_ref[...]
        st = st_ref[...]
        a_c = bl - b_c + li_c
        a_max = jnp.max(a_c, axis=0, keepdims=True)
        w_c = jnp.exp(a_c - a_max)
        vw = (v.astype(F32) * w_c).astype(BF16)
        cat = jnp.concatenate([vw, w_c.astype(BF16)], axis=1)
        chunk2 = lax.dot_general(k, cat, (((0,), (0,)), ((), ())), preferred_element_type=F32)
        d = jnp.where(mask, b_c + (li_r - b_r), -jnp.inf)
        dmax = jnp.max(d, axis=1, keepdims=True)
        m_inter = b_c + m_prev
        m_out = jnp.maximum(m_inter, dmax)
        dw = jnp.exp(d - m_out)
        inter_w = jnp.exp(m_inter - m_out)
        s = lax.dot_general(q, k, (((1,), (1,)), ((), ())), preferred_element_type=F32) * dw
        vcat = jnp.concatenate([v, ones_blk], axis=1)
        r2 = jnp.dot(s.astype(BF16), vcat, preferred_element_type=F32)
        q2 = jnp.dot(q, st.astype(BF16), preferred_element_type=F32)
        num = r2[:, 0:L] + inter_w * q2[:, 0:L]
        den = r2[:, L:2 * L] + inter_w * q2[:, L:2 * L]
        hout = num / jnp.maximum(jnp.abs(den), jnp.exp(-m_out))
        m_new = jnp.maximum(bl + m_prev, a_max)
        decay = jnp.exp(bl + m_prev - m_new)
        inject = jnp.exp(a_max - m_new)
        st_ref[...] = decay[:, 0:1] * st + inject[:, 0:1] * chunk2
        m_ref[...] = m_new
        return hout

    def finalize(t0, h):
        yv = _rms(h, ng_ref[...]) * jax.nn.sigmoid(o_ref[pl.ds(t0, L), :].astype(F32))
        y_ref[pl.ds(t0, L), :] = yv.astype(y_ref.dtype)

    def first_half(c, carry):
        tf = pl.multiple_of(c * L, L)
        tb = pl.multiple_of((nc - 1 - c) * L, L)
        hs_ref[pl.ds(tf, L), :] = chain(tf, stf_ref, mf_ref, True)
        hs_ref[pl.ds(tb, L), :] = chain(tb, stb_ref, mb_ref, False)
        return carry

    def second_half(c, carry):
        tf = pl.multiple_of(c * L, L)
        tb = pl.multiple_of((nc - 1 - c) * L, L)
        finalize(tf, hs_ref[pl.ds(tf, L), :] + chain(tf, stf_ref, mf_ref, True))
        finalize(tb, hs_ref[pl.ds(tb, L), :] + chain(tb, stb_ref, mb_ref, False))
        return carry

    lax.fori_loop(0, half, first_half, 0)
    lax.fori_loop(half, nc, second_half, 0)


def _mlstm(pm, gt, conv_w, conv_b, norm_g):
    b, s, _ = pm.shape
    nh = MLSTM_HEADS
    assert s % (2 * MLSTM_CHUNK) == 0
    blk = 512 if s % 512 == 0 else MLSTM_CHUNK
    kern = functools.partial(_mlstm_kernel, seq=s, blk=blk)
    L = MLSTM_CHUNK
    col = lambda off: (lambda bi, h: (bi, 0, off + h))
    return pl.pallas_call(
        kern,
        out_shape=jax.ShapeDtypeStruct((b, s, MLSTM_W), BF16),
        grid=(b, nh),
        in_specs=[
            pl.BlockSpec((None, s, LANES), col(0)),
            pl.BlockSpec((None, s, LANES), col(nh)),
            pl.BlockSpec((None, s, LANES), col(2 * nh)),
            pl.BlockSpec((None, s, LANES), col(3 * nh)),
            pl.BlockSpec((None, 8, s), lambda bi, h: (bi, h, 0)),
            pl.BlockSpec((CONV_W, LANES), lambda bi, h: (0, h)),
            pl.BlockSpec((CONV_W, LANES), lambda bi, h: (0, nh + h)),
            pl.BlockSpec((1, LANES), lambda bi, h: (0, h)),
            pl.BlockSpec((1, LANES), lambda bi, h: (0, nh + h)),
            pl.BlockSpec((1, LANES), lambda bi, h: (0, h)),
        ],
        out_specs=pl.BlockSpec((None, s, LANES), lambda bi, h: (bi, 0, h)),
        scratch_shapes=[
            pltpu.VMEM((s + 16, LANES), F32),
            pltpu.VMEM((s, LANES), BF16),
            pltpu.VMEM((s, LANES), BF16),
            pltpu.VMEM((8, s), F32),
            pltpu.VMEM((s, LANES), F32),
            pltpu.VMEM((L, 2 * L), F32),
            pltpu.VMEM((L, 2 * L), F32),
            pltpu.VMEM((1, L), F32),
            pltpu.VMEM((1, L), F32),
        ],
        compiler_params=_cparams(("parallel", "parallel"), VMEM_LIMIT),
        name="mlstm",
    )(pm, pm, pm, pm, gt, conv_w, conv_w, conv_b, conv_b, norm_g)


def _swap16(y, lane):
    fwd = pltpu.roll(y, LANES - 16, axis=1)
    bwd = pltpu.roll(y, 16, axis=1)
    return jnp.where((lane % 32) < 16, fwd, bwd)


def _attn_prep_kernel(pa_ref, cos_ref, sin_ref, qg_ref, kg_ref, mseg_ref, qp_ref, kp_ref, vp_ref, *, q_scale):
    tm = pa_ref.shape[0]
    lane = lax.broadcasted_iota(jnp.int32, (tm, LANES), 1)
    lo = lane < ATTN_HD
    cos = cos_ref[...]
    sin = sin_ref[...]
    mseg = mseg_ref[...]

    def norm_rope(x, g, scale):
        ms = jnp.dot(x * x, mseg, precision=HIGHEST, preferred_element_type=F32)
        y = x * lax.rsqrt(ms + NORM_EPS) * g
        return (y * cos + _swap16(y, lane) * sin) * scale

    def split(y):
        return (jnp.where(lo, y, 0.0), jnp.where(lo, pltpu.roll(y, ATTN_HD, axis=1), 0.0))

    for j in range(ATTN_W // LANES):
        x = pa_ref[:, j * LANES:(j + 1) * LANES].astype(F32)
        a, b = split(norm_rope(x, qg_ref[...], q_scale))
        qp_ref[2 * j] = a.astype(BF16)
        qp_ref[2 * j + 1] = b.astype(BF16)
    for j in range(KV_W // LANES):
        x = pa_ref[:, ATTN_W + j * LANES:ATTN_W + (j + 1) * LANES].astype(F32)
        a, b = split(norm_rope(x, kg_ref[...], 1.0))
        kp_ref[2 * j] = a.astype(BF16)
        kp_ref[2 * j + 1] = b.astype(BF16)
        xv = pa_ref[:, ATTN_W + KV_W + j * LANES:ATTN_W + KV_W + (j + 1) * LANES].astype(F32)
        a, b = split(xv)
        one = jnp.where(lane == ATTN_HD, 1.0, 0.0)
        vp_ref[2 * j] = (a + one).astype(BF16)
        vp_ref[2 * j + 1] = (b + one).astype(BF16)


def _attn_prep(pa, cos_t, sin_t, qg, kg, mseg, *, tm):
    b, s, _ = pa.shape
    q_scale = (ATTN_HD ** -0.5) * math.log2(math.e)
    kern = functools.partial(_attn_prep_kernel, q_scale=q_scale)
    return pl.pallas_call(
        kern,
        out_shape=(
            jax.ShapeDtypeStruct((b, ATTN_HEADS, s, LANES), BF16),
            jax.ShapeDtypeStruct((b, KV_HEADS, s, LANES), BF16),
            jax.ShapeDtypeStruct((b, KV_HEADS, s, LANES), BF16),
        ),
        grid=(b, s // tm),
        in_specs=[
            pl.BlockSpec((None, tm, PA_W), lambda bi, i: (bi, i, 0)),
            pl.BlockSpec((tm, LANES), lambda bi, i: (i, 0)),
            pl.BlockSpec((tm, LANES), lambda bi, i: (i, 0)),
            pl.BlockSpec((1, LANES), lambda bi, i: (0, 0)),
            pl.BlockSpec((1, LANES), lambda bi, i: (0, 0)),
            pl.BlockSpec((LANES, LANES), lambda bi, i: (0, 0)),
        ],
        out_specs=(
            pl.BlockSpec((None, ATTN_HEADS, tm, LANES), lambda bi, i: (bi, 0, i, 0)),
            pl.BlockSpec((None, KV_HEADS, tm, LANES), lambda bi, i: (bi, 0, i, 0)),
            pl.BlockSpec((None, KV_HEADS, tm, LANES), lambda bi, i: (bi, 0, i, 0)),
        ),
        compiler_params=_cparams(("parallel", "parallel"), VMEM_LIMIT),
        name="attn_prep",
    )(pa, cos_t, sin_t, qg, kg, mseg)


def _attn_kernel(q_ref, k_ref, v_ref, o_ref, m_ref, acc_ref, *, tk):
    g, tq, _ = q_ref.shape
    s = k_ref.shape[0]
    q = q_ref[...].reshape(g * tq, LANES)
    m_ref[...] = jnp.full(m_ref.shape, -jnp.inf, F32)
    acc_ref[...] = jnp.zeros(acc_ref.shape, F32)

    def body(j, carry):
        t0 = pl.multiple_of(j * tk, tk)
        k = k_ref[pl.ds(t0, tk), :]
        v = v_ref[pl.ds(t0, tk), :]
        sc = lax.dot_general(q, k, (((1,), (1,)), ((), ())), preferred_element_type=F32)
        m_prev = m_ref[...]
        m_new = jnp.maximum(m_prev, jnp.max(sc, axis=1, keepdims=True))
        alpha = jnp.exp2(m_prev - m_new)
        p = jnp.exp2(sc - m_new)
        acc_ref[...] = alpha * acc_ref[...] + jnp.dot(p.astype(BF16), v, preferred_element_type=F32)
        m_ref[...] = m_new
        return carry

    lax.fori_loop(0, s // tk, body, 0)
    acc = acc_ref[...]
    o = acc / acc[:, ATTN_HD:ATTN_HD + 1]
    lane = lax.broadcasted_iota(jnp.int32, (tq, LANES), 1)
    for pair in range(g // 2):
        a = o[(2 * pair) * tq:(2 * pair + 1) * tq]
        b = o[(2 * pair + 1) * tq:(2 * pair + 2) * tq]
        o_ref[:, pair * LANES:(pair + 1) * LANES] = jnp.where(
            lane < ATTN_HD, a, pltpu.roll(b, ATTN_HD, axis=1)).astype(o_ref.dtype)


def _attention(qp, kp, vp, *, tq, tk):
    b, nh, s, _ = qp.shape
    g = nh // KV_HEADS
    kern = functools.partial(_attn_kernel, tk=tk)
    return pl.pallas_call(
        kern,
        out_shape=jax.ShapeDtypeStruct((b, s, ATTN_W), BF16),
        grid=(b, KV_HEADS, s // tq),
        in_specs=[
            pl.BlockSpec((None, g, tq, LANES), lambda bi, kv, i: (bi, kv, i, 0)),
            pl.BlockSpec((None, None, s, LANES), lambda bi, kv, i: (bi, kv, 0, 0)),
            pl.BlockSpec((None, None, s, LANES), lambda bi, kv, i: (bi, kv, 0, 0)),
        ],
        out_specs=pl.BlockSpec((None, tq, g * ATTN_HD), lambda bi, kv, i: (bi, i, kv)),
        scratch_shapes=[
            pltpu.VMEM((g * tq, 1), F32),
            pltpu.VMEM((g * tq, LANES), F32),
        ],
        compiler_params=_cparams(("parallel", "parallel", "parallel"), VMEM_LIMIT),
        name="attn",
    )(qp, kp, vp)


def _merge_kernel(x_ref, ym_ref, ya_ref, gg_ref, mod_ref, wbm_ref, wba_ref, wo_ref, g2_ref, wr_ref, br_ref,
                  xn_ref, h2_ref, lt_ref):
    d = x_ref.shape[-1]
    a = jnp.dot(ym_ref[...], wbm_ref[...], preferred_element_type=F32)
    bm = jnp.dot(ya_ref[...], wba_ref[...], preferred_element_type=F32)
    gm = jax.nn.sigmoid(gg_ref[:, 0:d].astype(F32))
    ga = jax.nn.sigmoid(gg_ref[:, d:2 * d].astype(F32))
    merged = (gm * a + ga * bm).astype(BF16)
    u = jnp.dot(merged, wo_ref[...], preferred_element_type=F32)
    xn = x_ref[...] + mod_ref[2:3, :] * u
    xn_ref[...] = xn
    h2 = _rms(xn, g2_ref[...]) * (1.0 + mod_ref[4:5, :]) + mod_ref[3:4, :]
    h2_ref[...] = h2
    lt = lax.dot_general(wr_ref[...], h2, (((1,), (1,)), ((), ())), precision=HIGHEST,
                         preferred_element_type=F32)
    lt_ref[...] = lt + br_ref[...]


def _merge(x, ym, ya, gg, mod_l, wbm, wba, wo, g2, wr_t, br_t, *, tm):
    b, s, d = x.shape
    ns = s // tm
    rr = wr_t.shape[0]
    full = lambda shp: pl.BlockSpec(shp, lambda bi, i: tuple(0 for _ in shp))
    return pl.pallas_call(
        _merge_kernel,
        out_shape=(
            jax.ShapeDtypeStruct((b, s, d), F32),
            jax.ShapeDtypeStruct((b * s, d), F32),
            jax.ShapeDtypeStruct((rr, b * s), F32),
        ),
        grid=(b, ns),
        in_specs=[
            pl.BlockSpec((None, tm, d), lambda bi, i: (bi, i, 0)),
            pl.BlockSpec((None, tm, MLSTM_W), lambda bi, i: (bi, i, 0)),
            pl.BlockSpec((None, tm, ATTN_W), lambda bi, i: (bi, i, 0)),
            pl.BlockSpec((None, tm, 2 * d), lambda bi, i: (bi, i, 0)),
            pl.BlockSpec((None, 6, d), lambda bi, i: (bi, 0, 0)),
            full((MLSTM_W, d)),
            full((ATTN_W, d)),
            full((d, d)),
            full((1, d)),
            full((rr, d)),
            full((rr, 1)),
        ],
        out_specs=(
            pl.BlockSpec((None, tm, d), lambda bi, i: (bi, i, 0)),
            pl.BlockSpec((tm, d), lambda bi, i: (bi * ns + i, 0)),
            pl.BlockSpec((rr, tm), lambda bi, i: (0, bi * ns + i)),
        ),
        compiler_params=_cparams(("parallel", "parallel"), VMEM_LIMIT),
        name="merge",
    )(x, ym, ya, gg, mod_l, wbm, wba, wo, g2, wr_t, br_t)


def _route_kernel(lt_ref, meta_ref, wts_ref, cnt_ref, carry_ref):
    tb = lt_ref.shape[1]
    epg = EXPERTS_PER_GROUP

    @pl.when(pl.program_id(0) == 0)
    def _():
        carry_ref[...] = jnp.zeros_like(carry_ref)

    row8 = lax.broadcasted_iota(jnp.int32, (8, tb), 0)
    gl = jnp.where(row8 < N_GROUPS, lt_ref[0:8, :], -jnp.inf)
    ge = jnp.exp(gl - jnp.max(gl, axis=0, keepdims=True))
    pg = ge / jnp.sum(ge, axis=0, keepdims=True)
    p_top = jnp.max(pg, axis=0, keepdims=True)
    g_idx = jnp.min(jnp.where(pg == p_top, row8, 8), axis=0, keepdims=True)

    el = jnp.zeros((epg, tb), F32)
    for g in range(N_GROUPS):
        el = jnp.where(g_idx == g, lt_ref[8 + g * epg:8 + (g + 1) * epg, :], el)
    ee = jnp.exp(el - jnp.max(el, axis=0, keepdims=True))
    pe = ee / jnp.sum(ee, axis=0, keepdims=True)
    v1 = jnp.max(pe, axis=0, keepdims=True)
    i1 = jnp.min(jnp.where(pe == v1, row8, 8), axis=0, keepdims=True)
    pe2 = jnp.where(row8 == i1, -1.0, pe)
    v2 = jnp.max(pe2, axis=0, keepdims=True)
    i2 = jnp.min(jnp.where(pe2 == v2, row8, 8), axis=0, keepdims=True)
    denom = v1 + v2
    w0 = v1 / denom * p_top
    w1 = v2 / denom * p_top
    e0 = g_idx * epg + i1
    e1 = g_idx * epg + i2

    rowe = lax.broadcasted_iota(jnp.int32, (N_EXPERTS, tb), 0)
    oh0 = rowe == e0
    oh1 = rowe == e1
    oh = jnp.where(oh0 | oh1, 1.0, 0.0)
    src = lax.broadcasted_iota(jnp.int32, (tb, tb), 0)
    dst = lax.broadcasted_iota(jnp.int32, (tb, tb), 1)
    strict = jnp.where(src < dst, 1.0, 0.0).astype(BF16)
    cum = jnp.dot(oh.astype(BF16), strict, preferred_element_type=F32)
    base = carry_ref[:, 0:1] + cum
    r0 = jnp.sum(jnp.where(oh0, base, 0.0), axis=0, keepdims=True)
    r1 = jnp.sum(jnp.where(oh1, base, 0.0), axis=0, keepdims=True)
    total = carry_ref[...] + jnp.sum(oh, axis=1, keepdims=True)
    carry_ref[...] = total
    cnt_ref[...] = total.astype(jnp.int32)

    meta = jnp.where(row8 == 0, e0, jnp.where(row8 == 1, e1, jnp.where(
        row8 == 2, r0.astype(jnp.int32), jnp.where(row8 == 3, r1.astype(jnp.int32), 0))))
    meta_ref[...] = meta
    wts_ref[...] = jnp.where(row8 == 0, w0, jnp.where(row8 == 1, w1, 0.0))


def _route(lt, *, tb):
    rr, t = lt.shape
    return pl.pallas_call(
        _route_kernel,
        out_shape=(
            jax.ShapeDtypeStruct((8, t), jnp.int32),
            jax.ShapeDtypeStruct((8, t), F32),
            jax.ShapeDtypeStruct((N_EXPERTS, LANES), jnp.int32),
        ),
        grid=(t // tb,),
        in_specs=[pl.BlockSpec((rr, tb), lambda i: (0, i))],
        out_specs=(
            pl.BlockSpec((8, tb), lambda i: (0, i)),
            pl.BlockSpec((8, tb), lambda i: (0, i)),
            pl.BlockSpec((N_EXPERTS, LANES), lambda i: (0, 0)),
        ),
        scratch_shapes=[pltpu.VMEM((N_EXPERTS, LANES), F32)],
        compiler_params=_cparams(("arbitrary",), VMEM_LIMIT),
        name="route",
    )(lt)


def _dispatch_kernel(dest_ref, h_ref, xg_ref, zero_ref, sem, zsem, *, n_tok, zrows):
    i = pl.program_id(0)
    tb = h_ref.shape[0]
    p_rows = xg_ref.shape[0]

    @pl.when(i == 0)
    def _():
        zero_ref[...] = jnp.zeros_like(zero_ref)
        for r0 in range(0, p_rows, zrows):
            pltpu.make_async_copy(zero_ref, xg_ref.at[pl.ds(r0, zrows)], zsem).start()
        for r0 in range(0, p_rows, zrows):
            pltpu.make_async_copy(zero_ref, xg_ref.at[pl.ds(r0, zrows)], zsem).wait()

    base = i * tb

    def issue(r, carry):
        src = h_ref.at[pl.ds(r, 1)]
        d0 = dest_ref[base + r]
        d1 = dest_ref[n_tok + base + r]
        pltpu.make_async_copy(src, xg_ref.at[pl.ds(d0, 1)], sem).start()
        pltpu.make_async_copy(src, xg_ref.at[pl.ds(d1, 1)], sem).start()
        return carry

    lax.fori_loop(0, tb, issue, 0)

    def drain(r, carry):
        pltpu.make_async_copy(h_ref.at[pl.ds(0, 1)], xg_ref.at[pl.ds(0, 1)], sem).wait()
        pltpu.make_async_copy(h_ref.at[pl.ds(0, 1)], xg_ref.at[pl.ds(0, 1)], sem).wait()
        return carry

    lax.fori_loop(0, tb, drain, 0)


def _dispatch(dest, h2, p_rows, *, tb, zrows):
    t, d = h2.shape
    kern = functools.partial(_dispatch_kernel, n_tok=t, zrows=zrows)
    return pl.pallas_call(
        kern,
        out_shape=jax.ShapeDtypeStruct((p_rows, d), F32),
        grid_spec=pltpu.PrefetchScalarGridSpec(
            num_scalar_prefetch=1,
            grid=(t // tb,),
            in_specs=[pl.BlockSpec((tb, d), lambda i, dest: (i, 0))],
            out_specs=pl.BlockSpec(memory_space=pl.ANY),
            scratch_shapes=[
                pltpu.VMEM((zrows, d), F32),
                pltpu.SemaphoreType.DMA(()),
                pltpu.SemaphoreType.DMA(()),
            ],
        ),
        compiler_params=_cparams(("arbitrary",), VMEM_LIMIT),
        name="dispatch",
    )(dest, h2)


def _experts_kernel(te_ref, first_ref, used_ref, x_ref, wg_ref, wu_ref, wd_ref, y_ref, wgb, wub, wdb):
    i = pl.program_id(0)

    @pl.when(first_ref[i] == 1)
    def _():
        wgb[...] = wg_ref[...].astype(BF16)
        wub[...] = wu_ref[...].astype(BF16)
        wdb[...] = wd_ref[...].astype(BF16)

    @pl.when(used_ref[i] == 1)
    def _():
        x = x_ref[...].astype(BF16)
        a = jnp.dot(x, wgb[...], preferred_element_type=F32)
        u = jnp.dot(x, wub[...], preferred_element_type=F32)
        act = (a * jax.nn.sigmoid(a) * u).astype(BF16)
        y_ref[...] = jnp.dot(act, wdb[...], preferred_element_type=F32)

    @pl.when(used_ref[i] == 0)
    def _():
        y_ref[...] = jnp.zeros_like(y_ref)


def _experts(tile_expert, tile_first, tile_used, xg, w_gate, w_up, w_down, *, tm):
    p_rows, d = xg.shape
    _, _, f = w_gate.shape
    nt = p_rows // tm
    return pl.pallas_call(
        _experts_kernel,
        out_shape=jax.ShapeDtypeStruct((p_rows, d), F32),
        grid_spec=pltpu.PrefetchScalarGridSpec(
            num_scalar_prefetch=3,
            grid=(nt,),
            in_specs=[
                pl.BlockSpec((tm, d), lambda i, te, fi, us: (i, 0)),
                pl.BlockSpec((None, d, f), lambda i, te, fi, us: (te[i], 0, 0)),
                pl.BlockSpec((None, d, f), lambda i, te, fi, us: (te[i], 0, 0)),
                pl.BlockSpec((None, f, d), lambda i, te, fi, us: (te[i], 0, 0)),
            ],
            out_specs=pl.BlockSpec((tm, d), lambda i, te, fi, us: (i, 0)),
            scratch_shapes=[
                pltpu.VMEM((d, f), BF16),
                pltpu.VMEM((d, f), BF16),
                pltpu.VMEM((f, d), BF16),
            ],
        ),
        compiler_params=_cparams(("arbitrary",), VMEM_LIMIT),
        name="experts",
    )(tile_expert, tile_first, tile_used, xg, w_gate, w_up, w_down)


def _combine_kernel(dest_ref, x_ref, w_ref, mod_ref, fg_ref, y_ref, o_ref, buf_ref, sem, *, n_tok, final):
    i = pl.program_id(0) * pl.num_programs(1) + pl.program_id(1)
    tb = x_ref.shape[0]
    base = i * tb

    def issue(r, carry):
        d0 = dest_ref[base + r]
        d1 = dest_ref[n_tok + base + r]
        pltpu.make_async_copy(y_ref.at[pl.ds(d0, 1)], buf_ref.at[0, pl.ds(r, 1)], sem).start()
        pltpu.make_async_copy(y_ref.at[pl.ds(d1, 1)], buf_ref.at[1, pl.ds(r, 1)], sem).start()
        return carry

    lax.fori_loop(0, tb, issue, 0)

    def drain(r, carry):
        pltpu.make_async_copy(y_ref.at[pl.ds(0, 1)], buf_ref.at[0, pl.ds(0, 1)], sem).wait()
        pltpu.make_async_copy(y_ref.at[pl.ds(0, 1)], buf_ref.at[1, pl.ds(0, 1)], sem).wait()
        return carry

    lax.fori_loop(0, tb, drain, 0)
    moe = w_ref[:, 0:1] * buf_ref[0] + w_ref[:, 1:2] * buf_ref[1]
    xo = x_ref[...] + mod_ref[5:6, :] * moe
    if final:
        xo = _rms(xo, fg_ref[...])
    o_ref[...] = xo


def _combine(dest, xn, wts_col, mod_l, fg, y, *, tb, final):
    b, s, d = xn.shape
    ns = s // tb
    kern = functools.partial(_combine_kernel, n_tok=b * s, final=final)
    return pl.pallas_call(
        kern,
        out_shape=jax.ShapeDtypeStruct((b, s, d), F32),
        grid_spec=pltpu.PrefetchScalarGridSpec(
            num_scalar_prefetch=1,
            grid=(b, ns),
            in_specs=[
                pl.BlockSpec((None, tb, d), lambda bi, i, dest: (bi, i, 0)),
                pl.BlockSpec((tb, 8), lambda bi, i, dest: (bi * ns + i, 0)),
                pl.BlockSpec((None, 6, d), lambda bi, i, dest: (bi, 0, 0)),
                pl.BlockSpec((1, d), lambda bi, i, dest: (0, 0)),
                pl.BlockSpec(memory_space=pl.ANY),
            ],
            out_specs=pl.BlockSpec((None, tb, d), lambda bi, i, dest: (bi, i, 0)),
            scratch_shapes=[
                pltpu.VMEM((2, tb, d), F32),
                pltpu.SemaphoreType.DMA(()),
            ],
        ),
        compiler_params=_cparams(("arbitrary", "arbitrary"), VMEM_LIMIT),
        name="combine",
    )(dest, xn, wts_col, mod_l, fg, y)


def _rope_tables(seq):
    rows = seq // GRID_W
    row = jnp.repeat(jnp.arange(rows, dtype=F32), GRID_W)
    col = jnp.tile(jnp.arange(GRID_W, dtype=F32), rows)
    half = ATTN_HD // 2
    inv_freq = ROPE_BASE ** (-jnp.arange(0, half, 2, dtype=F32) / half)
    ang_r = row[:, None] * inv_freq
    ang_c = col[:, None] * inv_freq
    cos64 = jnp.concatenate([jnp.cos(ang_r), jnp.cos(ang_r), jnp.cos(ang_c), jnp.cos(ang_c)], axis=-1)
    sin64 = jnp.concatenate([-jnp.sin(ang_r), jnp.sin(ang_r), -jnp.sin(ang_c), jnp.sin(ang_c)], axis=-1)
    return jnp.tile(cos64, (1, LANES // ATTN_HD)), jnp.tile(sin64, (1, LANES // ATTN_HD))


def _plan_tiles(counts, tm, nt):
    tiles = (counts + tm - 1) // tm
    tile_end = jnp.cumsum(tiles)
    tile_start = tile_end - tiles
    offsets = (tile_start * tm).astype(jnp.int32)
    tidx = jnp.arange(nt, dtype=jnp.int32)
    te = jnp.sum((tidx[:, None] >= tile_end[None, :]).astype(jnp.int32), axis=1)
    used = (tidx < tile_end[-1]).astype(jnp.int32)
    last_used = jnp.sum((tile_end[-1] - 1 >= tile_end).astype(jnp.int32))
    te = jnp.where(used == 1, te, last_used).astype(jnp.int32)
    prev = jnp.concatenate([jnp.full((1,), -1, jnp.int32), te[:-1]])
    first = ((te != prev) & (used == 1)).astype(jnp.int32)
    return offsets, te, first, used


def kernel(x, c, w_ada, b_ada, norm1_g, w_in, b_in, conv_w, conv_b, mlstm_norm_g, q_norm_g, k_norm_g,
           w_branch_m, w_branch_a, w_out, norm2_g, w_router_group, b_router_group, w_router_expert,
           b_router_expert, w_gate, w_up, w_down, final_norm_g):
    b, s, d = x.shape
    depth = w_ada.shape[0]
    t = b * s
    nh = MLSTM_HEADS
    tm = 512 if s % 512 == 0 else 128
    tq = 256 if s % 256 == 0 else 128
    tk = 512 if s % 512 == 0 else 128
    tme = 256
    tbr = 512 if t % 512 == 0 else 128
    tbd = 256 if s % 256 == 0 else 128
    nt = (2 * t) // tme + N_EXPERTS
    p_rows = nt * tme
    zrows = tme

    mod = _adaln(c, w_ada, b_ada).reshape(depth, b, 6, d)
    cos_t, sin_t = _rope_tables(s)
    seg = np.arange(LANES) // ATTN_HD
    mseg = jnp.asarray((seg[:, None] == seg[None, :]).astype(np.float32) / ATTN_HD)

    o_mq, o_mk, o_mv, o_mo = 0, MLSTM_W, 2 * MLSTM_W, 3 * MLSTM_W
    o_g = 4 * MLSTM_W
    o_aq = o_g + 4 * nh
    o_ak = o_aq + ATTN_W
    o_av = o_ak + KV_W
    o_gm = o_av + KV_W
    o_ga = o_gm + d
    main_cols = np.concatenate([np.arange(o_mq, o_g), np.arange(o_aq, o_ga + d)])
    gate_cols = np.array([[o_g + ty * nh + h for ty in range(4)] for h in range(nh)])

    for l in range(depth):
        w_l = w_in[l]
        w_main = w_l[:, main_cols].astype(BF16)
        b_main = b_in[l][main_cols][None, :]
        wg = jnp.transpose(w_l[:, gate_cols.reshape(-1)]).reshape(nh, 4, d)
        wgt = jnp.concatenate([wg, jnp.zeros((nh, 4, d), F32)], axis=1).reshape(nh * 8, d).astype(BF16)
        bg = b_in[l][gate_cols.reshape(-1)].reshape(nh, 4)
        bgt = jnp.concatenate([bg, jnp.zeros((nh, 4), F32)], axis=1).reshape(nh * 8, 1)

        pm, pa, gg, gt = _in_proj(x, mod[l], norm1_g[l][None, :], w_main, b_main, wgt, bgt, tm=tm)
        ym = _mlstm(pm, gt, conv_w[l].reshape(CONV_W, 2 * MLSTM_W), conv_b[l][None, :],
                    mlstm_norm_g[l][None, :])
        qg = jnp.tile(q_norm_g[l], LANES // ATTN_HD)[None, :]
        kg = jnp.tile(k_norm_g[l], LANES // ATTN_HD)[None, :]
        qp, kp, vp = _attn_prep(pa, cos_t, sin_t, qg, kg, mseg, tm=tm)
        ya = _attention(qp, kp, vp, tq=tq, tk=tk)

        wr = jnp.concatenate([w_router_group[l], jnp.zeros((d, 8 - N_GROUPS), F32), w_router_expert[l]], axis=1)
        br = jnp.concatenate([b_router_group[l], jnp.zeros((8 - N_GROUPS,), F32), b_router_expert[l]])
        xn, h2, lt = _merge(x, ym, ya, gg, mod[l], w_branch_m[l].astype(BF16), w_branch_a[l].astype(BF16),
                            w_out[l].astype(BF16), norm2_g[l][None, :], jnp.transpose(wr), br[:, None], tm=tm)

        meta, wts, cnt = _route(lt, tb=tbr)
        offsets, te, first, used = _plan_tiles(cnt[:, 0], tme, nt)
        dest = (offsets[meta[0:2]] + meta[2:4]).reshape(-1)
        xg = _dispatch(dest, h2, p_rows, tb=tbd, zrows=zrows)
        y = _experts(te, first, used, xg, w_gate[l], w_up[l], w_down[l], tm=tme)
        x = _combine(dest, xn, jnp.transpose(wts), mod[l], final_norm_g[None, :], y,
                     tb=tbd, final=(l == depth - 1))
    return x
```

```python
import functools
import math

import jax
import jax.numpy as jnp
import numpy as np
from jax import lax
from jax.experimental import pallas as pl
from jax.experimental.pallas import tpu as pltpu

F32 = jnp.float32
BF16 = jnp.bfloat16
HIGHEST = lax.Precision.HIGHEST

GRID_W = 64
MLSTM_HEADS = 4
MLSTM_HD = 128
MLSTM_W = MLSTM_HEADS * MLSTM_HD
CONV_W = 5
ATTN_HEADS = 8
KV_HEADS = 2
ATTN_HD = 64
ATTN_W = ATTN_HEADS * ATTN_HD
KV_W = KV_HEADS * ATTN_HD
ROPE_BASE = 10000.0
N_GROUPS = 4
EXPERTS_PER_GROUP = 8
N_EXPERTS = N_GROUPS * EXPERTS_PER_GROUP
NORM_EPS = 1e-6

LANES = 128
MLSTM_CHUNK = 128
VMEM_LIMIT = 56 * 1024 * 1024

PM_W = 4 * MLSTM_W
PA_W = ATTN_W + 2 * KV_W
GG_W = None
ROUTER_ROWS = 8 + N_EXPERTS


def _cparams(sem, vmem=None):
    return pltpu.CompilerParams(dimension_semantics=sem, vmem_limit_bytes=vmem)


def _rms(x, g):
    ms = jnp.mean(x * x, axis=-1, keepdims=True)
    return x * lax.rsqrt(ms + NORM_EPS) * g


def _adaln_kernel(c_ref, w_ref, b_ref, o_ref):
    c = c_ref[...]
    cond = c * jax.nn.sigmoid(c)
    o_ref[0] = jnp.dot(cond, w_ref[0], precision=HIGHEST, preferred_element_type=F32) + b_ref[0]


def _adaln(c, w_ada, b_ada):
    depth, d, n = w_ada.shape
    b = c.shape[0]
    tn = 1536 if n % 1536 == 0 else n
    return pl.pallas_call(
        _adaln_kernel,
        out_shape=jax.ShapeDtypeStruct((depth, b, n), F32),
        grid=(depth, n // tn),
        in_specs=[
            pl.BlockSpec((b, d), lambda l, j: (0, 0)),
            pl.BlockSpec((1, d, tn), lambda l, j: (l, 0, j)),
            pl.BlockSpec((1, 1, tn), lambda l, j: (l, 0, j)),
        ],
        out_specs=pl.BlockSpec((1, b, tn), lambda l, j: (l, 0, j)),
        compiler_params=_cparams(("parallel", "parallel"), VMEM_LIMIT),
        name="adaln",
    )(c, w_ada, b_ada.reshape(depth, 1, n))


def _in_proj_kernel(x_ref, mod_ref, g_ref, w_ref, b_ref, wgt_ref, bgt_ref,
                    pm_ref, pa_ref, gg_ref, gt_ref, *, col_chunk):
    x = x_ref[...]
    y = _rms(x, g_ref[...])
    h = y * (1.0 + mod_ref[1:2, :]) + mod_ref[0:1, :]
    hb = h.astype(BF16)
    c0 = 0
    for o_ref in (pm_ref, pa_ref, gg_ref):
        width = o_ref.shape[-1]
        for a in range(0, width, col_chunk):
            e = min(a + col_chunk, width)
            acc = jnp.dot(hb, w_ref[:, c0 + a:c0 + e], preferred_element_type=F32)
            o_ref[:, a:e] = (acc + b_ref[:, c0 + a:c0 + e]).astype(o_ref.dtype)
        c0 += width
    gt = lax.dot_general(wgt_ref[...], hb, (((1,), (1,)), ((), ())), preferred_element_type=F32)
    gt_ref[...] = gt + bgt_ref[...]


def _in_proj(x, mod_l, g1, w_main, b_main, wgt, bgt, *, tm):
    b, s, d = x.shape
    nw = w_main.shape[1]
    gg_w = nw - PM_W - PA_W
    grows = wgt.shape[0]
    kern = functools.partial(_in_proj_kernel, col_chunk=512)
    return pl.pallas_call(
        kern,
        out_shape=(
            jax.ShapeDtypeStruct((b, s, PM_W), BF16),
            jax.ShapeDtypeStruct((b, s, PA_W), BF16),
            jax.ShapeDtypeStruct((b, s, gg_w), BF16),
            jax.ShapeDtypeStruct((b, grows, s), F32),
        ),
        grid=(b, s // tm),
        in_specs=[
            pl.BlockSpec((None, tm, d), lambda bi, i: (bi, i, 0)),
            pl.BlockSpec((None, 6, d), lambda bi, i: (bi, 0, 0)),
            pl.BlockSpec((1, d), lambda bi, i: (0, 0)),
            pl.BlockSpec((d, nw), lambda bi, i: (0, 0)),
            pl.BlockSpec((1, nw), lambda bi, i: (0, 0)),
            pl.BlockSpec((grows, d), lambda bi, i: (0, 0)),
            pl.BlockSpec((grows, 1), lambda bi, i: (0, 0)),
        ],
        out_specs=(
            pl.BlockSpec((None, tm, PM_W), lambda bi, i: (bi, i, 0)),
            pl.BlockSpec((None, tm, PA_W), lambda bi, i: (bi, i, 0)),
            pl.BlockSpec((None, tm, gg_w), lambda bi, i: (bi, i, 0)),
            pl.BlockSpec((None, grows, tm), lambda bi, i: (bi, 0, i)),
        ),
        compiler_params=_cparams(("parallel", "parallel"), VMEM_LIMIT),
        name="in_proj",
    )(x, mod_l, g1, w_main, b_main, wgt, bgt)


def _mlstm_kernel(q_ref, k_ref, v_ref, o_ref, gt_ref, cwq_ref, cwk_ref, cbq_ref, cbk_ref, ng_ref,
                  y_ref, xs_ref, qc_ref, kc_ref, tab_ref, hs_ref, stf_ref, stb_ref, mf_ref, mb_ref,
                  *, seq, blk):
    L = MLSTM_CHUNK
    nc = seq // L
    half = nc // 2
    pad = 8

    zeros_pad = jnp.zeros((pad, LANES), F32)
    xs_ref[0:pad, :] = zeros_pad
    xs_ref[pad + seq:pad + seq + pad, :] = zeros_pad

    def conv_pass(src_ref, w_ref, b_ref, dst_ref, scale):
        for r0 in range(0, seq, blk):
            xs_ref[pad + r0:pad + r0 + blk, :] = src_ref[r0:r0 + blk, :].astype(F32)
        for r0 in range(0, seq, blk):
            acc = jnp.zeros((blk, LANES), F32) + b_ref[...]
            for j in range(CONV_W):
                off = pad + r0 + j - CONV_W // 2
                acc = acc + w_ref[j:j + 1, :] * xs_ref[off:off + blk, :]
            yv = acc * jax.nn.sigmoid(acc) * scale
            dst_ref[r0:r0 + blk, :] = yv.astype(BF16)

    conv_pass(q_ref, cwq_ref, cbq_ref, qc_ref, MLSTM_HD ** -0.5)
    conv_pass(k_ref, cwk_ref, cbk_ref, kc_ref, 1.0)

    jj = lax.broadcasted_iota(jnp.int32, (L, L), 0)
    ll = lax.broadcasted_iota(jnp.int32, (L, L), 1)
    upper = (jj <= ll).astype(F32)
    lower = (jj >= ll).astype(F32)

    def gate_body(c, carry):
        t0 = pl.multiple_of(c * L, L)
        g = gt_ref[:, pl.ds(t0, L)]
        lf = jax.nn.log_sigmoid(g)
        cf = jnp.dot(lf, upper, precision=HIGHEST, preferred_element_type=F32)
        cb = jnp.dot(lf, lower, precision=HIGHEST, preferred_element_type=F32)
        row = lax.broadcasted_iota(jnp.int32, (8, L), 0)
        tab = jnp.where(row == 0, g, jnp.where(row == 1, cf, jnp.where(row == 2, g, cb)))
        tab = jnp.where(row < 4, tab, 0.0)
        tab_ref[:, pl.ds(t0, L)] = tab
        return carry

    lax.fori_loop(0, nc, gate_body, 0)

    srow = lax.broadcasted_iota(jnp.int32, (8, 4 * L), 0)
    scol = lax.broadcasted_iota(jnp.int32, (8, 4 * L), 1) // L
    sel = (srow == scol).astype(F32)

    stf_ref[...] = jnp.zeros_like(stf_ref)
    stb_ref[...] = jnp.zeros_like(stb_ref)
    mf_ref[...] = jnp.zeros_like(mf_ref)
    mb_ref[...] = jnp.zeros_like(mb_ref)
    ones_blk = jnp.ones((L, L), BF16)

    def chain(t0, st_ref, m_ref, fwd):
        q = qc_ref[pl.ds(t0, L), :]
        k = kc_ref[pl.ds(t0, L), :]
        v = v_ref[pl.ds(t0, L), :]
        r8 = tab_ref[:, pl.ds(t0, L)]
        cols = lax.dot_general(r8, sel, (((0,), (0,)), ((), ())), precision=HIGHEST,
                               preferred_element_type=F32)
        if fwd:
            li_c, b_c = cols[:, 0:L], cols[:, L:2 * L]
            li_r, b_r = r8[0:1, :], r8[1:2, :]
            bl = b_c[L - 1:L, :]
            mask = ll <= jj
        else:
            li_c, b_c = cols[:, 2 * L:3 * L], cols[:, 3 * L:4 * L]
            li_r, b_r = r8[2:3, :], r8[3:4, :]
            bl = b_c[0:1, :]
            mask = ll >= jj
        m_prev = m_ref[...]
        st = st_ref[...]
        a_c = bl - b_c + li_c
        a_max = jnp.max(a_c, axis=0, keepdims=True)
        w_c = jnp.exp(a_c - a_max)
        vw = (v.astype(F32) * w_c).astype(BF16)
        cat = jnp.concatenate([vw, w_c.astype(BF16)], axis=1)
        chunk2 = lax.dot_general(k, cat, (((0,), (0,)), ((), ())), preferred_element_type=F32)
        d = jnp.where(mask, b_c + (li_r - b_r), -jnp.inf)
        dmax = jnp.max(d, axis=1, keepdims=True)
        m_inter = b_c + m_prev
        m_out = jnp.maximum(m_inter, dmax)
        dw = jnp.exp(d - m_out)
        inter_w = jnp.exp(m_inter - m_out)
        s = lax.dot_general(q, k, (((1,), (1,)), ((), ())), preferred_element_type=F32) * dw
        vcat = jnp.concatenate([v, ones_blk], axis=1)
        r2 = jnp.dot(s.astype(BF16), vcat, preferred_element_type=F32)
        q2 = jnp.dot(q, st.astype(BF16), preferred_element_type=F32)
        num = r2[:, 0:L] + inter_w * q2[:, 0:L]
        den = r2[:, L:2 * L] + inter_w * q2[:, L:2 * L]
        hout = num / jnp.maximum(jnp.abs(den), jnp.exp(-m_out))
        m_new = jnp.maximum(bl + m_prev, a_max)
        decay = jnp.exp(bl + m_prev - m_new)
        inject = jnp.exp(a_max - m_new)
        st_ref[...] = decay[:, 0:1] * st + inject[:, 0:1] * chunk2
        m_ref[...] = m_new
        return hout

    def finalize(t0, h):
        yv = _rms(h, ng_ref[...]) * jax.nn.sigmoid(o_ref[pl.ds(t0, L), :].astype(F32))
        y_ref[pl.ds(t0, L), :] = yv.astype(y_ref.dtype)

    def first_half(c, carry):
        tf = pl.multiple_of(c * L, L)
        tb = pl.multiple_of((nc - 1 - c) * L, L)
        hs_ref[pl.ds(tf, L), :] = chain(tf, stf_ref, mf_ref, True)
        hs_ref[pl.ds(tb, L), :] = chain(tb, stb_ref, mb_ref, False)
        return carry

    def second_half(c, carry):
        tf = pl.multiple_of(c * L, L)
        tb = pl.multiple_of((nc - 1 - c) * L, L)
        finalize(tf, hs_ref[pl.ds(tf, L), :] + chain(tf, stf_ref, mf_ref, True))
        finalize(tb, hs_ref[pl.ds(tb, L), :] + chain(tb, stb_ref, mb_ref, False))
        return carry

    lax.fori_loop(0, half, first_half, 0)
    lax.fori_loop(half, nc, second_half, 0)


def _mlstm(pm, gt, conv_w, conv_b, norm_g):
    b, s, _ = pm.shape
    nh = MLSTM_HEADS
    assert s % (2 * MLSTM_CHUNK) == 0
    blk = 512 if s % 512 == 0 else MLSTM_CHUNK
    kern = functools.partial(_mlstm_kernel, seq=s, blk=blk)
    L = MLSTM_CHUNK
    col = lambda off: (lambda bi, h: (bi, 0, off + h))
    return pl.pallas_call(
        kern,
        out_shape=jax.ShapeDtypeStruct((b, s, MLSTM_W), BF16),
        grid=(b, nh),
        in_specs=[
            pl.BlockSpec((None, s, LANES), col(0)),
            pl.BlockSpec((None, s, LANES), col(nh)),
            pl.BlockSpec((None, s, LANES), col(2 * nh)),
            pl.BlockSpec((None, s, LANES), col(3 * nh)),
            pl.BlockSpec((None, 8, s), lambda bi, h: (bi, h, 0)),
            pl.BlockSpec((CONV_W, LANES), lambda bi, h: (0, h)),
            pl.BlockSpec((CONV_W, LANES), lambda bi, h: (0, nh + h)),
            pl.BlockSpec((1, LANES), lambda bi, h: (0, h)),
            pl.BlockSpec((1, LANES), lambda bi, h: (0, nh + h)),
            pl.BlockSpec((1, LANES), lambda bi, h: (0, h)),
        ],
        out_specs=pl.BlockSpec((None, s, LANES), lambda bi, h: (bi, 0, h)),
        scratch_shapes=[
            pltpu.VMEM((s + 16, LANES), F32),
            pltpu.VMEM((s, LANES), BF16),
            pltpu.VMEM((s, LANES), BF16),
            pltpu.VMEM((8, s), F32),
            pltpu.VMEM((s, LANES), F32),
            pltpu.VMEM((L, 2 * L), F32),
            pltpu.VMEM((L, 2 * L), F32),
            pltpu.VMEM((1, L), F32),
            pltpu.VMEM((1, L), F32),
        ],
        compiler_params=_cparams(("parallel", "parallel"), VMEM_LIMIT),
        name="mlstm",
    )(pm, pm, pm, pm, gt, conv_w, conv_w, conv_b, conv_b, norm_g)


def _swap16(y, lane):
    fwd = pltpu.roll(y, LANES - 16, axis=1)
    bwd = pltpu.roll(y, 16, axis=1)
    return jnp.where((lane % 32) < 16, fwd, bwd)


def _nt(a, b):
    return lax.dot_general(a, b, (((1,), (1,)), ((), ())), preferred_element_type=F32)


def _attn_prep_kernel(pa_ref, cos_ref, sin_ref, qg_ref, kg_ref, mseg_ref, qt_ref, kp_ref, vt_ref, *, q_scale):
    tm = pa_ref.shape[0]
    lane = lax.broadcasted_iota(jnp.int32, (tm, LANES), 1)
    cos = cos_ref[...]
    sin = sin_ref[...]
    mseg = mseg_ref[...]
    r = lax.broadcasted_iota(jnp.int32, (LANES, LANES), 0)
    c = lax.broadcasted_iota(jnp.int32, (LANES, LANES), 1)
    sel_lo = jnp.where((r == c) & (r < ATTN_HD), 1.0, 0.0).astype(BF16)
    sel_hi = jnp.where((c == r + ATTN_HD) & (r < ATTN_HD), 1.0, 0.0).astype(BF16)
    ones_row = jnp.where(lax.broadcasted_iota(jnp.int32, (LANES, tm), 0) == ATTN_HD, 1.0, 0.0)

    def norm_rope(x, g, scale):
        ms = jnp.dot(x * x, mseg, precision=HIGHEST, preferred_element_type=F32)
        y = x * lax.rsqrt(ms + NORM_EPS) * g
        return (y * cos + _swap16(y, lane) * sin) * scale

    for j in range(ATTN_W // LANES):
        x = pa_ref[:, j * LANES:(j + 1) * LANES].astype(F32)
        y = norm_rope(x, qg_ref[...], q_scale).astype(BF16)
        qt_ref[2 * j] = _nt(sel_lo, y).astype(BF16)
        qt_ref[2 * j + 1] = _nt(sel_hi, y).astype(BF16)
    for j in range(KV_W // LANES):
        x = pa_ref[:, ATTN_W + j * LANES:ATTN_W + (j + 1) * LANES].astype(F32)
        y = norm_rope(x, kg_ref[...], 1.0)
        kp_ref[2 * j] = jnp.where(lane < ATTN_HD, y, 0.0).astype(BF16)
        kp_ref[2 * j + 1] = jnp.where(lane < ATTN_HD, pltpu.roll(y, ATTN_HD, axis=1), 0.0).astype(BF16)
        xv = pa_ref[:, ATTN_W + KV_W + j * LANES:ATTN_W + KV_W + (j + 1) * LANES]
        vt_ref[2 * j] = (_nt(sel_lo, xv) + ones_row).astype(BF16)
        vt_ref[2 * j + 1] = (_nt(sel_hi, xv) + ones_row).astype(BF16)


def _attn_prep(pa, cos_t, sin_t, qg, kg, mseg, *, tm):
    b, s, _ = pa.shape
    q_scale = (ATTN_HD ** -0.5) * math.log2(math.e)
    kern = functools.partial(_attn_prep_kernel, q_scale=q_scale)
    return pl.pallas_call(
        kern,
        out_shape=(
            jax.ShapeDtypeStruct((b, ATTN_HEADS, LANES, s), BF16),
            jax.ShapeDtypeStruct((b, KV_HEADS, s, LANES), BF16),
            jax.ShapeDtypeStruct((b, KV_HEADS, LANES, s), BF16),
        ),
        grid=(b, s // tm),
        in_specs=[
            pl.BlockSpec((None, tm, PA_W), lambda bi, i: (bi, i, 0)),
            pl.BlockSpec((tm, LANES), lambda bi, i: (i, 0)),
            pl.BlockSpec((tm, LANES), lambda bi, i: (i, 0)),
            pl.BlockSpec((1, LANES), lambda bi, i: (0, 0)),
            pl.BlockSpec((1, LANES), lambda bi, i: (0, 0)),
            pl.BlockSpec((LANES, LANES), lambda bi, i: (0, 0)),
        ],
        out_specs=(
            pl.BlockSpec((None, ATTN_HEADS, LANES, tm), lambda bi, i: (bi, 0, 0, i)),
            pl.BlockSpec((None, KV_HEADS, tm, LANES), lambda bi, i: (bi, 0, i, 0)),
            pl.BlockSpec((None, KV_HEADS, LANES, tm), lambda bi, i: (bi, 0, 0, i)),
        ),
        compiler_params=_cparams(("parallel", "parallel"), VMEM_LIMIT),
        name="attn_prep",
    )(pa, cos_t, sin_t, qg, kg, mseg)


def _attn_kernel(qt_ref, k_ref, vt_ref, o_ref, m_ref, acc_ref, s_ref, *, tk):
    g, _, tq = qt_ref.shape
    s = k_ref.shape[0]
    n = s // tk
    m_ref[...] = jnp.full(m_ref.shape, -jnp.inf, F32)
    acc_ref[...] = jnp.zeros(acc_ref.shape, F32)

    def scores(j, slot, h):
        t0 = pl.multiple_of(j * tk, tk)
        s_ref[slot, :, h * tq:(h + 1) * tq] = jnp.dot(k_ref[pl.ds(t0, tk), :], qt_ref[h],
                                                      preferred_element_type=F32)

    def softmax_pv(j, slot, h):
        t0 = pl.multiple_of(j * tk, tk)
        cols = slice(h * tq, (h + 1) * tq)
        sc = s_ref[slot, :, cols]
        m_prev = m_ref[:, cols]
        m_new = jnp.maximum(m_prev, jnp.max(sc, axis=0, keepdims=True))
        alpha = jnp.exp2(m_prev - m_new)
        p = jnp.exp2(sc - m_new).astype(BF16)
        pv = jnp.dot(vt_ref[:, pl.ds(t0, tk)], p, preferred_element_type=F32)
        acc_ref[:, cols] = alpha * acc_ref[:, cols] + pv
        m_ref[:, cols] = m_new

    def step(j, slot, prefetch):
        for h in range(g):
            softmax_pv(j, slot, h)
            if prefetch:
                scores(j + 1, 1 - slot, h)

    for h in range(g):
        scores(0, 0, h)

    def pair(i, carry):
        step(2 * i, 0, True)
        step(2 * i + 1, 1, True)
        return carry

    lax.fori_loop(0, n // 2 - 1, pair, 0)
    step(n - 2, 0, True)
    step(n - 1, 1, False)
    acc = acc_ref[...]
    o = (acc[0:ATTN_HD, :] / acc[ATTN_HD:ATTN_HD + 1, :]).astype(BF16)
    r = lax.broadcasted_iota(jnp.int32, (LANES, LANES), 0)
    c = lax.broadcasted_iota(jnp.int32, (LANES, LANES), 1)
    eye = jnp.where(r == c, 1.0, 0.0).astype(BF16)
    for pair in range(g // 2):
        rows = jnp.concatenate([o[:, (2 * pair) * tq:(2 * pair + 1) * tq],
                                o[:, (2 * pair + 1) * tq:(2 * pair + 2) * tq]], axis=0)
        out = lax.dot_general(rows, eye, (((0,), (0,)), ((), ())), preferred_element_type=F32)
        o_ref[:, pair * LANES:(pair + 1) * LANES] = out.astype(o_ref.dtype)


def _attention(qt, kp, vt, *, tq, tk):
    b, nh, _, s = qt.shape
    g = nh // KV_HEADS
    assert (s // tk) % 2 == 0 and g % 2 == 0
    kern = functools.partial(_attn_kernel, tk=tk)
    return pl.pallas_call(
        kern,
        out_shape=jax.ShapeDtypeStruct((b, s, ATTN_W), BF16),
        grid=(b, KV_HEADS, s // tq),
        in_specs=[
            pl.BlockSpec((None, g, LANES, tq), lambda bi, kv, i: (bi, kv, 0, i)),
            pl.BlockSpec((None, None, s, LANES), lambda bi, kv, i: (bi, kv, 0, 0)),
            pl.BlockSpec((None, None, LANES, s), lambda bi, kv, i: (bi, kv, 0, 0)),
        ],
        out_specs=pl.BlockSpec((None, tq, g * ATTN_HD), lambda bi, kv, i: (bi, i, kv)),
        scratch_shapes=[
            pltpu.VMEM((1, g * tq), F32),
            pltpu.VMEM((LANES, g * tq), F32),
            pltpu.VMEM((2, tk, g * tq), F32),
        ],
        compiler_params=_cparams(("parallel", "parallel", "parallel"), VMEM_LIMIT),
        name="attn",
    )(qt, kp, vt)


def _merge_kernel(x_ref, ym_ref, ya_ref, gg_ref, mod_ref, wbm_ref, wba_ref, wo_ref, g2_ref, wr_ref, br_ref,
                  xn_ref, h2_ref, lt_ref):
    d = x_ref.shape[-1]
    a = jnp.dot(ym_ref[...], wbm_ref[...], preferred_element_type=F32)
    bm = jnp.dot(ya_ref[...], wba_ref[...], preferred_element_type=F32)
    gm = jax.nn.sigmoid(gg_ref[:, 0:d].astype(F32))
    ga = jax.nn.sigmoid(gg_ref[:, d:2 * d].astype(F32))
    merged = (gm * a + ga * bm).astype(BF16)
    u = jnp.dot(merged, wo_ref[...], preferred_element_type=F32)
    xn = x_ref[...] + mod_ref[2:3, :] * u
    xn_ref[...] = xn
    h2 = _rms(xn, g2_ref[...]) * (1.0 + mod_ref[4:5, :]) + mod_ref[3:4, :]
    h2_ref[...] = h2
    lt = lax.dot_general(wr_ref[...], h2, (((1,), (1,)), ((), ())), precision=HIGHEST,
                         preferred_element_type=F32)
    lt_ref[...] = lt + br_ref[...]


def _merge(x, ym, ya, gg, mod_l, wbm, wba, wo, g2, wr_t, br_t, *, tm):
    b, s, d = x.shape
    ns = s // tm
    rr = wr_t.shape[0]
    full = lambda shp: pl.BlockSpec(shp, lambda bi, i: tuple(0 for _ in shp))
    return pl.pallas_call(
        _merge_kernel,
        out_shape=(
            jax.ShapeDtypeStruct((b, s, d), F32),
            jax.ShapeDtypeStruct((b * s, d), F32),
            jax.ShapeDtypeStruct((rr, b * s), F32),
        ),
        grid=(b, ns),
        in_specs=[
            pl.BlockSpec((None, tm, d), lambda bi, i: (bi, i, 0)),
            pl.BlockSpec((None, tm, MLSTM_W), lambda bi, i: (bi, i, 0)),
            pl.BlockSpec((None, tm, ATTN_W), lambda bi, i: (bi, i, 0)),
            pl.BlockSpec((None, tm, 2 * d), lambda bi, i: (bi, i, 0)),
            pl.BlockSpec((None, 6, d), lambda bi, i: (bi, 0, 0)),
            full((MLSTM_W, d)),
            full((ATTN_W, d)),
            full((d, d)),
            full((1, d)),
            full((rr, d)),
            full((rr, 1)),
        ],
        out_specs=(
            pl.BlockSpec((None, tm, d), lambda bi, i: (bi, i, 0)),
            pl.BlockSpec((tm, d), lambda bi, i: (bi * ns + i, 0)),
            pl.BlockSpec((rr, tm), lambda bi, i: (0, bi * ns + i)),
        ),
        compiler_params=_cparams(("parallel", "parallel"), VMEM_LIMIT),
        name="merge",
    )(x, ym, ya, gg, mod_l, wbm, wba, wo, g2, wr_t, br_t)


def _route_kernel(lt_ref, meta_ref, wts_ref, cnt_ref, carry_ref):
    tb = lt_ref.shape[1]
    epg = EXPERTS_PER_GROUP

    @pl.when(pl.program_id(0) == 0)
    def _():
        carry_ref[...] = jnp.zeros_like(carry_ref)

    row8 = lax.broadcasted_iota(jnp.int32, (8, tb), 0)
    gl = jnp.where(row8 < N_GROUPS, lt_ref[0:8, :], -jnp.inf)
    ge = jnp.exp(gl - jnp.max(gl, axis=0, keepdims=True))
    pg = ge / jnp.sum(ge, axis=0, keepdims=True)
    p_top = jnp.max(pg, axis=0, keepdims=True)
    g_idx = jnp.min(jnp.where(pg == p_top, row8, 8), axis=0, keepdims=True)

    el = jnp.zeros((epg, tb), F32)
    for g in range(N_GROUPS):
        el = jnp.where(g_idx == g, lt_ref[8 + g * epg:8 + (g + 1) * epg, :], el)
    ee = jnp.exp(el - jnp.max(el, axis=0, keepdims=True))
    pe = ee / jnp.sum(ee, axis=0, keepdims=True)
    v1 = jnp.max(pe, axis=0, keepdims=True)
    i1 = jnp.min(jnp.where(pe == v1, row8, 8), axis=0, keepdims=True)
    pe2 = jnp.where(row8 == i1, -1.0, pe)
    v2 = jnp.max(pe2, axis=0, keepdims=True)
    i2 = jnp.min(jnp.where(pe2 == v2, row8, 8), axis=0, keepdims=True)
    denom = v1 + v2
    w0 = v1 / denom * p_top
    w1 = v2 / denom * p_top
    e0 = g_idx * epg + i1
    e1 = g_idx * epg + i2

    rowe = lax.broadcasted_iota(jnp.int32, (N_EXPERTS, tb), 0)
    oh0 = rowe == e0
    oh1 = rowe == e1
    oh = jnp.where(oh0 | oh1, 1.0, 0.0)
    src = lax.broadcasted_iota(jnp.int32, (tb, tb), 0)
    dst = lax.broadcasted_iota(jnp.int32, (tb, tb), 1)
    strict = jnp.where(src < dst, 1.0, 0.0).astype(BF16)
    cum = jnp.dot(oh.astype(BF16), strict, preferred_element_type=F32)
    base = carry_ref[:, 0:1] + cum
    r0 = jnp.sum(jnp.where(oh0, base, 0.0), axis=0, keepdims=True)
    r1 = jnp.sum(jnp.where(oh1, base, 0.0), axis=0, keepdims=True)
    total = carry_ref[...] + jnp.sum(oh, axis=1, keepdims=True)
    carry_ref[...] = total
    cnt_ref[...] = total.astype(jnp.int32)

    meta = jnp.where(row8 == 0, e0, jnp.where(row8 == 1, e1, jnp.where(
        row8 == 2, r0.astype(jnp.int32), jnp.where(row8 == 3, r1.astype(jnp.int32), 0))))
    meta_ref[...] = meta
    wts_ref[...] = jnp.where(row8 == 0, w0, jnp.where(row8 == 1, w1, 0.0))


def _route(lt, *, tb):
    rr, t = lt.shape
    return pl.pallas_call(
        _route_kernel,
        out_shape=(
            jax.ShapeDtypeStruct((8, t), jnp.int32),
            jax.ShapeDtypeStruct((8, t), F32),
            jax.ShapeDtypeStruct((N_EXPERTS, LANES), jnp.int32),
        ),
        grid=(t // tb,),
        in_specs=[pl.BlockSpec((rr, tb), lambda i: (0, i))],
        out_specs=(
            pl.BlockSpec((8, tb), lambda i: (0, i)),
            pl.BlockSpec((8, tb), lambda i: (0, i)),
            pl.BlockSpec((N_EXPERTS, LANES), lambda i: (0, 0)),
        ),
        scratch_shapes=[pltpu.VMEM((N_EXPERTS, LANES), F32)],
        compiler_params=_cparams(("arbitrary",), VMEM_LIMIT),
        name="route",
    )(lt)


def _dispatch_kernel(dest_ref, h_ref, xg_ref, zero_ref, sem, zsem, *, n_tok, zrows):
    i = pl.program_id(0)
    tb = h_ref.shape[0]
    p_rows = xg_ref.shape[0]

    @pl.when(i == 0)
    def _():
        zero_ref[...] = jnp.zeros_like(zero_ref)
        for r0 in range(0, p_rows, zrows):
            pltpu.make_async_copy(zero_ref, xg_ref.at[pl.ds(r0, zrows)], zsem).start()
        for r0 in range(0, p_rows, zrows):
            pltpu.make_async_copy(zero_ref, xg_ref.at[pl.ds(r0, zrows)], zsem).wait()

    base = i * tb

    def issue(r, carry):
        src = h_ref.at[pl.ds(r, 1)]
        d0 = dest_ref[base + r]
        d1 = dest_ref[n_tok + base + r]
        pltpu.make_async_copy(src, xg_ref.at[pl.ds(d0, 1)], sem).start()
        pltpu.make_async_copy(src, xg_ref.at[pl.ds(d1, 1)], sem).start()
        return carry

    lax.fori_loop(0, tb, issue, 0)

    def drain(r, carry):
        pltpu.make_async_copy(h_ref.at[pl.ds(0, 1)], xg_ref.at[pl.ds(0, 1)], sem).wait()
        pltpu.make_async_copy(h_ref.at[pl.ds(0, 1)], xg_ref.at[pl.ds(0, 1)], sem).wait()
        return carry

    lax.fori_loop(0, tb, drain, 0)


def _dispatch(dest, h2, p_rows, *, tb, zrows):
    t, d = h2.shape
    kern = functools.partial(_dispatch_kernel, n_tok=t, zrows=zrows)
    return pl.pallas_call(
        kern,
        out_shape=jax.ShapeDtypeStruct((p_rows, d), F32),
        grid_spec=pltpu.PrefetchScalarGridSpec(
            num_scalar_prefetch=1,
            grid=(t // tb,),
            in_specs=[pl.BlockSpec((tb, d), lambda i, dest: (i, 0))],
            out_specs=pl.BlockSpec(memory_space=pl.ANY),
            scratch_shapes=[
                pltpu.VMEM((zrows, d), F32),
                pltpu.SemaphoreType.DMA(()),
                pltpu.SemaphoreType.DMA(()),
            ],
        ),
        compiler_params=_cparams(("arbitrary",), VMEM_LIMIT),
        name="dispatch",
    )(dest, h2)


def _experts_kernel(te_ref, first_ref, used_ref, x_ref, wg_ref, wu_ref, wd_ref, y_ref, wgb, wub, wdb):
    i = pl.program_id(0)

    @pl.when(first_ref[i] == 1)
    def _():
        wgb[...] = wg_ref[...].astype(BF16)
        wub[...] = wu_ref[...].astype(BF16)
        wdb[...] = wd_ref[...].astype(BF16)

    @pl.when(used_ref[i] == 1)
    def _():
        x = x_ref[...].astype(BF16)
        a = jnp.dot(x, wgb[...], preferred_element_type=F32)
        u = jnp.dot(x, wub[...], preferred_element_type=F32)
        act = (a * jax.nn.sigmoid(a) * u).astype(BF16)
        y_ref[...] = jnp.dot(act, wdb[...], preferred_element_type=F32)

    @pl.when(used_ref[i] == 0)
    def _():
        y_ref[...] = jnp.zeros_like(y_ref)


def _experts(tile_expert, tile_first, tile_used, xg, w_gate, w_up, w_down, *, layer, tm):
    p_rows, d = xg.shape
    f = w_gate.shape[-1]
    nt = p_rows // tm
    wmap = lambda i, te, fi, us: (layer, te[i], 0, 0)
    return pl.pallas_call(
        _experts_kernel,
        out_shape=jax.ShapeDtypeStruct((p_rows, d), F32),
        grid_spec=pltpu.PrefetchScalarGridSpec(
            num_scalar_prefetch=3,
            grid=(nt,),
            in_specs=[
                pl.BlockSpec((tm, d), lambda i, te, fi, us: (i, 0)),
                pl.BlockSpec((None, None, d, f), wmap),
                pl.BlockSpec((None, None, d, f), wmap),
                pl.BlockSpec((None, None, f, d), wmap),
            ],
            out_specs=pl.BlockSpec((tm, d), lambda i, te, fi, us: (i, 0)),
            scratch_shapes=[
                pltpu.VMEM((d, f), BF16),
                pltpu.VMEM((d, f), BF16),
                pltpu.VMEM((f, d), BF16),
            ],
        ),
        compiler_params=_cparams(("arbitrary",), VMEM_LIMIT),
        name="experts",
    )(tile_expert, tile_first, tile_used, xg, w_gate, w_up, w_down)


def _combine_kernel(dest_ref, x_ref, w_ref, mod_ref, fg_ref, y_ref, o_ref, buf_ref, sem, *, n_tok, final):
    i = pl.program_id(0) * pl.num_programs(1) + pl.program_id(1)
    tb = x_ref.shape[0]
    base = i * tb

    def issue(r, carry):
        d0 = dest_ref[base + r]
        d1 = dest_ref[n_tok + base + r]
        pltpu.make_async_copy(y_ref.at[pl.ds(d0, 1)], buf_ref.at[0, pl.ds(r, 1)], sem).start()
        pltpu.make_async_copy(y_ref.at[pl.ds(d1, 1)], buf_ref.at[1, pl.ds(r, 1)], sem).start()
        return carry

    lax.fori_loop(0, tb, issue, 0)

    def drain(r, carry):
        pltpu.make_async_copy(y_ref.at[pl.ds(0, 1)], buf_ref.at[0, pl.ds(0, 1)], sem).wait()
        pltpu.make_async_copy(y_ref.at[pl.ds(0, 1)], buf_ref.at[1, pl.ds(0, 1)], sem).wait()
        return carry

    lax.fori_loop(0, tb, drain, 0)
    moe = w_ref[:, 0:1] * buf_ref[0] + w_ref[:, 1:2] * buf_ref[1]
    xo = x_ref[...] + mod_ref[5:6, :] * moe
    if final:
        xo = _rms(xo, fg_ref[...])
    o_ref[...] = xo


def _combine(dest, xn, wts_col, mod_l, fg, y, *, tb, final):
    b, s, d = xn.shape
    ns = s // tb
    kern = functools.partial(_combine_kernel, n_tok=b * s, final=final)
    return pl.pallas_call(
        kern,
        out_shape=jax.ShapeDtypeStruct((b, s, d), F32),
        grid_spec=pltpu.PrefetchScalarGridSpec(
            num_scalar_prefetch=1,
            grid=(b, ns),
            in_specs=[
                pl.BlockSpec((None, tb, d), lambda bi, i, dest: (bi, i, 0)),
                pl.BlockSpec((tb, 8), lambda bi, i, dest: (bi * ns + i, 0)),
                pl.BlockSpec((None, 6, d), lambda bi, i, dest: (bi, 0, 0)),
                pl.BlockSpec((1, d), lambda bi, i, dest: (0, 0)),
                pl.BlockSpec(memory_space=pl.ANY),
            ],
            out_specs=pl.BlockSpec((None, tb, d), lambda bi, i, dest: (bi, i, 0)),
            scratch_shapes=[
                pltpu.VMEM((2, tb, d), F32),
                pltpu.SemaphoreType.DMA(()),
            ],
        ),
        compiler_params=_cparams(("arbitrary", "arbitrary"), VMEM_LIMIT),
        name="combine",
    )(dest, xn, wts_col, mod_l, fg, y)


def _rope_tables(seq):
    rows = seq // GRID_W
    row = jnp.repeat(jnp.arange(rows, dtype=F32), GRID_W)
    col = jnp.tile(jnp.arange(GRID_W, dtype=F32), rows)
    half = ATTN_HD // 2
    inv_freq = ROPE_BASE ** (-jnp.arange(0, half, 2, dtype=F32) / half)
    ang_r = row[:, None] * inv_freq
    ang_c = col[:, None] * inv_freq
    cos64 = jnp.concatenate([jnp.cos(ang_r), jnp.cos(ang_r), jnp.cos(ang_c), jnp.cos(ang_c)], axis=-1)
    sin64 = jnp.concatenate([-jnp.sin(ang_r), jnp.sin(ang_r), -jnp.sin(ang_c), jnp.sin(ang_c)], axis=-1)
    return jnp.tile(cos64, (1, LANES // ATTN_HD)), jnp.tile(sin64, (1, LANES // ATTN_HD))


def _plan_tiles(counts, tm, nt):
    tiles = (counts + tm - 1) // tm
    tile_end = jnp.cumsum(tiles)
    tile_start = tile_end - tiles
    offsets = (tile_start * tm).astype(jnp.int32)
    tidx = jnp.arange(nt, dtype=jnp.int32)
    te = jnp.sum((tidx[:, None] >= tile_end[None, :]).astype(jnp.int32), axis=1)
    used = (tidx < tile_end[-1]).astype(jnp.int32)
    last_used = jnp.sum((tile_end[-1] - 1 >= tile_end).astype(jnp.int32))
    te = jnp.where(used == 1, te, last_used).astype(jnp.int32)
    prev = jnp.concatenate([jnp.full((1,), -1, jnp.int32), te[:-1]])
    first = ((te != prev) & (used == 1)).astype(jnp.int32)
    return offsets, te, first, used


def kernel(x, c, w_ada, b_ada, norm1_g, w_in, b_in, conv_w, conv_b, mlstm_norm_g, q_norm_g, k_norm_g,
           w_branch_m, w_branch_a, w_out, norm2_g, w_router_group, b_router_group, w_router_expert,
           b_router_expert, w_gate, w_up, w_down, final_norm_g):
    b, s, d = x.shape
    depth = w_ada.shape[0]
    t = b * s
    nh = MLSTM_HEADS
    tm = 512 if s % 512 == 0 else 128
    tq = 256 if s % 256 == 0 else 128
    tk = 512 if s % 2048 == 0 else 128
    tme = 256
    tbr = 512 if t % 512 == 0 else 128
    tbd = 256 if s % 256 == 0 else 128
    nt = (2 * t) // tme + N_EXPERTS
    p_rows = nt * tme
    zrows = tme

    mod = _adaln(c, w_ada, b_ada).reshape(depth, b, 6, d)
    cos_t, sin_t = _rope_tables(s)
    seg = np.arange(LANES) // ATTN_HD
    mseg = jnp.asarray((seg[:, None] == seg[None, :]).astype(np.float32) / ATTN_HD)

    o_mq, o_mk, o_mv, o_mo = 0, MLSTM_W, 2 * MLSTM_W, 3 * MLSTM_W
    o_g = 4 * MLSTM_W
    o_aq = o_g + 4 * nh
    o_ak = o_aq + ATTN_W
    o_av = o_ak + KV_W
    o_gm = o_av + KV_W
    o_ga = o_gm + d

    for l in range(depth):
        w_l = w_in[l]
        w_main = jnp.concatenate([w_l[:, :o_g], w_l[:, o_aq:]], axis=1).astype(BF16)
        b_main = jnp.concatenate([b_in[l][:o_g], b_in[l][o_aq:]])[None, :]
        wg = jnp.transpose(w_l[:, o_g:o_aq].reshape(d, 4, nh), (2, 1, 0))
        wgt = jnp.concatenate([wg, jnp.zeros((nh, 4, d), F32)], axis=1).reshape(nh * 8, d).astype(BF16)
        bg = jnp.transpose(b_in[l][o_g:o_aq].reshape(4, nh))
        bgt = jnp.concatenate([bg, jnp.zeros((nh, 4), F32)], axis=1).reshape(nh * 8, 1)

        pm, pa, gg, gt = _in_proj(x, mod[l], norm1_g[l][None, :], w_main, b_main, wgt, bgt, tm=tm)
        ym = _mlstm(pm, gt, conv_w[l].reshape(CONV_W, 2 * MLSTM_W), conv_b[l][None, :],
                    mlstm_norm_g[l][None, :])
        qg = jnp.tile(q_norm_g[l], LANES // ATTN_HD)[None, :]
        kg = jnp.tile(k_norm_g[l], LANES // ATTN_HD)[None, :]
        qp, kp, vp = _attn_prep(pa, cos_t, sin_t, qg, kg, mseg, tm=tm)
        ya = _attention(qp, kp, vp, tq=tq, tk=tk)

        wr = jnp.concatenate([w_router_group[l], jnp.zeros((d, 8 - N_GROUPS), F32), w_router_expert[l]], axis=1)
        br = jnp.concatenate([b_router_group[l], jnp.zeros((8 - N_GROUPS,), F32), b_router_expert[l]])
        xn, h2, lt = _merge(x, ym, ya, gg, mod[l], w_branch_m[l].astype(BF16), w_branch_a[l].astype(BF16),
                            w_out[l].astype(BF16), norm2_g[l][None, :], jnp.transpose(wr), br[:, None], tm=tm)

        meta, wts, cnt = _route(lt, tb=tbr)
        offsets, te, first, used = _plan_tiles(cnt[:, 0], tme, nt)
        dest = (offsets[meta[0:2]] + meta[2:4]).reshape(-1)
        xg = _dispatch(dest, h2, p_rows, tb=tbd, zrows=zrows)
        y = _experts(te, first, used, xg, w_gate, w_up, w_down, layer=l, tm=tme)
        x = _combine(dest, xn, jnp.transpose(wts), mod[l], final_norm_g[None, :], y,
                     tb=tbd, final=(l == depth - 1))
    return x
```

```python
import functools
import math

import jax
import jax.numpy as jnp
import numpy as np
from jax import lax
from jax.experimental import pallas as pl
from jax.experimental.pallas import tpu as pltpu

F32 = jnp.float32
BF16 = jnp.bfloat16
HIGHEST = lax.Precision.HIGHEST

GRID_W = 64
MLSTM_HEADS = 4
MLSTM_HD = 128
MLSTM_W = MLSTM_HEADS * MLSTM_HD
CONV_W = 5
ATTN_HEADS = 8
KV_HEADS = 2
ATTN_HD = 64
ATTN_W = ATTN_HEADS * ATTN_HD
KV_W = KV_HEADS * ATTN_HD
ROPE_BASE = 10000.0
N_GROUPS = 4
EXPERTS_PER_GROUP = 8
N_EXPERTS = N_GROUPS * EXPERTS_PER_GROUP
NORM_EPS = 1e-6

LANES = 128
MLSTM_CHUNK = 128
SEG_ALIGN = 8
VMEM_LIMIT = 56 * 1024 * 1024

PM_W = 4 * MLSTM_W
PA_W = ATTN_W + 2 * KV_W
GG_W = None
ROUTER_ROWS = 8 + N_EXPERTS


def _cparams(sem, vmem=None):
    return pltpu.CompilerParams(dimension_semantics=sem, vmem_limit_bytes=vmem)


def _rms(x, g):
    ms = jnp.mean(x * x, axis=-1, keepdims=True)
    return x * lax.rsqrt(ms + NORM_EPS) * g


def _adaln_kernel(c_ref, w_ref, b_ref, o_ref):
    c = c_ref[...]
    cond = c * jax.nn.sigmoid(c)
    o_ref[0] = jnp.dot(cond, w_ref[0], precision=HIGHEST, preferred_element_type=F32) + b_ref[0]


def _adaln(c, w_ada, b_ada):
    depth, d, n = w_ada.shape
    b = c.shape[0]
    tn = 1536 if n % 1536 == 0 else n
    return pl.pallas_call(
        _adaln_kernel,
        out_shape=jax.ShapeDtypeStruct((depth, b, n), F32),
        grid=(depth, n // tn),
        in_specs=[
            pl.BlockSpec((b, d), lambda l, j: (0, 0)),
            pl.BlockSpec((1, d, tn), lambda l, j: (l, 0, j)),
            pl.BlockSpec((1, 1, tn), lambda l, j: (l, 0, j)),
        ],
        out_specs=pl.BlockSpec((1, b, tn), lambda l, j: (l, 0, j)),
        compiler_params=_cparams(("parallel", "parallel"), VMEM_LIMIT),
        name="adaln",
    )(c, w_ada, b_ada.reshape(depth, 1, n))


def _in_proj_kernel(x_ref, mod_ref, g_ref, w_ref, b_ref, wgt_ref, bgt_ref,
                    pm_ref, pa_ref, gg_ref, gt_ref, *, col_chunk):
    x = x_ref[...]
    y = _rms(x, g_ref[...])
    h = y * (1.0 + mod_ref[1:2, :]) + mod_ref[0:1, :]
    hb = h.astype(BF16)
    c0 = 0
    for o_ref in (pm_ref, pa_ref, gg_ref):
        width = o_ref.shape[-1]
        for a in range(0, width, col_chunk):
            e = min(a + col_chunk, width)
            acc = jnp.dot(hb, w_ref[:, c0 + a:c0 + e], preferred_element_type=F32)
            o_ref[:, a:e] = (acc + b_ref[:, c0 + a:c0 + e]).astype(o_ref.dtype)
        c0 += width
    gt = lax.dot_general(wgt_ref[...], hb, (((1,), (1,)), ((), ())), preferred_element_type=F32)
    gt_ref[...] = gt + bgt_ref[...]


def _in_proj(x, mod_l, g1, w_main, b_main, wgt, bgt, *, tm):
    b, s, d = x.shape
    nw = w_main.shape[1]
    gg_w = nw - PM_W - PA_W
    grows = wgt.shape[0]
    kern = functools.partial(_in_proj_kernel, col_chunk=512)
    return pl.pallas_call(
        kern,
        out_shape=(
            jax.ShapeDtypeStruct((b, s, PM_W), BF16),
            jax.ShapeDtypeStruct((b, s, PA_W), BF16),
            jax.ShapeDtypeStruct((b, s, gg_w), BF16),
            jax.ShapeDtypeStruct((b, grows, s), F32),
        ),
        grid=(b, s // tm),
        in_specs=[
            pl.BlockSpec((None, tm, d), lambda bi, i: (bi, i, 0)),
            pl.BlockSpec((None, 6, d), lambda bi, i: (bi, 0, 0)),
            pl.BlockSpec((1, d), lambda bi, i: (0, 0)),
            pl.BlockSpec((d, nw), lambda bi, i: (0, 0)),
            pl.BlockSpec((1, nw), lambda bi, i: (0, 0)),
            pl.BlockSpec((grows, d), lambda bi, i: (0, 0)),
            pl.BlockSpec((grows, 1), lambda bi, i: (0, 0)),
        ],
        out_specs=(
            pl.BlockSpec((None, tm, PM_W), lambda bi, i: (bi, i, 0)),
            pl.BlockSpec((None, tm, PA_W), lambda bi, i: (bi, i, 0)),
            pl.BlockSpec((None, tm, gg_w), lambda bi, i: (bi, i, 0)),
            pl.BlockSpec((None, grows, tm), lambda bi, i: (bi, 0, i)),
        ),
        compiler_params=_cparams(("parallel", "parallel"), VMEM_LIMIT),
        name="in_proj",
    )(x, mod_l, g1, w_main, b_main, wgt, bgt)


def _mlstm_kernel(q_ref, k_ref, v_ref, o_ref, gt_ref, cwq_ref, cwk_ref, cbq_ref, cbk_ref, ng_ref,
                  y_ref, xs_ref, qc_ref, kc_ref, tab_ref, hs_ref, stf_ref, stb_ref, mf_ref, mb_ref,
                  *, seq, blk):
    L = MLSTM_CHUNK
    nc = seq // L
    half = nc // 2
    pad = 8

    zeros_pad = jnp.zeros((pad, LANES), F32)
    xs_ref[0:pad, :] = zeros_pad
    xs_ref[pad + seq:pad + seq + pad, :] = zeros_pad

    def conv_pass(src_ref, w_ref, b_ref, dst_ref, scale):
        for r0 in range(0, seq, blk):
            xs_ref[pad + r0:pad + r0 + blk, :] = src_ref[r0:r0 + blk, :].astype(F32)
        for r0 in range(0, seq, blk):
            acc = jnp.zeros((blk, LANES), F32) + b_ref[...]
            for j in range(CONV_W):
                off = pad + r0 + j - CONV_W // 2
                acc = acc + w_ref[j:j + 1, :] * xs_ref[off:off + blk, :]
            yv = acc * jax.nn.sigmoid(acc) * scale
            dst_ref[r0:r0 + blk, :] = yv.astype(BF16)

    conv_pass(q_ref, cwq_ref, cbq_ref, qc_ref, MLSTM_HD ** -0.5)
    conv_pass(k_ref, cwk_ref, cbk_ref, kc_ref, 1.0)

    jj = lax.broadcasted_iota(jnp.int32, (L, L), 0)
    ll = lax.broadcasted_iota(jnp.int32, (L, L), 1)
    tri = jnp.concatenate([jnp.where(jj <= ll, 1.0, 0.0), jnp.where(jj >= ll, 1.0, 0.0)], axis=1).astype(BF16)

    def split3(x):
        hi = x.astype(BF16)
        r1 = x - hi.astype(F32)
        mid = r1.astype(BF16)
        lo = (r1 - mid.astype(F32)).astype(BF16)
        return jnp.concatenate([hi, mid, lo], axis=0)

    def gate_body(c, carry):
        t0 = pl.multiple_of(c * L, L)
        g = gt_ref[:, pl.ds(t0, L)]
        lf = jax.nn.log_sigmoid(g)
        c3 = jnp.dot(split3(lf), tri, preferred_element_type=F32)
        c2 = c3[0:8] + c3[8:16] + c3[16:24]
        cf, cb = c2[:, 0:L], c2[:, L:2 * L]
        row = lax.broadcasted_iota(jnp.int32, (8, L), 0)
        tab = jnp.where(row == 0, g, jnp.where(row == 1, cf, jnp.where(row == 2, g, cb)))
        tab = jnp.where(row < 4, tab, 0.0)
        tab_ref[:, pl.ds(t0, L)] = tab
        return carry

    lax.fori_loop(0, nc, gate_body, 0)

    srow = lax.broadcasted_iota(jnp.int32, (24, 4 * L), 0) % 8
    scol = lax.broadcasted_iota(jnp.int32, (24, 4 * L), 1) // L
    sel3 = jnp.where(srow == scol, 1.0, 0.0).astype(BF16)

    stf_ref[...] = jnp.zeros_like(stf_ref)
    stb_ref[...] = jnp.zeros_like(stb_ref)
    mf_ref[...] = jnp.zeros_like(mf_ref)
    mb_ref[...] = jnp.zeros_like(mb_ref)
    ones_blk = jnp.ones((L, L), BF16)

    def chain(t0, st_ref, m_ref, fwd):
        q = qc_ref[pl.ds(t0, L), :]
        k = kc_ref[pl.ds(t0, L), :]
        v = v_ref[pl.ds(t0, L), :]
        r8 = tab_ref[:, pl.ds(t0, L)]
        cols = lax.dot_general(split3(r8), sel3, (((0,), (0,)), ((), ())),
                               preferred_element_type=F32)
        if fwd:
            li_c, b_c = cols[:, 0:L], cols[:, L:2 * L]
            li_r, b_r = r8[0:1, :], r8[1:2, :]
            bl = b_c[L - 1:L, :]
            mask = ll <= jj
        else:
            li_c, b_c = cols[:, 2 * L:3 * L], cols[:, 3 * L:4 * L]
            li_r, b_r = r8[2:3, :], r8[3:4, :]
            bl = b_c[0:1, :]
            mask = ll >= jj
        m_prev = m_ref[...]
        st = st_ref[...]
        a_c = bl - b_c + li_c
        a_max = jnp.max(a_c, axis=0, keepdims=True)
        w_c = jnp.exp(a_c - a_max)
        vw = (v.astype(F32) * w_c).astype(BF16)
        cat = jnp.concatenate([vw, w_c.astype(BF16)], axis=1)
        chunk2 = lax.dot_general(k, cat, (((0,), (0,)), ((), ())), preferred_element_type=F32)
        d = jnp.where(mask, b_c + (li_r - b_r), -jnp.inf)
        dmax = jnp.max(d, axis=1, keepdims=True)
        m_inter = b_c + m_prev
        m_out = jnp.maximum(m_inter, dmax)
        dw = jnp.exp(d - m_out)
        inter_w = jnp.exp(m_inter - m_out)
        s = lax.dot_general(q, k, (((1,), (1,)), ((), ())), preferred_element_type=F32) * dw
        vcat = jnp.concatenate([v, ones_blk], axis=1)
        r2 = jnp.dot(s.astype(BF16), vcat, preferred_element_type=F32)
        q2 = jnp.dot(q, st.astype(BF16), preferred_element_type=F32)
        num = r2[:, 0:L] + inter_w * q2[:, 0:L]
        den = r2[:, L:2 * L] + inter_w * q2[:, L:2 * L]
        hout = num / jnp.maximum(jnp.abs(den), jnp.exp(-m_out))
        m_new = jnp.maximum(bl + m_prev, a_max)
        decay = jnp.exp(bl + m_prev - m_new)
        inject = jnp.exp(a_max - m_new)
        st_ref[...] = decay[:, 0:1] * st + inject[:, 0:1] * chunk2
        m_ref[...] = m_new
        return hout

    def finalize(t0, h):
        yv = _rms(h, ng_ref[...]) * jax.nn.sigmoid(o_ref[pl.ds(t0, L), :].astype(F32))
        y_ref[pl.ds(t0, L), :] = yv.astype(y_ref.dtype)

    unroll = 4 if half % 4 == 0 else 1

    def first_half(i, carry):
        for u in range(unroll):
            c = i * unroll + u
            tf = pl.multiple_of(c * L, L)
            tb = pl.multiple_of((nc - 1 - c) * L, L)
            hs_ref[pl.ds(tf, L), :] = chain(tf, stf_ref, mf_ref, True)
            hs_ref[pl.ds(tb, L), :] = chain(tb, stb_ref, mb_ref, False)
        return carry

    def second_half(i, carry):
        for u in range(unroll):
            c = half + i * unroll + u
            tf = pl.multiple_of(c * L, L)
            tb = pl.multiple_of((nc - 1 - c) * L, L)
            finalize(tf, hs_ref[pl.ds(tf, L), :] + chain(tf, stf_ref, mf_ref, True))
            finalize(tb, hs_ref[pl.ds(tb, L), :] + chain(tb, stb_ref, mb_ref, False))
        return carry

    lax.fori_loop(0, half // unroll, first_half, 0)
    lax.fori_loop(0, half // unroll, second_half, 0)


def _mlstm(pm, gt, conv_w, conv_b, norm_g):
    b, s, _ = pm.shape
    nh = MLSTM_HEADS
    assert s % (2 * MLSTM_CHUNK) == 0
    blk = 512 if s % 512 == 0 else MLSTM_CHUNK
    kern = functools.partial(_mlstm_kernel, seq=s, blk=blk)
    L = MLSTM_CHUNK
    col = lambda off: (lambda bi, h: (bi, 0, off + h))
    return pl.pallas_call(
        kern,
        out_shape=jax.ShapeDtypeStruct((b, s, MLSTM_W), BF16),
        grid=(b, nh),
        in_specs=[
            pl.BlockSpec((None, s, LANES), col(0)),
            pl.BlockSpec((None, s, LANES), col(nh)),
            pl.BlockSpec((None, s, LANES), col(2 * nh)),
            pl.BlockSpec((None, s, LANES), col(3 * nh)),
            pl.BlockSpec((None, 8, s), lambda bi, h: (bi, h, 0)),
            pl.BlockSpec((CONV_W, LANES), lambda bi, h: (0, h)),
            pl.BlockSpec((CONV_W, LANES), lambda bi, h: (0, nh + h)),
            pl.BlockSpec((1, LANES), lambda bi, h: (0, h)),
            pl.BlockSpec((1, LANES), lambda bi, h: (0, nh + h)),
            pl.BlockSpec((1, LANES), lambda bi, h: (0, h)),
        ],
        out_specs=pl.BlockSpec((None, s, LANES), lambda bi, h: (bi, 0, h)),
        scratch_shapes=[
            pltpu.VMEM((s + 16, LANES), F32),
            pltpu.VMEM((s, LANES), BF16),
            pltpu.VMEM((s, LANES), BF16),
            pltpu.VMEM((8, s), F32),
            pltpu.VMEM((s, LANES), F32),
            pltpu.VMEM((L, 2 * L), F32),
            pltpu.VMEM((L, 2 * L), F32),
            pltpu.VMEM((1, L), F32),
            pltpu.VMEM((1, L), F32),
        ],
        compiler_params=_cparams(("parallel", "parallel"), VMEM_LIMIT),
        name="mlstm",
    )(pm, pm, pm, pm, gt, conv_w, conv_w, conv_b, conv_b, norm_g)


def _swap16(y, lane):
    fwd = pltpu.roll(y, LANES - 16, axis=1)
    bwd = pltpu.roll(y, 16, axis=1)
    return jnp.where((lane % 32) < 16, fwd, bwd)


def _nt(a, b):
    return lax.dot_general(a, b, (((1,), (1,)), ((), ())), preferred_element_type=F32)


def _attn_prep_kernel(pa_ref, cos_ref, sin_ref, qg_ref, kg_ref, mseg_ref, qt_ref, kp_ref, vt_ref, *, q_scale):
    tm = pa_ref.shape[0]
    lane = lax.broadcasted_iota(jnp.int32, (tm, LANES), 1)
    cos = cos_ref[...]
    sin = sin_ref[...]
    mseg = mseg_ref[...]
    r = lax.broadcasted_iota(jnp.int32, (LANES, LANES), 0)
    c = lax.broadcasted_iota(jnp.int32, (LANES, LANES), 1)
    sel_lo = jnp.where((r == c) & (r < ATTN_HD), 1.0, 0.0).astype(BF16)
    sel_hi = jnp.where((c == r + ATTN_HD) & (r < ATTN_HD), 1.0, 0.0).astype(BF16)
    ones_row = jnp.where(lax.broadcasted_iota(jnp.int32, (LANES, tm), 0) == ATTN_HD, 1.0, 0.0)

    def norm_rope(x, g, scale):
        ms = jnp.dot(x * x, mseg, precision=HIGHEST, preferred_element_type=F32)
        y = x * lax.rsqrt(ms + NORM_EPS) * g
        return (y * cos + _swap16(y, lane) * sin) * scale

    for j in range(ATTN_W // LANES):
        x = pa_ref[:, j * LANES:(j + 1) * LANES].astype(F32)
        y = norm_rope(x, qg_ref[...], q_scale).astype(BF16)
        qt_ref[2 * j] = _nt(sel_lo, y).astype(BF16)
        qt_ref[2 * j + 1] = _nt(sel_hi, y).astype(BF16)
    for j in range(KV_W // LANES):
        x = pa_ref[:, ATTN_W + j * LANES:ATTN_W + (j + 1) * LANES].astype(F32)
        y = norm_rope(x, kg_ref[...], 1.0)
        kp_ref[2 * j] = jnp.where(lane < ATTN_HD, y, 0.0).astype(BF16)
        kp_ref[2 * j + 1] = jnp.where(lane < ATTN_HD, pltpu.roll(y, ATTN_HD, axis=1), 0.0).astype(BF16)
        xv = pa_ref[:, ATTN_W + KV_W + j * LANES:ATTN_W + KV_W + (j + 1) * LANES]
        vt_ref[2 * j] = (_nt(sel_lo, xv) + ones_row).astype(BF16)
        vt_ref[2 * j + 1] = (_nt(sel_hi, xv) + ones_row).astype(BF16)


def _attn_prep(pa, cos_t, sin_t, qg, kg, mseg, *, tm):
    b, s, _ = pa.shape
    q_scale = (ATTN_HD ** -0.5) * math.log2(math.e)
    kern = functools.partial(_attn_prep_kernel, q_scale=q_scale)
    return pl.pallas_call(
        kern,
        out_shape=(
            jax.ShapeDtypeStruct((b, ATTN_HEADS, LANES, s), BF16),
            jax.ShapeDtypeStruct((b, KV_HEADS, s, LANES), BF16),
            jax.ShapeDtypeStruct((b, KV_HEADS, LANES, s), BF16),
        ),
        grid=(b, s // tm),
        in_specs=[
            pl.BlockSpec((None, tm, PA_W), lambda bi, i: (bi, i, 0)),
            pl.BlockSpec((tm, LANES), lambda bi, i: (i, 0)),
            pl.BlockSpec((tm, LANES), lambda bi, i: (i, 0)),
            pl.BlockSpec((1, LANES), lambda bi, i: (0, 0)),
            pl.BlockSpec((1, LANES), lambda bi, i: (0, 0)),
            pl.BlockSpec((LANES, LANES), lambda bi, i: (0, 0)),
        ],
        out_specs=(
            pl.BlockSpec((None, ATTN_HEADS, LANES, tm), lambda bi, i: (bi, 0, 0, i)),
            pl.BlockSpec((None, KV_HEADS, tm, LANES), lambda bi, i: (bi, 0, i, 0)),
            pl.BlockSpec((None, KV_HEADS, LANES, tm), lambda bi, i: (bi, 0, 0, i)),
        ),
        compiler_params=_cparams(("parallel", "parallel"), VMEM_LIMIT),
        name="attn_prep",
    )(pa, cos_t, sin_t, qg, kg, mseg)


def _attn_kernel(qt_ref, k_ref, vt_ref, o_ref, m_ref, acc_ref, s_ref, *, tk):
    g, _, tq = qt_ref.shape
    s = k_ref.shape[0]
    n = s // tk
    m_ref[...] = jnp.full(m_ref.shape, -jnp.inf, F32)
    acc_ref[...] = jnp.zeros(acc_ref.shape, F32)

    def scores(j, slot, h):
        t0 = pl.multiple_of(j * tk, tk)
        s_ref[slot, :, h * tq:(h + 1) * tq] = jnp.dot(k_ref[pl.ds(t0, tk), :], qt_ref[h],
                                                      preferred_element_type=F32)

    def softmax_pv(j, slot, h):
        t0 = pl.multiple_of(j * tk, tk)
        cols = slice(h * tq, (h + 1) * tq)
        sc = s_ref[slot, :, cols]
        m_prev = m_ref[:, cols]
        m_new = jnp.maximum(m_prev, jnp.max(sc, axis=0, keepdims=True))
        alpha = jnp.exp2(m_prev - m_new)
        p = jnp.exp2(sc - m_new).astype(BF16)
        pv = jnp.dot(vt_ref[:, pl.ds(t0, tk)], p, preferred_element_type=F32)
        acc_ref[:, cols] = alpha * acc_ref[:, cols] + pv
        m_ref[:, cols] = m_new

    def step(j, slot, prefetch):
        for h in range(g):
            softmax_pv(j, slot, h)
            if prefetch:
                scores(j + 1, 1 - slot, h)

    for h in range(g):
        scores(0, 0, h)

    def pair(i, carry):
        step(2 * i, 0, True)
        step(2 * i + 1, 1, True)
        return carry

    lax.fori_loop(0, n // 2 - 1, pair, 0)
    step(n - 2, 0, True)
    step(n - 1, 1, False)
    acc = acc_ref[...]
    o = (acc[0:ATTN_HD, :] / acc[ATTN_HD:ATTN_HD + 1, :]).astype(BF16)
    r = lax.broadcasted_iota(jnp.int32, (LANES, LANES), 0)
    c = lax.broadcasted_iota(jnp.int32, (LANES, LANES), 1)
    eye = jnp.where(r == c, 1.0, 0.0).astype(BF16)
    for pair in range(g // 2):
        rows = jnp.concatenate([o[:, (2 * pair) * tq:(2 * pair + 1) * tq],
                                o[:, (2 * pair + 1) * tq:(2 * pair + 2) * tq]], axis=0)
        out = lax.dot_general(rows, eye, (((0,), (0,)), ((), ())), preferred_element_type=F32)
        o_ref[:, pair * LANES:(pair + 1) * LANES] = out.astype(o_ref.dtype)


def _attention(qt, kp, vt, *, tq, tk):
    b, nh, _, s = qt.shape
    g = nh // KV_HEADS
    assert (s // tk) % 2 == 0 and g % 2 == 0
    kern = functools.partial(_attn_kernel, tk=tk)
    return pl.pallas_call(
        kern,
        out_shape=jax.ShapeDtypeStruct((b, s, ATTN_W), BF16),
        grid=(b, KV_HEADS, s // tq),
        in_specs=[
            pl.BlockSpec((None, g, LANES, tq), lambda bi, kv, i: (bi, kv, 0, i)),
            pl.BlockSpec((None, None, s, LANES), lambda bi, kv, i: (bi, kv, 0, 0)),
            pl.BlockSpec((None, None, LANES, s), lambda bi, kv, i: (bi, kv, 0, 0)),
        ],
        out_specs=pl.BlockSpec((None, tq, g * ATTN_HD), lambda bi, kv, i: (bi, i, kv)),
        scratch_shapes=[
            pltpu.VMEM((1, g * tq), F32),
            pltpu.VMEM((LANES, g * tq), F32),
            pltpu.VMEM((2, tk, g * tq), F32),
        ],
        compiler_params=_cparams(("parallel", "parallel", "parallel"), VMEM_LIMIT),
        name="attn",
    )(qt, kp, vt)


def _merge_kernel(x_ref, ym_ref, ya_ref, gg_ref, mod_ref, wbm_ref, wba_ref, wo_ref, g2_ref, wr_ref, br_ref,
                  xn_ref, h2_ref, lt_ref):
    d = x_ref.shape[-1]
    a = jnp.dot(ym_ref[...], wbm_ref[...], preferred_element_type=F32)
    bm = jnp.dot(ya_ref[...], wba_ref[...], preferred_element_type=F32)
    gm = jax.nn.sigmoid(gg_ref[:, 0:d].astype(F32))
    ga = jax.nn.sigmoid(gg_ref[:, d:2 * d].astype(F32))
    merged = (gm * a + ga * bm).astype(BF16)
    u = jnp.dot(merged, wo_ref[...], preferred_element_type=F32)
    xn = x_ref[...] + mod_ref[2:3, :] * u
    xn_ref[...] = xn
    h2 = _rms(xn, g2_ref[...]) * (1.0 + mod_ref[4:5, :]) + mod_ref[3:4, :]
    h2_ref[...] = h2
    lt = lax.dot_general(wr_ref[...], h2, (((1,), (1,)), ((), ())), precision=HIGHEST,
                         preferred_element_type=F32)
    lt_ref[...] = lt + br_ref[...]


def _merge(x, ym, ya, gg, mod_l, wbm, wba, wo, g2, wr_t, br_t, *, tm):
    b, s, d = x.shape
    ns = s // tm
    rr = wr_t.shape[0]
    full = lambda shp: pl.BlockSpec(shp, lambda bi, i: tuple(0 for _ in shp))
    return pl.pallas_call(
        _merge_kernel,
        out_shape=(
            jax.ShapeDtypeStruct((b, s, d), F32),
            jax.ShapeDtypeStruct((b * s, d), F32),
            jax.ShapeDtypeStruct((rr, b * s), F32),
        ),
        grid=(b, ns),
        in_specs=[
            pl.BlockSpec((None, tm, d), lambda bi, i: (bi, i, 0)),
            pl.BlockSpec((None, tm, MLSTM_W), lambda bi, i: (bi, i, 0)),
            pl.BlockSpec((None, tm, ATTN_W), lambda bi, i: (bi, i, 0)),
            pl.BlockSpec((None, tm, 2 * d), lambda bi, i: (bi, i, 0)),
            pl.BlockSpec((None, 6, d), lambda bi, i: (bi, 0, 0)),
            full((MLSTM_W, d)),
            full((ATTN_W, d)),
            full((d, d)),
            full((1, d)),
            full((rr, d)),
            full((rr, 1)),
        ],
        out_specs=(
            pl.BlockSpec((None, tm, d), lambda bi, i: (bi, i, 0)),
            pl.BlockSpec((tm, d), lambda bi, i: (bi * ns + i, 0)),
            pl.BlockSpec((rr, tm), lambda bi, i: (0, bi * ns + i)),
        ),
        compiler_params=_cparams(("parallel", "parallel"), VMEM_LIMIT),
        name="merge",
    )(x, ym, ya, gg, mod_l, wbm, wba, wo, g2, wr_t, br_t)


def _route_kernel(lt_ref, pos_ref, wts_ref, runs_ref, carry_ref):
    tb = lt_ref.shape[1]
    epg = EXPERTS_PER_GROUP

    @pl.when(pl.program_id(0) == 0)
    def _():
        carry_ref[...] = jnp.zeros_like(carry_ref)

    row8 = lax.broadcasted_iota(jnp.int32, (8, tb), 0)
    gl = jnp.where(row8 < N_GROUPS, lt_ref[0:8, :], -jnp.inf)
    ge = jnp.exp(gl - jnp.max(gl, axis=0, keepdims=True))
    pg = ge / jnp.sum(ge, axis=0, keepdims=True)
    p_top = jnp.max(pg, axis=0, keepdims=True)
    g_idx = jnp.min(jnp.where(pg == p_top, row8, 8), axis=0, keepdims=True)

    el = jnp.zeros((epg, tb), F32)
    for g in range(N_GROUPS):
        el = jnp.where(g_idx == g, lt_ref[8 + g * epg:8 + (g + 1) * epg, :], el)
    ee = jnp.exp(el - jnp.max(el, axis=0, keepdims=True))
    pe = ee / jnp.sum(ee, axis=0, keepdims=True)
    v1 = jnp.max(pe, axis=0, keepdims=True)
    i1 = jnp.min(jnp.where(pe == v1, row8, 8), axis=0, keepdims=True)
    pe2 = jnp.where(row8 == i1, -1.0, pe)
    v2 = jnp.max(pe2, axis=0, keepdims=True)
    i2 = jnp.min(jnp.where(pe2 == v2, row8, 8), axis=0, keepdims=True)
    denom = v1 + v2
    w0 = v1 / denom * p_top
    w1 = v2 / denom * p_top
    e0 = g_idx * epg + i1
    e1 = g_idx * epg + i2

    rowe = lax.broadcasted_iota(jnp.int32, (N_EXPERTS, tb), 0)
    oh0 = rowe == e0
    oh1 = rowe == e1
    oh = jnp.where(oh0 | oh1, 1.0, 0.0)
    src = lax.broadcasted_iota(jnp.int32, (tb, tb), 0)
    dst = lax.broadcasted_iota(jnp.int32, (tb, tb), 1)
    strict = jnp.where(src < dst, 1.0, 0.0).astype(BF16)
    cum = jnp.dot(oh.astype(BF16), strict, preferred_element_type=F32)
    cnt_col = jnp.sum(oh, axis=1, keepdims=True)
    seg_col = jnp.floor((cnt_col + (SEG_ALIGN - 1.0)) * (1.0 / SEG_ALIGN)) * SEG_ALIGN
    er = lax.broadcasted_iota(jnp.int32, (N_EXPERTS, N_EXPERTS), 0)
    ec = lax.broadcasted_iota(jnp.int32, (N_EXPERTS, N_EXPERTS), 1)
    before = jnp.where(ec < er, 1.0, 0.0).astype(BF16)
    start_col = jnp.dot(before, jnp.broadcast_to(seg_col, (N_EXPERTS, LANES)).astype(BF16),
                        preferred_element_type=F32)[:, 0:1]
    base = start_col + cum
    p0 = jnp.sum(jnp.where(oh0, base, 0.0), axis=0, keepdims=True)
    p1 = jnp.sum(jnp.where(oh1, base, 0.0), axis=0, keepdims=True)
    pos_ref[...] = jnp.where(row8 == 0, p0, jnp.where(row8 == 1, p1, 0.0)).astype(jnp.int32)
    wts_ref[...] = jnp.where(row8 == 0, w0, jnp.where(row8 == 1, w1, 0.0))

    ohp = jnp.concatenate([oh, jnp.zeros((LANES - N_EXPERTS, tb), F32)], axis=0).astype(BF16)
    cnt_row = _nt(jnp.ones((8, tb), BF16), ohp)
    seg_row = jnp.floor((cnt_row + (SEG_ALIGN - 1.0)) * (1.0 / SEG_ALIGN)) * SEG_ALIGN
    carry = carry_ref[...]
    runs_ref[0] = (seg_row + pltpu.roll(carry, N_EXPERTS, axis=1)).astype(jnp.int32)
    carry_ref[...] = carry + seg_row


def _route(lt, *, tb):
    rr, t = lt.shape
    nblk = t // tb
    return pl.pallas_call(
        _route_kernel,
        out_shape=(
            jax.ShapeDtypeStruct((8, t), jnp.int32),
            jax.ShapeDtypeStruct((8, t), F32),
            jax.ShapeDtypeStruct((nblk, 8, LANES), jnp.int32),
        ),
        grid=(nblk,),
        in_specs=[pl.BlockSpec((rr, tb), lambda i: (0, i))],
        out_specs=(
            pl.BlockSpec((8, tb), lambda i: (0, i)),
            pl.BlockSpec((8, tb), lambda i: (0, i)),
            pl.BlockSpec((1, 8, LANES), lambda i: (i, 0, 0)),
        ),
        scratch_shapes=[pltpu.VMEM((8, LANES), F32)],
        compiler_params=_cparams(("arbitrary",), VMEM_LIMIT),
        name="route",
    )(lt)


def _segment_copies(blk, runs_ref, off_ref, local_ref, hbm_ref, sem, *, to_hbm, wait):
    big = 4 * SEG_ALIGN

    def piece(s, d, rows):
        s = pl.multiple_of(s, SEG_ALIGN)
        d = pl.multiple_of(d, SEG_ALIGN)
        loc = local_ref.at[pl.ds(s, rows)]
        far = hbm_ref.at[pl.ds(d, rows)]
        cp = pltpu.make_async_copy(loc, far, sem) if to_hbm else pltpu.make_async_copy(far, loc, sem)
        if wait:
            cp.wait()
        else:
            cp.start()

    def expert(e, s0):
        n = runs_ref[blk * 2 * N_EXPERTS + e]
        d0 = off_ref[e] + runs_ref[blk * 2 * N_EXPERTS + N_EXPERTS + e]
        nbig = n >> 5
        nsmall = (n >> 3) & 3

        def big_piece(c, carry):
            piece(s0 + c * big, d0 + c * big, big)
            return carry

        def small_piece(c, carry):
            piece(s0 + nbig * big + c * SEG_ALIGN, d0 + nbig * big + c * SEG_ALIGN, SEG_ALIGN)
            return carry

        lax.fori_loop(0, nbig, big_piece, 0)
        lax.fori_loop(0, nsmall, small_piece, 0)
        return s0 + n

    lax.fori_loop(0, N_EXPERTS, expert, jnp.int32(0))


def _dispatch_kernel(runs_ref, off_ref, last_ref, h_ref, pos_ref, xg_ref, sbuf_ref, zero_ref, sem, zsem, *, tme):
    i = pl.program_id(0)
    tb = h_ref.shape[0]
    rp = sbuf_ref.shape[0]

    @pl.when(i == 0)
    def _():
        zero_ref[...] = jnp.zeros_like(zero_ref)
        for wait in (False, True):
            def tail_tile(j, carry, wait=wait):
                cp = pltpu.make_async_copy(zero_ref, xg_ref.at[pl.ds(pl.multiple_of(j * tme, tme), tme)], zsem)
                if wait:
                    cp.wait()
                else:
                    cp.start()
                return carry

            lax.fori_loop(last_ref[N_EXPERTS], xg_ref.shape[0] // tme, tail_tile, 0)
            for e in range(N_EXPERTS):
                @pl.when(last_ref[e] >= 0)
                def _():
                    cp = pltpu.make_async_copy(
                        zero_ref, xg_ref.at[pl.ds(pl.multiple_of(last_ref[e], SEG_ALIGN), tme)], zsem)
                    if wait:
                        cp.wait()
                    else:
                        cp.start()

    ii = lax.broadcasted_iota(jnp.int32, (rp, tb), 0)
    perm = jnp.where((ii == pos_ref[0:1, :]) | (ii == pos_ref[1:2, :]), 1.0, 0.0).astype(BF16)
    sbuf_ref[...] = jnp.dot(perm, h_ref[...].astype(BF16), preferred_element_type=F32)
    _segment_copies(i, runs_ref, off_ref, sbuf_ref, xg_ref, sem, to_hbm=True, wait=False)
    _segment_copies(i, runs_ref, off_ref, sbuf_ref, xg_ref, sem, to_hbm=True, wait=True)


def _dispatch(runs, offsets, last_tile, h2, pos, p_rows, *, tb, tme):
    t, d = h2.shape
    rp = 2 * tb + SEG_ALIGN * N_EXPERTS
    kern = functools.partial(_dispatch_kernel, tme=tme)
    return pl.pallas_call(
        kern,
        out_shape=jax.ShapeDtypeStruct((p_rows, d), F32),
        grid_spec=pltpu.PrefetchScalarGridSpec(
            num_scalar_prefetch=3,
            grid=(t // tb,),
            in_specs=[
                pl.BlockSpec((tb, d), lambda i, *_: (i, 0)),
                pl.BlockSpec((8, tb), lambda i, *_: (0, i)),
            ],
            out_specs=pl.BlockSpec(memory_space=pl.ANY),
            scratch_shapes=[
                pltpu.VMEM((rp, d), F32),
                pltpu.VMEM((tme, d), F32),
                pltpu.SemaphoreType.DMA(()),
                pltpu.SemaphoreType.DMA(()),
            ],
        ),
        compiler_params=_cparams(("arbitrary",), VMEM_LIMIT),
        name="dispatch",
    )(runs, offsets, last_tile, h2, pos)


def _experts_kernel(te_ref, first_ref, used_ref, x_ref, wg_ref, wu_ref, wd_ref, y_ref, wgb, wub, wdb):
    i = pl.program_id(0)

    @pl.when(first_ref[i] == 1)
    def _():
        wgb[...] = wg_ref[...].astype(BF16)
        wub[...] = wu_ref[...].astype(BF16)
        wdb[...] = wd_ref[...].astype(BF16)

    @pl.when(used_ref[i] == 1)
    def _():
        x = x_ref[...].astype(BF16)
        a = jnp.dot(x, wgb[...], preferred_element_type=F32)
        u = jnp.dot(x, wub[...], preferred_element_type=F32)
        act = (a * jax.nn.sigmoid(a) * u).astype(BF16)
        y_ref[...] = jnp.dot(act, wdb[...], preferred_element_type=F32)

    @pl.when(used_ref[i] == 0)
    def _():
        y_ref[...] = jnp.zeros_like(y_ref)


def _experts(tile_expert, tile_first, tile_used, xg, w_gate, w_up, w_down, *, layer, tm):
    p_rows, d = xg.shape
    f = w_gate.shape[-1]
    nt = p_rows // tm
    wmap = lambda i, te, fi, us: (layer, te[i], 0, 0)
    return pl.pallas_call(
        _experts_kernel,
        out_shape=jax.ShapeDtypeStruct((p_rows, d), F32),
        grid_spec=pltpu.PrefetchScalarGridSpec(
            num_scalar_prefetch=3,
            grid=(nt,),
            in_specs=[
                pl.BlockSpec((tm, d), lambda i, te, fi, us: (i, 0)),
                pl.BlockSpec((None, None, d, f), wmap),
                pl.BlockSpec((None, None, d, f), wmap),
                pl.BlockSpec((None, None, f, d), wmap),
            ],
            out_specs=pl.BlockSpec((tm, d), lambda i, te, fi, us: (i, 0)),
            scratch_shapes=[
                pltpu.VMEM((d, f), BF16),
                pltpu.VMEM((d, f), BF16),
                pltpu.VMEM((f, d), BF16),
            ],
        ),
        compiler_params=_cparams(("arbitrary",), VMEM_LIMIT),
        name="experts",
    )(tile_expert, tile_first, tile_used, xg, w_gate, w_up, w_down)


def _combine_kernel(runs_ref, off_ref, x_ref, pos_ref, w_ref, mod_ref, fg_ref, y_ref, o_ref, ybuf_ref, sem, *, final):
    blk = pl.program_id(0) * pl.num_programs(1) + pl.program_id(1)
    tb = x_ref.shape[0]
    rp = ybuf_ref.shape[0]

    @pl.when(blk == 0)
    def _():
        ybuf_ref[...] = jnp.zeros_like(ybuf_ref)

    _segment_copies(blk, runs_ref, off_ref, ybuf_ref, y_ref, sem, to_hbm=False, wait=False)
    _segment_copies(blk, runs_ref, off_ref, ybuf_ref, y_ref, sem, to_hbm=False, wait=True)
    li = lax.broadcasted_iota(jnp.int32, (tb, rp), 1)
    wmat = (jnp.where(li == pos_ref[:, 0:1], w_ref[:, 0:1], 0.0)
            + jnp.where(li == pos_ref[:, 1:2], w_ref[:, 1:2], 0.0))
    moe = jnp.dot(wmat.astype(BF16), ybuf_ref[...].astype(BF16), preferred_element_type=F32)
    xo = x_ref[...] + mod_ref[5:6, :] * moe
    if final:
        xo = _rms(xo, fg_ref[...])
    o_ref[...] = xo


def _combine(runs, offsets, xn, pos_col, wts_col, mod_l, fg, y, *, tb, final):
    b, s, d = xn.shape
    ns = s // tb
    rp = 2 * tb + SEG_ALIGN * N_EXPERTS
    kern = functools.partial(_combine_kernel, final=final)
    return pl.pallas_call(
        kern,
        out_shape=jax.ShapeDtypeStruct((b, s, d), F32),
        grid_spec=pltpu.PrefetchScalarGridSpec(
            num_scalar_prefetch=2,
            grid=(b, ns),
            in_specs=[
                pl.BlockSpec((None, tb, d), lambda bi, i, *_: (bi, i, 0)),
                pl.BlockSpec((tb, 8), lambda bi, i, *_: (bi * ns + i, 0)),
                pl.BlockSpec((tb, 8), lambda bi, i, *_: (bi * ns + i, 0)),
                pl.BlockSpec((None, 6, d), lambda bi, i, *_: (bi, 0, 0)),
                pl.BlockSpec((1, d), lambda bi, i, *_: (0, 0)),
                pl.BlockSpec(memory_space=pl.ANY),
            ],
            out_specs=pl.BlockSpec((None, tb, d), lambda bi, i, *_: (bi, i, 0)),
            scratch_shapes=[
                pltpu.VMEM((rp, d), F32),
                pltpu.SemaphoreType.DMA(()),
            ],
        ),
        compiler_params=_cparams(("arbitrary", "arbitrary"), VMEM_LIMIT),
        name="combine",
    )(runs, offsets, xn, pos_col, wts_col, mod_l, fg, y)


def _rope_tables(seq):
    rows = seq // GRID_W
    row = jnp.repeat(jnp.arange(rows, dtype=F32), GRID_W)
    col = jnp.tile(jnp.arange(GRID_W, dtype=F32), rows)
    half = ATTN_HD // 2
    inv_freq = ROPE_BASE ** (-jnp.arange(0, half, 2, dtype=F32) / half)
    ang_r = row[:, None] * inv_freq
    ang_c = col[:, None] * inv_freq
    cos64 = jnp.concatenate([jnp.cos(ang_r), jnp.cos(ang_r), jnp.cos(ang_c), jnp.cos(ang_c)], axis=-1)
    sin64 = jnp.concatenate([-jnp.sin(ang_r), jnp.sin(ang_r), -jnp.sin(ang_c), jnp.sin(ang_c)], axis=-1)
    return jnp.tile(cos64, (1, LANES // ATTN_HD)), jnp.tile(sin64, (1, LANES // ATTN_HD))


def _plan_tiles(rows, tm, nt):
    tiles = (rows + tm - 1) // tm
    tile_end = jnp.cumsum(tiles)
    tile_start = tile_end - tiles
    offsets = (tile_start * tm).astype(jnp.int32)
    last_tile = jnp.where(tiles > 0, (tile_end - 1) * tm, -1).astype(jnp.int32)
    last_tile = jnp.concatenate([last_tile, tile_end[-1:].astype(jnp.int32)])
    tidx = jnp.arange(nt, dtype=jnp.int32)
    te = jnp.sum((tidx[:, None] >= tile_end[None, :]).astype(jnp.int32), axis=1)
    used = (tidx < tile_end[-1]).astype(jnp.int32)
    last_used = jnp.sum((tile_end[-1] - 1 >= tile_end).astype(jnp.int32))
    te = jnp.where(used == 1, te, last_used).astype(jnp.int32)
    prev = jnp.concatenate([jnp.full((1,), -1, jnp.int32), te[:-1]])
    first = ((te != prev) & (used == 1)).astype(jnp.int32)
    return offsets, last_tile, te, first, used


def _tiles(s, t):
    tm = 512 if s % 512 == 0 else 128
    tq = 256 if s % 256 == 0 else 128
    tk = 512 if s % 2048 == 0 else 128
    tme = 256
    tbr = 512 if s % 512 == 0 else 128
    nblk = t // tbr
    nt = -(-(2 * t + (SEG_ALIGN - 1) * N_EXPERTS * nblk) // tme) + N_EXPERTS
    return tm, tq, tk, tme, tbr, nt


def kernel(x, c, w_ada, b_ada, norm1_g, w_in, b_in, conv_w, conv_b, mlstm_norm_g, q_norm_g, k_norm_g,
           w_branch_m, w_branch_a, w_out, norm2_g, w_router_group, b_router_group, w_router_expert,
           b_router_expert, w_gate, w_up, w_down, final_norm_g):
    b, s, d = x.shape
    depth = w_ada.shape[0]
    t = b * s
    nh = MLSTM_HEADS
    tm, tq, tk, tme, tbr, nt = _tiles(s, t)
    p_rows = nt * tme

    mod = _adaln(c, w_ada, b_ada).reshape(depth, b, 6, d)
    cos_t, sin_t = _rope_tables(s)
    seg = np.arange(LANES) // ATTN_HD
    mseg = jnp.asarray((seg[:, None] == seg[None, :]).astype(np.float32) / ATTN_HD)

    o_g = 4 * MLSTM_W
    o_aq = o_g + 4 * nh

    for l in range(depth):
        w_l = w_in[l]
        w_main = jnp.concatenate([w_l[:, :o_g], w_l[:, o_aq:]], axis=1).astype(BF16)
        b_main = jnp.concatenate([b_in[l][:o_g], b_in[l][o_aq:]])[None, :]
        wg = jnp.transpose(w_l[:, o_g:o_aq].reshape(d, 4, nh), (2, 1, 0))
        wgt = jnp.concatenate([wg, jnp.zeros((nh, 4, d), F32)], axis=1).reshape(nh * 8, d).astype(BF16)
        bg = jnp.transpose(b_in[l][o_g:o_aq].reshape(4, nh))
        bgt = jnp.concatenate([bg, jnp.zeros((nh, 4), F32)], axis=1).reshape(nh * 8, 1)

        pm, pa, gg, gt = _in_proj(x, mod[l], norm1_g[l][None, :], w_main, b_main, wgt, bgt, tm=tm)
        ym = _mlstm(pm, gt, conv_w[l].reshape(CONV_W, 2 * MLSTM_W), conv_b[l][None, :],
                    mlstm_norm_g[l][None, :])
        qg = jnp.tile(q_norm_g[l], LANES // ATTN_HD)[None, :]
        kg = jnp.tile(k_norm_g[l], LANES // ATTN_HD)[None, :]
        qp, kp, vp = _attn_prep(pa, cos_t, sin_t, qg, kg, mseg, tm=tm)
        ya = _attention(qp, kp, vp, tq=tq, tk=tk)

        wr = jnp.concatenate([w_router_group[l], jnp.zeros((d, 8 - N_GROUPS), F32), w_router_expert[l]], axis=1)
        br = jnp.concatenate([b_router_group[l], jnp.zeros((8 - N_GROUPS,), F32), b_router_expert[l]])
        xn, h2, lt = _merge(x, ym, ya, gg, mod[l], w_branch_m[l].astype(BF16), w_branch_a[l].astype(BF16),
                            w_out[l].astype(BF16), norm2_g[l][None, :], jnp.transpose(wr), br[:, None], tm=tm)

        pos, wts, runs = _route(lt, tb=tbr)
        runs = runs[:, 0, :2 * N_EXPERTS]
        rows = runs[-1, :N_EXPERTS] + runs[-1, N_EXPERTS:]
        offsets, last_tile, te, first, used = _plan_tiles(rows, tme, nt)
        runs = runs.reshape(-1)
        xg = _dispatch(runs, offsets, last_tile, h2, pos, p_rows, tb=tbr, tme=tme)
        y = _experts(te, first, used, xg, w_gate, w_up, w_down, layer=l, tm=tme)
        x = _combine(runs, offsets, xn, jnp.transpose(pos), jnp.transpose(wts), mod[l], final_norm_g[None, :], y,
                     tb=tbr, final=(l == depth - 1))
    return x
```

```python
import functools
import math

import jax
import jax.numpy as jnp
import numpy as np
from jax import lax
from jax.experimental import pallas as pl
from jax.experimental.pallas import tpu as pltpu

F32 = jnp.float32
BF16 = jnp.bfloat16
HIGHEST = lax.Precision.HIGHEST

GRID_W = 64
MLSTM_HEADS = 4
MLSTM_HD = 128
MLSTM_W = MLSTM_HEADS * MLSTM_HD
CONV_W = 5
ATTN_HEADS = 8
KV_HEADS = 2
ATTN_HD = 64
ATTN_W = ATTN_HEADS * ATTN_HD
KV_W = KV_HEADS * ATTN_HD
ROPE_BASE = 10000.0
N_GROUPS = 4
EXPERTS_PER_GROUP = 8
N_EXPERTS = N_GROUPS * EXPERTS_PER_GROUP
NORM_EPS = 1e-6

LANES = 128
MLSTM_CHUNK = 128
SEG_ALIGN = 16
SEG_SHIFT = 4
PV_ROWS = ATTN_HD + 16
VMEM_LIMIT = 56 * 1024 * 1024

PM_W = 4 * MLSTM_W
PA_W = ATTN_W + 2 * KV_W
GG_W = None
ROUTER_ROWS = 8 + N_EXPERTS


def _cparams(sem, vmem=None):
    return pltpu.CompilerParams(dimension_semantics=sem, vmem_limit_bytes=vmem)


def _rms(x, g):
    ms = jnp.mean(x * x, axis=-1, keepdims=True)
    return x * lax.rsqrt(ms + NORM_EPS) * g


def _adaln_kernel(c_ref, w_ref, b_ref, o_ref):
    c = c_ref[...]
    cond = c * jax.nn.sigmoid(c)
    o_ref[0] = jnp.dot(cond, w_ref[0], precision=HIGHEST, preferred_element_type=F32) + b_ref[0]


def _adaln(c, w_ada, b_ada):
    depth, d, n = w_ada.shape
    b = c.shape[0]
    tn = 1536 if n % 1536 == 0 else n
    return pl.pallas_call(
        _adaln_kernel,
        out_shape=jax.ShapeDtypeStruct((depth, b, n), F32),
        grid=(depth, n // tn),
        in_specs=[
            pl.BlockSpec((b, d), lambda l, j: (0, 0)),
            pl.BlockSpec((1, d, tn), lambda l, j: (l, 0, j)),
            pl.BlockSpec((1, 1, tn), lambda l, j: (l, 0, j)),
        ],
        out_specs=pl.BlockSpec((1, b, tn), lambda l, j: (l, 0, j)),
        compiler_params=_cparams(("parallel", "parallel"), VMEM_LIMIT),
        name="adaln",
    )(c, w_ada, b_ada.reshape(depth, 1, n))


def _in_proj_kernel(x_ref, mod_ref, g_ref, w_ref, b_ref, wgt_ref, bgt_ref,
                    pm_ref, pa_ref, gg_ref, gt_ref, *, col_chunk):
    x = x_ref[...]
    y = _rms(x, g_ref[...])
    h = y * (1.0 + mod_ref[1:2, :]) + mod_ref[0:1, :]
    hb = h.astype(BF16)
    c0 = 0
    for o_ref in (pm_ref, pa_ref, gg_ref):
        width = o_ref.shape[-1]
        for a in range(0, width, col_chunk):
            e = min(a + col_chunk, width)
            acc = jnp.dot(hb, w_ref[:, c0 + a:c0 + e], preferred_element_type=F32)
            o_ref[:, a:e] = (acc + b_ref[:, c0 + a:c0 + e]).astype(o_ref.dtype)
        c0 += width
    gt = lax.dot_general(wgt_ref[...], hb, (((1,), (1,)), ((), ())), preferred_element_type=F32)
    gt_ref[...] = gt + bgt_ref[...]


def _in_proj(x, mod_l, g1, w_main, b_main, wgt, bgt, *, tm):
    b, s, d = x.shape
    nw = w_main.shape[1]
    gg_w = nw - PM_W - PA_W
    grows = wgt.shape[0]
    kern = functools.partial(_in_proj_kernel, col_chunk=512)
    return pl.pallas_call(
        kern,
        out_shape=(
            jax.ShapeDtypeStruct((b, s, PM_W), BF16),
            jax.ShapeDtypeStruct((b, s, PA_W), BF16),
            jax.ShapeDtypeStruct((b, s, gg_w), BF16),
            jax.ShapeDtypeStruct((b, grows, s), F32),
        ),
        grid=(b, s // tm),
        in_specs=[
            pl.BlockSpec((None, tm, d), lambda bi, i: (bi, i, 0)),
            pl.BlockSpec((None, 6, d), lambda bi, i: (bi, 0, 0)),
            pl.BlockSpec((1, d), lambda bi, i: (0, 0)),
            pl.BlockSpec((d, nw), lambda bi, i: (0, 0)),
            pl.BlockSpec((1, nw), lambda bi, i: (0, 0)),
            pl.BlockSpec((grows, d), lambda bi, i: (0, 0)),
            pl.BlockSpec((grows, 1), lambda bi, i: (0, 0)),
        ],
        out_specs=(
            pl.BlockSpec((None, tm, PM_W), lambda bi, i: (bi, i, 0)),
            pl.BlockSpec((None, tm, PA_W), lambda bi, i: (bi, i, 0)),
            pl.BlockSpec((None, tm, gg_w), lambda bi, i: (bi, i, 0)),
            pl.BlockSpec((None, grows, tm), lambda bi, i: (bi, 0, i)),
        ),
        compiler_params=_cparams(("parallel", "parallel"), VMEM_LIMIT),
        name="in_proj",
    )(x, mod_l, g1, w_main, b_main, wgt, bgt)


def _mlstm_kernel(q_ref, k_ref, v_ref, o_ref, gt_ref, cwq_ref, cwk_ref, cbq_ref, cbk_ref, ng_ref,
                  y_ref, xs_ref, qc_ref, kc_ref, tab_ref, hs_ref, stf_ref, stb_ref, mf_ref, mb_ref,
                  *, seq, blk):
    L = MLSTM_CHUNK
    nc = seq // L
    half = nc // 2
    pad = 8

    zeros_pad = jnp.zeros((pad, LANES), F32)
    xs_ref[0:pad, :] = zeros_pad
    xs_ref[pad + seq:pad + seq + pad, :] = zeros_pad

    def conv_pass(src_ref, w_ref, b_ref, dst_ref, scale):
        for r0 in range(0, seq, blk):
            xs_ref[pad + r0:pad + r0 + blk, :] = src_ref[r0:r0 + blk, :].astype(F32)
        for r0 in range(0, seq, blk):
            acc = jnp.zeros((blk, LANES), F32) + b_ref[...]
            for j in range(CONV_W):
                off = pad + r0 + j - CONV_W // 2
                acc = acc + w_ref[j:j + 1, :] * xs_ref[off:off + blk, :]
            yv = acc * jax.nn.sigmoid(acc) * scale
            dst_ref[r0:r0 + blk, :] = yv.astype(BF16)

    conv_pass(q_ref, cwq_ref, cbq_ref, qc_ref, MLSTM_HD ** -0.5)
    conv_pass(k_ref, cwk_ref, cbk_ref, kc_ref, 1.0)

    jj = lax.broadcasted_iota(jnp.int32, (L, L), 0)
    ll = lax.broadcasted_iota(jnp.int32, (L, L), 1)
    tri = jnp.concatenate([jnp.where(jj <= ll, 1.0, 0.0), jnp.where(jj >= ll, 1.0, 0.0)], axis=1).astype(BF16)

    def split3(x):
        hi = x.astype(BF16)
        r1 = x - hi.astype(F32)
        mid = r1.astype(BF16)
        lo = (r1 - mid.astype(F32)).astype(BF16)
        return jnp.concatenate([hi, mid, lo], axis=0)

    def gate_body(c, carry):
        t0 = pl.multiple_of(c * L, L)
        g = gt_ref[:, pl.ds(t0, L)]
        lf = jax.nn.log_sigmoid(g)
        c3 = jnp.dot(split3(lf), tri, preferred_element_type=F32)
        c2 = c3[0:8] + c3[8:16] + c3[16:24]
        cf, cb = c2[:, 0:L], c2[:, L:2 * L]
        row = lax.broadcasted_iota(jnp.int32, (8, L), 0)
        tab = jnp.where(row == 0, g, jnp.where(row == 1, cf, jnp.where(row == 2, g, cb)))
        tab = jnp.where(row < 4, tab, 0.0)
        tab_ref[:, pl.ds(t0, L)] = tab
        return carry

    lax.fori_loop(0, nc, gate_body, 0)

    srow = lax.broadcasted_iota(jnp.int32, (24, 4 * L), 0) % 8
    scol = lax.broadcasted_iota(jnp.int32, (24, 4 * L), 1) // L
    sel3 = jnp.where(srow == scol, 1.0, 0.0).astype(BF16)

    stf_ref[...] = jnp.zeros_like(stf_ref)
    stb_ref[...] = jnp.zeros_like(stb_ref)
    mf_ref[...] = jnp.zeros_like(mf_ref)
    mb_ref[...] = jnp.zeros_like(mb_ref)
    ones_blk = jnp.ones((L, L), BF16)

    def chain(t0, st_ref, m_ref, fwd):
        q = qc_ref[pl.ds(t0, L), :]
        k = kc_ref[pl.ds(t0, L), :]
        v = v_ref[pl.ds(t0, L), :]
        r8 = tab_ref[:, pl.ds(t0, L)]
        cols = lax.dot_general(split3(r8), sel3, (((0,), (0,)), ((), ())),
                               preferred_element_type=F32)
        if fwd:
            li_c, b_c = cols[:, 0:L], cols[:, L:2 * L]
            li_r, b_r = r8[0:1, :], r8[1:2, :]
            bl = b_c[L - 1:L, :]
            mask = ll <= jj
        else:
            li_c, b_c = cols[:, 2 * L:3 * L], cols[:, 3 * L:4 * L]
            li_r, b_r = r8[2:3, :], r8[3:4, :]
            bl = b_c[0:1, :]
            mask = ll >= jj
        m_prev = m_ref[...]
        st = st_ref[...]
        a_c = bl - b_c + li_c
        a_max = jnp.max(a_c, axis=0, keepdims=True)
        w_c = jnp.exp(a_c - a_max)
        vw = (v.astype(F32) * w_c).astype(BF16)
        cat = jnp.concatenate([vw, w_c.astype(BF16)], axis=1)
        chunk2 = lax.dot_general(k, cat, (((0,), (0,)), ((), ())), preferred_element_type=F32)
        d = jnp.where(mask, b_c + (li_r - b_r), -jnp.inf)
        dmax = jnp.max(d, axis=1, keepdims=True)
        m_inter = b_c + m_prev
        m_out = jnp.maximum(m_inter, dmax)
        dw = jnp.exp(d - m_out)
        inter_w = jnp.exp(m_inter - m_out)
        s = lax.dot_general(q, k, (((1,), (1,)), ((), ())), preferred_element_type=F32) * dw
        vcat = jnp.concatenate([v, ones_blk], axis=1)
        r2 = jnp.dot(s.astype(BF16), vcat, preferred_element_type=F32)
        q2 = jnp.dot(q, st.astype(BF16), preferred_element_type=F32)
        num = r2[:, 0:L] + inter_w * q2[:, 0:L]
        den = r2[:, L:2 * L] + inter_w * q2[:, L:2 * L]
        hout = num / jnp.maximum(jnp.abs(den), jnp.exp(-m_out))
        m_new = jnp.maximum(bl + m_prev, a_max)
        decay = jnp.exp(bl + m_prev - m_new)
        inject = jnp.exp(a_max - m_new)
        st_ref[...] = decay[:, 0:1] * st + inject[:, 0:1] * chunk2
        m_ref[...] = m_new
        return hout

    def finalize(t0, h):
        yv = _rms(h, ng_ref[...]) * jax.nn.sigmoid(o_ref[pl.ds(t0, L), :].astype(F32))
        y_ref[pl.ds(t0, L), :] = yv.astype(y_ref.dtype)

    unroll = 4 if half % 4 == 0 else 1

    def first_half(i, carry):
        for u in range(unroll):
            c = i * unroll + u
            tf = pl.multiple_of(c * L, L)
            tb = pl.multiple_of((nc - 1 - c) * L, L)
            hs_ref[pl.ds(tf, L), :] = chain(tf, stf_ref, mf_ref, True)
            hs_ref[pl.ds(tb, L), :] = chain(tb, stb_ref, mb_ref, False)
        return carry

    def second_half(i, carry):
        for u in range(unroll):
            c = half + i * unroll + u
            tf = pl.multiple_of(c * L, L)
            tb = pl.multiple_of((nc - 1 - c) * L, L)
            finalize(tf, hs_ref[pl.ds(tf, L), :] + chain(tf, stf_ref, mf_ref, True))
            finalize(tb, hs_ref[pl.ds(tb, L), :] + chain(tb, stb_ref, mb_ref, False))
        return carry

    lax.fori_loop(0, half // unroll, first_half, 0)
    lax.fori_loop(0, half // unroll, second_half, 0)


def _mlstm(pm, gt, conv_w, conv_b, norm_g):
    b, s, _ = pm.shape
    nh = MLSTM_HEADS
    assert s % (2 * MLSTM_CHUNK) == 0
    blk = 512 if s % 512 == 0 else MLSTM_CHUNK
    kern = functools.partial(_mlstm_kernel, seq=s, blk=blk)
    L = MLSTM_CHUNK
    col = lambda off: (lambda bi, h: (bi, 0, off + h))
    return pl.pallas_call(
        kern,
        out_shape=jax.ShapeDtypeStruct((b, s, MLSTM_W), BF16),
        grid=(b, nh),
        in_specs=[
            pl.BlockSpec((None, s, LANES), col(0)),
            pl.BlockSpec((None, s, LANES), col(nh)),
            pl.BlockSpec((None, s, LANES), col(2 * nh)),
            pl.BlockSpec((None, s, LANES), col(3 * nh)),
            pl.BlockSpec((None, 8, s), lambda bi, h: (bi, h, 0)),
            pl.BlockSpec((CONV_W, LANES), lambda bi, h: (0, h)),
            pl.BlockSpec((CONV_W, LANES), lambda bi, h: (0, nh + h)),
            pl.BlockSpec((1, LANES), lambda bi, h: (0, h)),
            pl.BlockSpec((1, LANES), lambda bi, h: (0, nh + h)),
            pl.BlockSpec((1, LANES), lambda bi, h: (0, h)),
        ],
        out_specs=pl.BlockSpec((None, s, LANES), lambda bi, h: (bi, 0, h)),
        scratch_shapes=[
            pltpu.VMEM((s + 16, LANES), F32),
            pltpu.VMEM((s, LANES), BF16),
            pltpu.VMEM((s, LANES), BF16),
            pltpu.VMEM((8, s), F32),
            pltpu.VMEM((s, LANES), F32),
            pltpu.VMEM((L, 2 * L), F32),
            pltpu.VMEM((L, 2 * L), F32),
            pltpu.VMEM((1, L), F32),
            pltpu.VMEM((1, L), F32),
        ],
        compiler_params=_cparams(("parallel", "parallel"), VMEM_LIMIT),
        name="mlstm",
    )(pm, pm, pm, pm, gt, conv_w, conv_w, conv_b, conv_b, norm_g)


def _swap16(y, lane):
    fwd = pltpu.roll(y, LANES - 16, axis=1)
    bwd = pltpu.roll(y, 16, axis=1)
    return jnp.where((lane % 32) < 16, fwd, bwd)


def _nt(a, b):
    return lax.dot_general(a, b, (((1,), (1,)), ((), ())), preferred_element_type=F32)


def _attn_prep_kernel(pa_ref, cos_ref, sin_ref, qg_ref, kg_ref, mseg_ref, qt_ref, kp_ref, vt_ref, *, q_scale):
    tm = pa_ref.shape[0]
    lane = lax.broadcasted_iota(jnp.int32, (tm, LANES), 1)
    cos = cos_ref[...]
    sin = sin_ref[...]
    mseg = mseg_ref[...]
    r = lax.broadcasted_iota(jnp.int32, (LANES, LANES), 0)
    c = lax.broadcasted_iota(jnp.int32, (LANES, LANES), 1)
    sel_lo = jnp.where((r == c) & (r < ATTN_HD), 1.0, 0.0).astype(BF16)
    sel_hi = jnp.where((c == r + ATTN_HD) & (r < ATTN_HD), 1.0, 0.0).astype(BF16)
    ones_row = jnp.where(lax.broadcasted_iota(jnp.int32, (LANES, tm), 0) == ATTN_HD, 1.0, 0.0)

    def norm_rope(x, g, scale):
        xx = x * x
        hi = xx.astype(BF16)
        lo = (xx - hi.astype(F32)).astype(BF16)
        ms = jnp.dot(jnp.concatenate([hi, lo], axis=1), mseg, preferred_element_type=F32)
        y = x * lax.rsqrt(ms + NORM_EPS) * g
        return (y * cos + _swap16(y, lane) * sin) * scale

    for j in range(ATTN_W // LANES):
        x = pa_ref[:, j * LANES:(j + 1) * LANES].astype(F32)
        y = norm_rope(x, qg_ref[...], q_scale).astype(BF16)
        qt_ref[2 * j] = _nt(sel_lo, y).astype(BF16)
        qt_ref[2 * j + 1] = _nt(sel_hi, y).astype(BF16)
    for j in range(KV_W // LANES):
        x = pa_ref[:, ATTN_W + j * LANES:ATTN_W + (j + 1) * LANES].astype(F32)
        y = norm_rope(x, kg_ref[...], 1.0)
        kp_ref[2 * j] = jnp.where(lane < ATTN_HD, y, 0.0).astype(BF16)
        kp_ref[2 * j + 1] = jnp.where(lane < ATTN_HD, pltpu.roll(y, ATTN_HD, axis=1), 0.0).astype(BF16)
        xv = pa_ref[:, ATTN_W + KV_W + j * LANES:ATTN_W + KV_W + (j + 1) * LANES]
        vt_ref[2 * j] = (_nt(sel_lo, xv) + ones_row).astype(BF16)
        vt_ref[2 * j + 1] = (_nt(sel_hi, xv) + ones_row).astype(BF16)


def _attn_prep(pa, cos_t, sin_t, qg, kg, mseg, *, tm):
    b, s, _ = pa.shape
    q_scale = (ATTN_HD ** -0.5) * math.log2(math.e)
    kern = functools.partial(_attn_prep_kernel, q_scale=q_scale)
    return pl.pallas_call(
        kern,
        out_shape=(
            jax.ShapeDtypeStruct((b, ATTN_HEADS, LANES, s), BF16),
            jax.ShapeDtypeStruct((b, KV_HEADS, s, LANES), BF16),
            jax.ShapeDtypeStruct((b, KV_HEADS, LANES, s), BF16),
        ),
        grid=(b, s // tm),
        in_specs=[
            pl.BlockSpec((None, tm, PA_W), lambda bi, i: (bi, i, 0)),
            pl.BlockSpec((tm, LANES), lambda bi, i: (i, 0)),
            pl.BlockSpec((tm, LANES), lambda bi, i: (i, 0)),
            pl.BlockSpec((1, LANES), lambda bi, i: (0, 0)),
            pl.BlockSpec((1, LANES), lambda bi, i: (0, 0)),
            pl.BlockSpec((2 * LANES, LANES), lambda bi, i: (0, 0)),
        ],
        out_specs=(
            pl.BlockSpec((None, ATTN_HEADS, LANES, tm), lambda bi, i: (bi, 0, 0, i)),
            pl.BlockSpec((None, KV_HEADS, tm, LANES), lambda bi, i: (bi, 0, i, 0)),
            pl.BlockSpec((None, KV_HEADS, LANES, tm), lambda bi, i: (bi, 0, 0, i)),
        ),
        compiler_params=_cparams(("parallel", "parallel"), VMEM_LIMIT),
        name="attn_prep",
    )(pa, cos_t, sin_t, qg, kg, mseg)


def _attn_kernel(qt_ref, k_ref, vt_ref, o_ref, m_ref, acc_ref, s_ref, *, tk):
    g, _, tq = qt_ref.shape
    s = k_ref.shape[0]
    n = s // tk
    m_ref[...] = jnp.full(m_ref.shape, -jnp.inf, F32)
    acc_ref[...] = jnp.zeros(acc_ref.shape, F32)

    def scores(j, slot, h):
        t0 = pl.multiple_of(j * tk, tk)
        s_ref[slot, :, h * tq:(h + 1) * tq] = jnp.dot(k_ref[pl.ds(t0, tk), :], qt_ref[h],
                                                      preferred_element_type=F32)

    def softmax_pv(j, slot, h):
        t0 = pl.multiple_of(j * tk, tk)
        cols = slice(h * tq, (h + 1) * tq)
        sc = s_ref[slot, :, cols]
        m_prev = m_ref[:, cols]
        m_new = jnp.maximum(m_prev, jnp.max(sc, axis=0, keepdims=True))
        alpha = jnp.exp2(m_prev - m_new)
        p = jnp.exp2(sc - m_new).astype(BF16)
        pv = jnp.dot(vt_ref[0:PV_ROWS, pl.ds(t0, tk)], p, preferred_element_type=F32)
        acc_ref[:, cols] = alpha * acc_ref[:, cols] + pv
        m_ref[:, cols] = m_new

    def step(j, slot, prefetch):
        for h in range(g):
            if prefetch:
                scores(j + 1, 1 - slot, h)
            softmax_pv(j, slot, h)

    for h in range(g):
        scores(0, 0, h)

    def pair(i, carry):
        step(2 * i, 0, True)
        step(2 * i + 1, 1, True)
        return carry

    lax.fori_loop(0, n // 2 - 1, pair, 0)
    step(n - 2, 0, True)
    step(n - 1, 1, False)
    acc = acc_ref[...]
    o = (acc[0:ATTN_HD, :] / acc[ATTN_HD:ATTN_HD + 1, :]).astype(BF16)
    r = lax.broadcasted_iota(jnp.int32, (LANES, LANES), 0)
    c = lax.broadcasted_iota(jnp.int32, (LANES, LANES), 1)
    eye = jnp.where(r == c, 1.0, 0.0).astype(BF16)
    for pair in range(g // 2):
        rows = jnp.concatenate([o[:, (2 * pair) * tq:(2 * pair + 1) * tq],
                                o[:, (2 * pair + 1) * tq:(2 * pair + 2) * tq]], axis=0)
        out = lax.dot_general(rows, eye, (((0,), (0,)), ((), ())), preferred_element_type=F32)
        o_ref[:, pair * LANES:(pair + 1) * LANES] = out.astype(o_ref.dtype)


def _attention(qt, kp, vt, *, tq, tk):
    b, nh, _, s = qt.shape
    g = nh // KV_HEADS
    assert (s // tk) % 2 == 0 and g % 2 == 0
    kern = functools.partial(_attn_kernel, tk=tk)
    return pl.pallas_call(
        kern,
        out_shape=jax.ShapeDtypeStruct((b, s, ATTN_W), BF16),
        grid=(b, KV_HEADS, s // tq),
        in_specs=[
            pl.BlockSpec((None, g, LANES, tq), lambda bi, kv, i: (bi, kv, 0, i)),
            pl.BlockSpec((None, None, s, LANES), lambda bi, kv, i: (bi, kv, 0, 0)),
            pl.BlockSpec((None, None, LANES, s), lambda bi, kv, i: (bi, kv, 0, 0)),
        ],
        out_specs=pl.BlockSpec((None, tq, g * ATTN_HD), lambda bi, kv, i: (bi, i, kv)),
        scratch_shapes=[
            pltpu.VMEM((1, g * tq), F32),
            pltpu.VMEM((PV_ROWS, g * tq), F32),
            pltpu.VMEM((2, tk, g * tq), F32),
        ],
        compiler_params=_cparams(("parallel", "parallel", "parallel"), VMEM_LIMIT),
        name="attn",
    )(qt, kp, vt)


def _merge_kernel(x_ref, ym_ref, ya_ref, gg_ref, mod_ref, wbm_ref, wba_ref, wo_ref, g2_ref, wr_ref, br_ref,
                  xn_ref, h2_ref, lt_ref):
    tm, d = x_ref.shape
    sub = tm
    for r0 in range(0, tm, sub):
        rows = slice(r0, r0 + sub)
        a = jnp.dot(ym_ref[rows, :], wbm_ref[...], preferred_element_type=F32)
        bm = jnp.dot(ya_ref[rows, :], wba_ref[...], preferred_element_type=F32)
        gm = jax.nn.sigmoid(gg_ref[rows, 0:d].astype(F32))
        ga = jax.nn.sigmoid(gg_ref[rows, d:2 * d].astype(F32))
        merged = (gm * a + ga * bm).astype(BF16)
        u = jnp.dot(merged, wo_ref[...], preferred_element_type=F32)
        xn = x_ref[rows, :] + mod_ref[2:3, :] * u
        xn_ref[rows, :] = xn
        h2 = _rms(xn, g2_ref[...]) * (1.0 + mod_ref[4:5, :]) + mod_ref[3:4, :]
        h2_ref[rows, :] = h2.astype(h2_ref.dtype)
        lt = lax.dot_general(wr_ref[...], h2, (((1,), (1,)), ((), ())), precision=HIGHEST,
                             preferred_element_type=F32)
        lt_ref[:, rows] = lt + br_ref[...]


def _merge(x, ym, ya, gg, mod_l, wbm, wba, wo, g2, wr_t, br_t, *, tm):
    b, s, d = x.shape
    ns = s // tm
    rr = wr_t.shape[0]
    full = lambda shp: pl.BlockSpec(shp, lambda bi, i: tuple(0 for _ in shp))
    return pl.pallas_call(
        _merge_kernel,
        out_shape=(
            jax.ShapeDtypeStruct((b, s, d), F32),
            jax.ShapeDtypeStruct((b * s, d), BF16),
            jax.ShapeDtypeStruct((rr, b * s), F32),
        ),
        grid=(b, ns),
        in_specs=[
            pl.BlockSpec((None, tm, d), lambda bi, i: (bi, i, 0)),
            pl.BlockSpec((None, tm, MLSTM_W), lambda bi, i: (bi, i, 0)),
            pl.BlockSpec((None, tm, ATTN_W), lambda bi, i: (bi, i, 0)),
            pl.BlockSpec((None, tm, 2 * d), lambda bi, i: (bi, i, 0)),
            pl.BlockSpec((None, 6, d), lambda bi, i: (bi, 0, 0)),
            full((MLSTM_W, d)),
            full((ATTN_W, d)),
            full((d, d)),
            full((1, d)),
            full((rr, d)),
            full((rr, 1)),
        ],
        out_specs=(
            pl.BlockSpec((None, tm, d), lambda bi, i: (bi, i, 0)),
            pl.BlockSpec((tm, d), lambda bi, i: (bi * ns + i, 0)),
            pl.BlockSpec((rr, tm), lambda bi, i: (0, bi * ns + i)),
        ),
        compiler_params=_cparams(("parallel", "parallel"), VMEM_LIMIT),
        name="merge",
    )(x, ym, ya, gg, mod_l, wbm, wba, wo, g2, wr_t, br_t)


def _route_kernel(lt_ref, pos_ref, wts_ref, runs_ref, carry_ref):
    tb = lt_ref.shape[1]
    epg = EXPERTS_PER_GROUP

    @pl.when(pl.program_id(0) == 0)
    def _():
        carry_ref[...] = jnp.zeros_like(carry_ref)

    row8 = lax.broadcasted_iota(jnp.int32, (8, tb), 0)
    gl = jnp.where(row8 < N_GROUPS, lt_ref[0:8, :], -jnp.inf)
    ge = jnp.exp(gl - jnp.max(gl, axis=0, keepdims=True))
    pg = ge / jnp.sum(ge, axis=0, keepdims=True)
    p_top = jnp.max(pg, axis=0, keepdims=True)
    g_idx = jnp.min(jnp.where(pg == p_top, row8, 8), axis=0, keepdims=True)

    el = jnp.zeros((epg, tb), F32)
    for g in range(N_GROUPS):
        el = jnp.where(g_idx == g, lt_ref[8 + g * epg:8 + (g + 1) * epg, :], el)
    ee = jnp.exp(el - jnp.max(el, axis=0, keepdims=True))
    pe = ee / jnp.sum(ee, axis=0, keepdims=True)
    v1 = jnp.max(pe, axis=0, keepdims=True)
    i1 = jnp.min(jnp.where(pe == v1, row8, 8), axis=0, keepdims=True)
    pe2 = jnp.where(row8 == i1, -1.0, pe)
    v2 = jnp.max(pe2, axis=0, keepdims=True)
    i2 = jnp.min(jnp.where(pe2 == v2, row8, 8), axis=0, keepdims=True)
    denom = v1 + v2
    w0 = v1 / denom * p_top
    w1 = v2 / denom * p_top
    e0 = g_idx * epg + i1
    e1 = g_idx * epg + i2

    rowe = lax.broadcasted_iota(jnp.int32, (N_EXPERTS, tb), 0)
    oh0 = rowe == e0
    oh1 = rowe == e1
    oh = jnp.where(oh0 | oh1, 1.0, 0.0)
    src = lax.broadcasted_iota(jnp.int32, (tb, tb), 0)
    dst = lax.broadcasted_iota(jnp.int32, (tb, tb), 1)
    strict = jnp.where(src < dst, 1.0, 0.0).astype(BF16)
    cum = jnp.dot(oh.astype(BF16), strict, preferred_element_type=F32)
    cnt_col = jnp.sum(oh, axis=1, keepdims=True)
    seg_col = jnp.floor((cnt_col + (SEG_ALIGN - 1.0)) * (1.0 / SEG_ALIGN)) * SEG_ALIGN
    er = lax.broadcasted_iota(jnp.int32, (N_EXPERTS, N_EXPERTS), 0)
    ec = lax.broadcasted_iota(jnp.int32, (N_EXPERTS, N_EXPERTS), 1)
    before = jnp.where(ec < er, 1.0, 0.0).astype(BF16)
    start_col = jnp.dot(before, jnp.broadcast_to(seg_col, (N_EXPERTS, LANES)).astype(BF16),
                        preferred_element_type=F32)[:, 0:1]
    base = start_col + cum
    p0 = jnp.sum(jnp.where(oh0, base, 0.0), axis=0, keepdims=True)
    p1 = jnp.sum(jnp.where(oh1, base, 0.0), axis=0, keepdims=True)
    pos_ref[...] = jnp.where(row8 == 0, p0, jnp.where(row8 == 1, p1, 0.0)).astype(jnp.int32)
    wts_ref[...] = jnp.where(row8 == 0, w0, jnp.where(row8 == 1, w1, 0.0))

    ohp = jnp.concatenate([oh, jnp.zeros((LANES - N_EXPERTS, tb), F32)], axis=0).astype(BF16)
    cnt_row = _nt(jnp.ones((8, tb), BF16), ohp)
    seg_row = jnp.floor((cnt_row + (SEG_ALIGN - 1.0)) * (1.0 / SEG_ALIGN)) * SEG_ALIGN
    carry = carry_ref[...]
    runs_ref[0] = (seg_row + pltpu.roll(carry, N_EXPERTS, axis=1)).astype(jnp.int32)
    carry_ref[...] = carry + seg_row


def _route(lt, *, tb):
    rr, t = lt.shape
    nblk = t // tb
    return pl.pallas_call(
        _route_kernel,
        out_shape=(
            jax.ShapeDtypeStruct((8, t), jnp.int32),
            jax.ShapeDtypeStruct((8, t), F32),
            jax.ShapeDtypeStruct((nblk, 8, LANES), jnp.int32),
        ),
        grid=(nblk,),
        in_specs=[pl.BlockSpec((rr, tb), lambda i: (0, i))],
        out_specs=(
            pl.BlockSpec((8, tb), lambda i: (0, i)),
            pl.BlockSpec((8, tb), lambda i: (0, i)),
            pl.BlockSpec((1, 8, LANES), lambda i: (i, 0, 0)),
        ),
        scratch_shapes=[pltpu.VMEM((8, LANES), F32)],
        compiler_params=_cparams(("arbitrary",), VMEM_LIMIT),
        name="route",
    )(lt)


def _segment_copies(blk, runs_ref, off_ref, local_ref, hbm_ref, sem, *, to_hbm, wait):
    big = 2 * SEG_ALIGN

    def piece(s, d, rows):
        s = pl.multiple_of(s, SEG_ALIGN)
        d = pl.multiple_of(d, SEG_ALIGN)
        loc = local_ref.at[pl.ds(s, rows)]
        far = hbm_ref.at[pl.ds(d, rows)]
        cp = pltpu.make_async_copy(loc, far, sem) if to_hbm else pltpu.make_async_copy(far, loc, sem)
        if wait:
            cp.wait()
        else:
            cp.start()

    def expert(e, s0):
        n = runs_ref[blk * 2 * N_EXPERTS + e]
        d0 = off_ref[e] + runs_ref[blk * 2 * N_EXPERTS + N_EXPERTS + e]
        nbig = n >> (SEG_SHIFT + 1)
        nsmall = (n >> SEG_SHIFT) & 1

        def big_piece(c, carry):
            piece(s0 + c * big, d0 + c * big, big)
            return carry

        def small_piece(c, carry):
            piece(s0 + nbig * big + c * SEG_ALIGN, d0 + nbig * big + c * SEG_ALIGN, SEG_ALIGN)
            return carry

        lax.fori_loop(0, nbig, big_piece, 0)
        lax.fori_loop(0, nsmall, small_piece, 0)
        return s0 + n

    lax.fori_loop(0, N_EXPERTS, expert, jnp.int32(0))


def _dispatch_kernel(runs_ref, off_ref, last_ref, h_ref, pos_ref, xg_ref, sbuf_ref, zero_ref, sem, zsem, *, tme):
    i = pl.program_id(0)
    tb = h_ref.shape[0]
    rp = sbuf_ref.shape[0]

    @pl.when(i == 0)
    def _():
        zero_ref[...] = jnp.zeros_like(zero_ref)
        for wait in (False, True):
            def tail_tile(j, carry, wait=wait):
                cp = pltpu.make_async_copy(zero_ref, xg_ref.at[pl.ds(pl.multiple_of(j * tme, tme), tme)], zsem)
                if wait:
                    cp.wait()
                else:
                    cp.start()
                return carry

            lax.fori_loop(last_ref[N_EXPERTS], xg_ref.shape[0] // tme, tail_tile, 0)
            for e in range(N_EXPERTS):
                @pl.when(last_ref[e] >= 0)
                def _():
                    cp = pltpu.make_async_copy(
                        zero_ref, xg_ref.at[pl.ds(pl.multiple_of(last_ref[e], SEG_ALIGN), tme)], zsem)
                    if wait:
                        cp.wait()
                    else:
                        cp.start()

    ii = lax.broadcasted_iota(jnp.int32, (rp, tb), 0)
    perm = jnp.where((ii == pos_ref[0:1, :]) | (ii == pos_ref[1:2, :]), 1.0, 0.0).astype(BF16)
    sbuf_ref[...] = jnp.dot(perm, h_ref[...], preferred_element_type=F32).astype(BF16)
    _segment_copies(i, runs_ref, off_ref, sbuf_ref, xg_ref, sem, to_hbm=True, wait=False)
    _segment_copies(i, runs_ref, off_ref, sbuf_ref, xg_ref, sem, to_hbm=True, wait=True)


def _dispatch(runs, offsets, last_tile, h2, pos, p_rows, *, tb, tme):
    t, d = h2.shape
    rp = 2 * tb + SEG_ALIGN * N_EXPERTS
    kern = functools.partial(_dispatch_kernel, tme=tme)
    return pl.pallas_call(
        kern,
        out_shape=jax.ShapeDtypeStruct((p_rows, d), BF16),
        grid_spec=pltpu.PrefetchScalarGridSpec(
            num_scalar_prefetch=3,
            grid=(t // tb,),
            in_specs=[
                pl.BlockSpec((tb, d), lambda i, *_: (i, 0)),
                pl.BlockSpec((8, tb), lambda i, *_: (0, i)),
            ],
            out_specs=pl.BlockSpec(memory_space=pl.ANY),
            scratch_shapes=[
                pltpu.VMEM((rp, d), BF16),
                pltpu.VMEM((tme, d), BF16),
                pltpu.SemaphoreType.DMA(()),
                pltpu.SemaphoreType.DMA(()),
            ],
        ),
        compiler_params=_cparams(("arbitrary",), VMEM_LIMIT),
        name="dispatch",
    )(runs, offsets, last_tile, h2, pos)


def _experts_kernel(te_ref, first_ref, used_ref, x_ref, wg_ref, wu_ref, wd_ref, y_ref, wgb, wub, wdb):
    i = pl.program_id(0)

    @pl.when(first_ref[i] == 1)
    def _():
        wgb[...] = wg_ref[...].astype(BF16)
        wub[...] = wu_ref[...].astype(BF16)
        wdb[...] = wd_ref[...].astype(BF16)

    @pl.when(used_ref[i] == 1)
    def _():
        x = x_ref[...]
        a = jnp.dot(x, wgb[...], preferred_element_type=F32)
        u = jnp.dot(x, wub[...], preferred_element_type=F32)
        act = (a * jax.nn.sigmoid(a) * u).astype(BF16)
        y_ref[...] = jnp.dot(act, wdb[...], preferred_element_type=F32).astype(y_ref.dtype)

    @pl.when(used_ref[i] == 0)
    def _():
        y_ref[...] = jnp.zeros_like(y_ref)


def _experts(tile_expert, tile_first, tile_used, xg, w_gate, w_up, w_down, *, layer, tm):
    p_rows, d = xg.shape
    f = w_gate.shape[-1]
    nt = p_rows // tm
    wmap = lambda i, te, fi, us: (layer, te[i], 0, 0)
    return pl.pallas_call(
        _experts_kernel,
        out_shape=jax.ShapeDtypeStruct((p_rows, d), BF16),
        grid_spec=pltpu.PrefetchScalarGridSpec(
            num_scalar_prefetch=3,
            grid=(nt,),
            in_specs=[
                pl.BlockSpec((tm, d), lambda i, te, fi, us: (i, 0)),
                pl.BlockSpec((None, None, d, f), wmap),
                pl.BlockSpec((None, None, d, f), wmap),
                pl.BlockSpec((None, None, f, d), wmap),
            ],
            out_specs=pl.BlockSpec((tm, d), lambda i, te, fi, us: (i, 0)),
            scratch_shapes=[
                pltpu.VMEM((d, f), BF16),
                pltpu.VMEM((d, f), BF16),
                pltpu.VMEM((f, d), BF16),
            ],
        ),
        compiler_params=_cparams(("arbitrary",), VMEM_LIMIT),
        name="experts",
    )(tile_expert, tile_first, tile_used, xg, w_gate, w_up, w_down)


def _combine_kernel(runs_ref, off_ref, x_ref, pos_ref, w_ref, mod_ref, fg_ref, y_ref, o_ref, ybuf_ref, sem, *, final):
    blk = pl.program_id(0) * pl.num_programs(1) + pl.program_id(1)
    tb = x_ref.shape[0]
    rp = ybuf_ref.shape[0]

    @pl.when(blk == 0)
    def _():
        ybuf_ref[...] = jnp.zeros_like(ybuf_ref)

    _segment_copies(blk, runs_ref, off_ref, ybuf_ref, y_ref, sem, to_hbm=False, wait=False)
    _segment_copies(blk, runs_ref, off_ref, ybuf_ref, y_ref, sem, to_hbm=False, wait=True)
    li = lax.broadcasted_iota(jnp.int32, (tb, rp), 1)
    wmat = (jnp.where(li == pos_ref[:, 0:1], w_ref[:, 0:1], 0.0)
            + jnp.where(li == pos_ref[:, 1:2], w_ref[:, 1:2], 0.0))
    moe = jnp.dot(wmat.astype(BF16), ybuf_ref[...], preferred_element_type=F32)
    xo = x_ref[...] + mod_ref[5:6, :] * moe
    if final:
        xo = _rms(xo, fg_ref[...])
    o_ref[...] = xo


def _combine(runs, offsets, xn, pos_col, wts_col, mod_l, fg, y, *, tb, final):
    b, s, d = xn.shape
    ns = s // tb
    rp = 2 * tb + SEG_ALIGN * N_EXPERTS
    kern = functools.partial(_combine_kernel, final=final)
    return pl.pallas_call(
        kern,
        out_shape=jax.ShapeDtypeStruct((b, s, d), F32),
        grid_spec=pltpu.PrefetchScalarGridSpec(
            num_scalar_prefetch=2,
            grid=(b, ns),
            in_specs=[
                pl.BlockSpec((None, tb, d), lambda bi, i, *_: (bi, i, 0)),
                pl.BlockSpec((tb, 8), lambda bi, i, *_: (bi * ns + i, 0)),
                pl.BlockSpec((tb, 8), lambda bi, i, *_: (bi * ns + i, 0)),
                pl.BlockSpec((None, 6, d), lambda bi, i, *_: (bi, 0, 0)),
                pl.BlockSpec((1, d), lambda bi, i, *_: (0, 0)),
                pl.BlockSpec(memory_space=pl.ANY),
            ],
            out_specs=pl.BlockSpec((None, tb, d), lambda bi, i, *_: (bi, i, 0)),
            scratch_shapes=[
                pltpu.VMEM((rp, d), BF16),
                pltpu.SemaphoreType.DMA(()),
            ],
        ),
        compiler_params=_cparams(("arbitrary", "arbitrary"), VMEM_LIMIT),
        name="combine",
    )(runs, offsets, xn, pos_col, wts_col, mod_l, fg, y)


def _rope_tables(seq):
    rows = seq // GRID_W
    row = jnp.repeat(jnp.arange(rows, dtype=F32), GRID_W)
    col = jnp.tile(jnp.arange(GRID_W, dtype=F32), rows)
    half = ATTN_HD // 2
    inv_freq = ROPE_BASE ** (-jnp.arange(0, half, 2, dtype=F32) / half)
    ang_r = row[:, None] * inv_freq
    ang_c = col[:, None] * inv_freq
    cos64 = jnp.concatenate([jnp.cos(ang_r), jnp.cos(ang_r), jnp.cos(ang_c), jnp.cos(ang_c)], axis=-1)
    sin64 = jnp.concatenate([-jnp.sin(ang_r), jnp.sin(ang_r), -jnp.sin(ang_c), jnp.sin(ang_c)], axis=-1)
    return jnp.tile(cos64, (1, LANES // ATTN_HD)), jnp.tile(sin64, (1, LANES // ATTN_HD))


def _plan_tiles(rows, tm, nt):
    tiles = (rows + tm - 1) // tm
    tile_end = jnp.cumsum(tiles)
    tile_start = tile_end - tiles
    offsets = (tile_start * tm).astype(jnp.int32)
    last_tile = jnp.where(tiles > 0, (tile_end - 1) * tm, -1).astype(jnp.int32)
    last_tile = jnp.concatenate([last_tile, tile_end[-1:].astype(jnp.int32)])
    tidx = jnp.arange(nt, dtype=jnp.int32)
    te = jnp.sum((tidx[:, None] >= tile_end[None, :]).astype(jnp.int32), axis=1)
    used = (tidx < tile_end[-1]).astype(jnp.int32)
    last_used = jnp.sum((tile_end[-1] - 1 >= tile_end).astype(jnp.int32))
    te = jnp.where(used == 1, te, last_used).astype(jnp.int32)
    prev = jnp.concatenate([jnp.full((1,), -1, jnp.int32), te[:-1]])
    first = ((te != prev) & (used == 1)).astype(jnp.int32)
    return offsets, last_tile, te, first, used


def _tiles(s, t):
    tm = 512 if s % 512 == 0 else 128
    tq = 256 if s % 256 == 0 else 128
    tk = 512 if s % 2048 == 0 else 128
    tme = 256
    tbr = 512 if s % 512 == 0 else 128
    nblk = t // tbr
    nt = -(-(2 * t + (SEG_ALIGN - 1) * N_EXPERTS * nblk) // tme) + N_EXPERTS
    return tm, tq, tk, tme, tbr, nt


def kernel(x, c, w_ada, b_ada, norm1_g, w_in, b_in, conv_w, conv_b, mlstm_norm_g, q_norm_g, k_norm_g,
           w_branch_m, w_branch_a, w_out, norm2_g, w_router_group, b_router_group, w_router_expert,
           b_router_expert, w_gate, w_up, w_down, final_norm_g):
    b, s, d = x.shape
    depth = w_ada.shape[0]
    t = b * s
    nh = MLSTM_HEADS
    tm, tq, tk, tme, tbr, nt = _tiles(s, t)
    p_rows = nt * tme

    mod = _adaln(c, w_ada, b_ada).reshape(depth, b, 6, d)
    cos_t, sin_t = _rope_tables(s)
    seg = np.arange(LANES) // ATTN_HD
    mseg = jnp.asarray((seg[:, None] == seg[None, :]).astype(np.float32) / ATTN_HD)
    mseg = jnp.concatenate([mseg, mseg], axis=0).astype(BF16)

    o_g = 4 * MLSTM_W
    o_aq = o_g + 4 * nh

    for l in range(depth):
        w_l = w_in[l]
        w_main = jnp.concatenate([w_l[:, :o_g], w_l[:, o_aq:]], axis=1).astype(BF16)
        b_main = jnp.concatenate([b_in[l][:o_g], b_in[l][o_aq:]])[None, :]
        wg = jnp.transpose(w_l[:, o_g:o_aq].reshape(d, 4, nh), (2, 1, 0))
        wgt = jnp.concatenate([wg, jnp.zeros((nh, 4, d), F32)], axis=1).reshape(nh * 8, d).astype(BF16)
        bg = jnp.transpose(b_in[l][o_g:o_aq].reshape(4, nh))
        bgt = jnp.concatenate([bg, jnp.zeros((nh, 4), F32)], axis=1).reshape(nh * 8, 1)

        pm, pa, gg, gt = _in_proj(x, mod[l], norm1_g[l][None, :], w_main, b_main, wgt, bgt, tm=tm)
        ym = _mlstm(pm, gt, conv_w[l].reshape(CONV_W, 2 * MLSTM_W), conv_b[l][None, :],
                    mlstm_norm_g[l][None, :])
        qg = jnp.tile(q_norm_g[l], LANES // ATTN_HD)[None, :]
        kg = jnp.tile(k_norm_g[l], LANES // ATTN_HD)[None, :]
        qp, kp, vp = _attn_prep(pa, cos_t, sin_t, qg, kg, mseg, tm=tm)
        ya = _attention(qp, kp, vp, tq=tq, tk=tk)

        wr = jnp.concatenate([w_router_group[l], jnp.zeros((d, 8 - N_GROUPS), F32), w_router_expert[l]], axis=1)
        br = jnp.concatenate([b_router_group[l], jnp.zeros((8 - N_GROUPS,), F32), b_router_expert[l]])
        xn, h2, lt = _merge(x, ym, ya, gg, mod[l], w_branch_m[l].astype(BF16), w_branch_a[l].astype(BF16),
                            w_out[l].astype(BF16), norm2_g[l][None, :], jnp.transpose(wr), br[:, None], tm=tm)

        pos, wts, runs = _route(lt, tb=tbr)
        runs = runs[:, 0, :2 * N_EXPERTS]
        rows = runs[-1, :N_EXPERTS] + runs[-1, N_EXPERTS:]
        offsets, last_tile, te, first, used = _plan_tiles(rows, tme, nt)
        runs = runs.reshape(-1)
        xg = _dispatch(runs, offsets, last_tile, h2, pos, p_rows, tb=tbr, tme=tme)
        y = _experts(te, first, used, xg, w_gate, w_up, w_down, layer=l, tm=tme)
        x = _combine(runs, offsets, xn, jnp.transpose(pos), jnp.transpose(wts), mod[l], final_norm_g[None, :], y,
                     tb=tbr, final=(l == depth - 1))
    return x
```

```python
import functools
import math

import jax
import jax.numpy as jnp
import numpy as np
from jax import lax
from jax.experimental import pallas as pl
from jax.experimental.pallas import tpu as pltpu

F32 = jnp.float32
BF16 = jnp.bfloat16
HIGHEST = lax.Precision.HIGHEST

GRID_W = 64
MLSTM_HEADS = 4
MLSTM_HD = 128
MLSTM_W = MLSTM_HEADS * MLSTM_HD
CONV_W = 5
ATTN_HEADS = 8
KV_HEADS = 2
ATTN_HD = 64
ATTN_W = ATTN_HEADS * ATTN_HD
KV_W = KV_HEADS * ATTN_HD
ROPE_BASE = 10000.0
N_GROUPS = 4
EXPERTS_PER_GROUP = 8
N_EXPERTS = N_GROUPS * EXPERTS_PER_GROUP
NORM_EPS = 1e-6

LANES = 128
MLSTM_CHUNK = 128
SEG_ALIGN = 16
SEG_SHIFT = 4
PV_ROWS = ATTN_HD + 16
VMEM_LIMIT = 56 * 1024 * 1024

PM_W = 4 * MLSTM_W
PA_W = ATTN_W + 2 * KV_W
GG_W = None
ROUTER_ROWS = 8 + N_EXPERTS


def _cparams(sem, vmem=None):
    return pltpu.CompilerParams(dimension_semantics=sem, vmem_limit_bytes=vmem)


def _rms(x, g):
    ms = jnp.mean(x * x, axis=-1, keepdims=True)
    return x * lax.rsqrt(ms + NORM_EPS) * g


def _adaln_kernel(c_ref, w_ref, b_ref, o_ref):
    c = c_ref[...]
    cond = c * jax.nn.sigmoid(c)
    o_ref[0] = jnp.dot(cond, w_ref[0], precision=HIGHEST, preferred_element_type=F32) + b_ref[0]


def _adaln(c, w_ada, b_ada):
    depth, d, n = w_ada.shape
    b = c.shape[0]
    tn = 1536 if n % 1536 == 0 else n
    return pl.pallas_call(
        _adaln_kernel,
        out_shape=jax.ShapeDtypeStruct((depth, b, n), F32),
        grid=(depth, n // tn),
        in_specs=[
            pl.BlockSpec((b, d), lambda l, j: (0, 0)),
            pl.BlockSpec((1, d, tn), lambda l, j: (l, 0, j)),
            pl.BlockSpec((1, 1, tn), lambda l, j: (l, 0, j)),
        ],
        out_specs=pl.BlockSpec((1, b, tn), lambda l, j: (l, 0, j)),
        compiler_params=_cparams(("parallel", "parallel"), VMEM_LIMIT),
        name="adaln",
    )(c, w_ada, b_ada.reshape(depth, 1, n))


def _in_proj_kernel(x_ref, mod_ref, g_ref, w_ref, b_ref, wgt_ref, bgt_ref,
                    pm_ref, pa_ref, gg_ref, gt_ref, *, col_chunk):
    x = x_ref[...]
    y = _rms(x, g_ref[...])
    h = y * (1.0 + mod_ref[1:2, :]) + mod_ref[0:1, :]
    hb = h.astype(BF16)
    c0 = 0
    for o_ref in (pm_ref, pa_ref, gg_ref):
        width = o_ref.shape[-1]
        for a in range(0, width, col_chunk):
            e = min(a + col_chunk, width)
            acc = jnp.dot(hb, w_ref[:, c0 + a:c0 + e], preferred_element_type=F32)
            o_ref[:, a:e] = (acc + b_ref[:, c0 + a:c0 + e]).astype(o_ref.dtype)
        c0 += width
    gt = lax.dot_general(wgt_ref[...], hb, (((1,), (1,)), ((), ())), preferred_element_type=F32)
    gt_ref[...] = gt + bgt_ref[...]


def _in_proj(x, mod_l, g1, w_main, b_main, wgt, bgt, *, tm):
    b, s, d = x.shape
    nw = w_main.shape[1]
    gg_w = nw - PM_W - PA_W
    grows = wgt.shape[0]
    kern = functools.partial(_in_proj_kernel, col_chunk=512)
    return pl.pallas_call(
        kern,
        out_shape=(
            jax.ShapeDtypeStruct((b, s, PM_W), BF16),
            jax.ShapeDtypeStruct((b, s, PA_W), BF16),
            jax.ShapeDtypeStruct((b, s, gg_w), BF16),
            jax.ShapeDtypeStruct((b, grows, s), F32),
        ),
        grid=(b, s // tm),
        in_specs=[
            pl.BlockSpec((None, tm, d), lambda bi, i: (bi, i, 0)),
            pl.BlockSpec((None, 6, d), lambda bi, i: (bi, 0, 0)),
            pl.BlockSpec((1, d), lambda bi, i: (0, 0)),
            pl.BlockSpec((d, nw), lambda bi, i: (0, 0)),
            pl.BlockSpec((1, nw), lambda bi, i: (0, 0)),
            pl.BlockSpec((grows, d), lambda bi, i: (0, 0)),
            pl.BlockSpec((grows, 1), lambda bi, i: (0, 0)),
        ],
        out_specs=(
            pl.BlockSpec((None, tm, PM_W), lambda bi, i: (bi, i, 0)),
            pl.BlockSpec((None, tm, PA_W), lambda bi, i: (bi, i, 0)),
            pl.BlockSpec((None, tm, gg_w), lambda bi, i: (bi, i, 0)),
            pl.BlockSpec((None, grows, tm), lambda bi, i: (bi, 0, i)),
        ),
        compiler_params=_cparams(("parallel", "parallel"), VMEM_LIMIT),
        name="in_proj",
    )(x, mod_l, g1, w_main, b_main, wgt, bgt)


def _mlstm_kernel(q_ref, k_ref, v_ref, o_ref, gt_ref, cwq_ref, cwk_ref, cbq_ref, cbk_ref, ng_ref,
                  y_ref, xs_ref, qc_ref, kt_ref, tab_ref, col_ref, hs_ref, stf_ref, stb_ref, mf_ref, mb_ref,
                  *, seq, blk):
    L = MLSTM_CHUNK
    nc = seq // L
    half = nc // 2
    pad = 8

    zeros_pad = jnp.zeros((pad, LANES), F32)
    xs_ref[0:pad, :] = zeros_pad
    xs_ref[pad + seq:pad + seq + pad, :] = zeros_pad

    def conv_pass(src_ref, w_ref, b_ref, dst_ref, scale, transposed):
        for r0 in range(0, seq, blk):
            xs_ref[pad + r0:pad + r0 + blk, :] = src_ref[r0:r0 + blk, :].astype(F32)
        for r0 in range(0, seq, blk):
            acc = jnp.zeros((blk, LANES), F32) + b_ref[...]
            for j in range(CONV_W):
                off = pad + r0 + j - CONV_W // 2
                acc = acc + w_ref[j:j + 1, :] * xs_ref[off:off + blk, :]
            yv = acc * jax.nn.sigmoid(acc) * scale
            if transposed:
                dst_ref[:, r0:r0 + blk] = jnp.transpose(yv).astype(BF16)
            else:
                dst_ref[r0:r0 + blk, :] = yv.astype(BF16)

    conv_pass(q_ref, cwq_ref, cbq_ref, qc_ref, MLSTM_HD ** -0.5, False)
    conv_pass(k_ref, cwk_ref, cbk_ref, kt_ref, 1.0, True)

    jj = lax.broadcasted_iota(jnp.int32, (L, L), 0)
    ll = lax.broadcasted_iota(jnp.int32, (L, L), 1)
    tri = jnp.concatenate([jnp.where(jj <= ll, 1.0, 0.0), jnp.where(jj >= ll, 1.0, 0.0)], axis=1).astype(BF16)

    def split3(x):
        hi = x.astype(BF16)
        r1 = x - hi.astype(F32)
        mid = r1.astype(BF16)
        lo = (r1 - mid.astype(F32)).astype(BF16)
        return jnp.concatenate([hi, mid, lo], axis=0)

    gate_unroll = 4 if nc % 4 == 0 else 1

    def gate_body(i, carry):
        for u in range(gate_unroll):
            t0 = pl.multiple_of((i * gate_unroll + u) * L, L)
            g = gt_ref[:, pl.ds(t0, L)]
            lf = jax.nn.log_sigmoid(g)
            c3 = jnp.dot(split3(lf), tri, preferred_element_type=F32)
            c2 = c3[0:8] + c3[8:16] + c3[16:24]
            cf, cb = c2[:, 0:L], c2[:, L:2 * L]
            row = lax.broadcasted_iota(jnp.int32, (8, L), 0)
            tab = jnp.where(row == 0, g, jnp.where(row == 1, cf, jnp.where(row == 2, g, cb)))
            tab = jnp.where(row < 4, tab, 0.0)
            tab_ref[:, pl.ds(t0, L)] = tab
            colf = jnp.transpose(jnp.concatenate([tab, jnp.zeros((L - 8, L), F32)], axis=0))
            hi = colf.astype(BF16).astype(F32)
            r1 = colf - hi
            mid = r1.astype(BF16).astype(F32)
            lo = (r1 - mid).astype(BF16).astype(F32)
            col_ref[pl.ds(t0, L), :] = (hi + pltpu.roll(mid, 8, axis=1) + pltpu.roll(lo, 16, axis=1)).astype(BF16)
        return carry

    lax.fori_loop(0, nc // gate_unroll, gate_body, 0)

    srow = lax.broadcasted_iota(jnp.int32, (L, 4 * L), 0)
    scol = lax.broadcasted_iota(jnp.int32, (L, 4 * L), 1) // L
    sel3 = jnp.where((srow < 24) & (srow % 8 == scol), 1.0, 0.0).astype(BF16)

    stf_ref[...] = jnp.zeros_like(stf_ref)
    stb_ref[...] = jnp.zeros_like(stb_ref)
    mf_ref[...] = jnp.zeros_like(mf_ref)
    mb_ref[...] = jnp.zeros_like(mb_ref)
    ones_blk = jnp.ones((L, L), BF16)

    def stage_a(t0, fwd):
        c = dict(t0=t0, fwd=fwd)
        c["q"] = qc_ref[pl.ds(t0, L), :]
        c["kt"] = kt_ref[:, pl.ds(t0, L)]
        c["v"] = v_ref[pl.ds(t0, L), :]
        c["cols"] = jnp.dot(col_ref[pl.ds(t0, L), :], sel3, preferred_element_type=F32)
        c["qk"] = jnp.dot(c["q"], c["kt"], preferred_element_type=F32)
        return c

    def stage_b(c):
        r8 = tab_ref[:, pl.ds(c["t0"], L)]
        cols = c["cols"]
        if c["fwd"]:
            li_c, b_c = cols[:, 0:L], cols[:, L:2 * L]
            li_r, b_r = r8[0:1, :], r8[1:2, :]
            bl = b_c[L - 1:L, :]
            mask = ll <= jj
        else:
            li_c, b_c = cols[:, 2 * L:3 * L], cols[:, 3 * L:4 * L]
            li_r, b_r = r8[2:3, :], r8[3:4, :]
            bl = b_c[0:1, :]
            mask = ll >= jj
        a_c = bl - b_c + li_c
        a_max = jnp.max(a_c, axis=0, keepdims=True)
        w_c = jnp.exp(a_c - a_max)
        vw = (c["v"].astype(F32) * w_c).astype(BF16)
        c["cat"] = jnp.concatenate([vw, w_c.astype(BF16)], axis=1)
        d = jnp.where(mask, b_c + (li_r - b_r), -jnp.inf)
        c.update(b_c=b_c, bl=bl, a_max=a_max, d=d, dmax=jnp.max(d, axis=1, keepdims=True))
        del c["cols"]

    def stage_c(c):
        c["chunk2"] = jnp.dot(c["kt"], c.pop("cat"), preferred_element_type=F32)

    def stage_d(c, st_ref, m_ref):
        m_prev = m_ref[...]
        c["st"] = st_ref[...]
        m_inter = c["b_c"] + m_prev
        m_out = jnp.maximum(m_inter, c["dmax"])
        c["inter_w"] = jnp.exp(m_inter - m_out)
        c["s"] = (c.pop("qk") * jnp.exp(c["d"] - m_out)).astype(BF16)
        c["m_out"] = m_out
        m_new = jnp.maximum(c["bl"] + m_prev, c["a_max"])
        c["decay"] = jnp.exp(c["bl"] + m_prev - m_new)
        c["inject"] = jnp.exp(c["a_max"] - m_new)
        m_ref[...] = m_new

    def stage_e(c):
        vcat = jnp.concatenate([c["v"], ones_blk], axis=1)
        c["r2"] = jnp.dot(c["s"], vcat, preferred_element_type=F32)
        c["q2"] = jnp.dot(c["q"], c["st"].astype(BF16), preferred_element_type=F32)

    def stage_f(c, st_ref):
        r2, q2, inter_w = c["r2"], c["q2"], c["inter_w"]
        num = r2[:, 0:L] + inter_w * q2[:, 0:L]
        den = r2[:, L:2 * L] + inter_w * q2[:, L:2 * L]
        st_ref[...] = c["decay"][:, 0:1] * c["st"] + c["inject"][:, 0:1] * c["chunk2"]
        return num / jnp.maximum(jnp.abs(den), jnp.exp(-c["m_out"]))

    def finalize(t0, h):
        yv = _rms(h, ng_ref[...]) * jax.nn.sigmoid(o_ref[pl.ds(t0, L), :].astype(F32))
        y_ref[pl.ds(t0, L), :] = yv.astype(y_ref.dtype)

    unroll = 4 if half % 4 == 0 else 1

    def body(first_chunk, emit):
        chunks = []
        for u in range(unroll):
            cidx = first_chunk + u
            chunks.append((stage_a(pl.multiple_of(cidx * L, L), True),
                           stage_a(pl.multiple_of((nc - 1 - cidx) * L, L), False)))
        for pair in chunks:
            for c in pair:
                stage_b(c)
        for pair in chunks:
            for c in pair:
                stage_c(c)
        for cf, cb in chunks:
            stage_d(cf, stf_ref, mf_ref)
            stage_d(cb, stb_ref, mb_ref)
            stage_e(cf)
            stage_e(cb)
            emit(cf["t0"], stage_f(cf, stf_ref))
            emit(cb["t0"], stage_f(cb, stb_ref))

    def keep(t0, h):
        hs_ref[pl.ds(t0, L), :] = h

    def add_finalize(t0, h):
        finalize(t0, hs_ref[pl.ds(t0, L), :] + h)

    def first_half(i, carry):
        body(i * unroll, keep)
        return carry

    def second_half(i, carry):
        body(half + i * unroll, add_finalize)
        return carry

    lax.fori_loop(0, half // unroll, first_half, 0)
    lax.fori_loop(0, half // unroll, second_half, 0)


def _mlstm(pm, gt, conv_w, conv_b, norm_g):
    b, s, _ = pm.shape
    nh = MLSTM_HEADS
    assert s % (2 * MLSTM_CHUNK) == 0
    blk = 512 if s % 512 == 0 else MLSTM_CHUNK
    kern = functools.partial(_mlstm_kernel, seq=s, blk=blk)
    L = MLSTM_CHUNK
    col = lambda off: (lambda bi, h: (bi, 0, off + h))
    return pl.pallas_call(
        kern,
        out_shape=jax.ShapeDtypeStruct((b, s, MLSTM_W), BF16),
        grid=(b, nh),
        in_specs=[
            pl.BlockSpec((None, s, LANES), col(0)),
            pl.BlockSpec((None, s, LANES), col(nh)),
            pl.BlockSpec((None, s, LANES), col(2 * nh)),
            pl.BlockSpec((None, s, LANES), col(3 * nh)),
            pl.BlockSpec((None, 8, s), lambda bi, h: (bi, h, 0)),
            pl.BlockSpec((CONV_W, LANES), lambda bi, h: (0, h)),
            pl.BlockSpec((CONV_W, LANES), lambda bi, h: (0, nh + h)),
            pl.BlockSpec((1, LANES), lambda bi, h: (0, h)),
            pl.BlockSpec((1, LANES), lambda bi, h: (0, nh + h)),
            pl.BlockSpec((1, LANES), lambda bi, h: (0, h)),
        ],
        out_specs=pl.BlockSpec((None, s, LANES), lambda bi, h: (bi, 0, h)),
        scratch_shapes=[
            pltpu.VMEM((s + 16, LANES), F32),
            pltpu.VMEM((s, LANES), BF16),
            pltpu.VMEM((LANES, s), BF16),
            pltpu.VMEM((8, s), F32),
            pltpu.VMEM((s, LANES), BF16),
            pltpu.VMEM((s, LANES), F32),
            pltpu.VMEM((L, 2 * L), F32),
            pltpu.VMEM((L, 2 * L), F32),
            pltpu.VMEM((1, L), F32),
            pltpu.VMEM((1, L), F32),
        ],
        compiler_params=_cparams(("parallel", "parallel"), VMEM_LIMIT),
        name="mlstm",
    )(pm, pm, pm, pm, gt, conv_w, conv_w, conv_b, conv_b, norm_g)


def _swap16(y, lane):
    fwd = pltpu.roll(y, LANES - 16, axis=1)
    bwd = pltpu.roll(y, 16, axis=1)
    return jnp.where((lane % 32) < 16, fwd, bwd)


def _nt(a, b):
    return lax.dot_general(a, b, (((1,), (1,)), ((), ())), preferred_element_type=F32)


def _attn_prep_kernel(pa_ref, cos_ref, sin_ref, qg_ref, kg_ref, mseg_ref, qt_ref, kp_ref, vt_ref, *, q_scale):
    tm = pa_ref.shape[0]
    lane = lax.broadcasted_iota(jnp.int32, (tm, LANES), 1)
    cos = cos_ref[...]
    sin = sin_ref[...]
    mseg = mseg_ref[...]
    r = lax.broadcasted_iota(jnp.int32, (LANES, LANES), 0)
    c = lax.broadcasted_iota(jnp.int32, (LANES, LANES), 1)
    sel_lo = jnp.where((r == c) & (r < ATTN_HD), 1.0, 0.0).astype(BF16)
    sel_hi = jnp.where((c == r + ATTN_HD) & (r < ATTN_HD), 1.0, 0.0).astype(BF16)
    ones_row = jnp.where(lax.broadcasted_iota(jnp.int32, (LANES, tm), 0) == ATTN_HD, 1.0, 0.0)

    def norm_rope(x, g, scale):
        xx = x * x
        hi = xx.astype(BF16)
        lo = (xx - hi.astype(F32)).astype(BF16)
        ms = jnp.dot(jnp.concatenate([hi, lo], axis=1), mseg, preferred_element_type=F32)
        y = x * lax.rsqrt(ms + NORM_EPS) * g
        return (y * cos + _swap16(y, lane) * sin) * scale

    for j in range(ATTN_W // LANES):
        x = pa_ref[:, j * LANES:(j + 1) * LANES].astype(F32)
        y = norm_rope(x, qg_ref[...], q_scale).astype(BF16)
        qt_ref[2 * j] = _nt(sel_lo, y).astype(BF16)
        qt_ref[2 * j + 1] = _nt(sel_hi, y).astype(BF16)
    for j in range(KV_W // LANES):
        x = pa_ref[:, ATTN_W + j * LANES:ATTN_W + (j + 1) * LANES].astype(F32)
        y = norm_rope(x, kg_ref[...], 1.0)
        kp_ref[2 * j] = jnp.where(lane < ATTN_HD, y, 0.0).astype(BF16)
        kp_ref[2 * j + 1] = jnp.where(lane < ATTN_HD, pltpu.roll(y, ATTN_HD, axis=1), 0.0).astype(BF16)
        xv = pa_ref[:, ATTN_W + KV_W + j * LANES:ATTN_W + KV_W + (j + 1) * LANES]
        vt_ref[2 * j] = (_nt(sel_lo, xv) + ones_row).astype(BF16)
        vt_ref[2 * j + 1] = (_nt(sel_hi, xv) + ones_row).astype(BF16)


def _attn_prep(pa, cos_t, sin_t, qg, kg, mseg, *, tm):
    b, s, _ = pa.shape
    q_scale = (ATTN_HD ** -0.5) * math.log2(math.e)
    kern = functools.partial(_attn_prep_kernel, q_scale=q_scale)
    return pl.pallas_call(
        kern,
        out_shape=(
            jax.ShapeDtypeStruct((b, ATTN_HEADS, LANES, s), BF16),
            jax.ShapeDtypeStruct((b, KV_HEADS, s, LANES), BF16),
            jax.ShapeDtypeStruct((b, KV_HEADS, LANES, s), BF16),
        ),
        grid=(b, s // tm),
        in_specs=[
            pl.BlockSpec((None, tm, PA_W), lambda bi, i: (bi, i, 0)),
            pl.BlockSpec((tm, LANES), lambda bi, i: (i, 0)),
            pl.BlockSpec((tm, LANES), lambda bi, i: (i, 0)),
            pl.BlockSpec((1, LANES), lambda bi, i: (0, 0)),
            pl.BlockSpec((1, LANES), lambda bi, i: (0, 0)),
            pl.BlockSpec((2 * LANES, LANES), lambda bi, i: (0, 0)),
        ],
        out_specs=(
            pl.BlockSpec((None, ATTN_HEADS, LANES, tm), lambda bi, i: (bi, 0, 0, i)),
            pl.BlockSpec((None, KV_HEADS, tm, LANES), lambda bi, i: (bi, 0, i, 0)),
            pl.BlockSpec((None, KV_HEADS, LANES, tm), lambda bi, i: (bi, 0, 0, i)),
        ),
        compiler_params=_cparams(("parallel", "parallel"), VMEM_LIMIT),
        name="attn_prep",
    )(pa, cos_t, sin_t, qg, kg, mseg)


def _attn_kernel(qt_ref, k_ref, vt_ref, o_ref, m_ref, acc_ref, s_ref, *, tk):
    g, _, tq = qt_ref.shape
    s = k_ref.shape[0]
    n = s // tk
    m_ref[...] = jnp.full(m_ref.shape, -jnp.inf, F32)
    acc_ref[...] = jnp.zeros(acc_ref.shape, F32)

    def scores(j, slot, h):
        t0 = pl.multiple_of(j * tk, tk)
        s_ref[slot, :, h * tq:(h + 1) * tq] = jnp.dot(k_ref[pl.ds(t0, tk), :], qt_ref[h],
                                                      preferred_element_type=F32)

    def softmax(slot, h):
        cols = slice(h * tq, (h + 1) * tq)
        sc = s_ref[slot, :, cols]
        m_prev = m_ref[:, cols]
        m_new = jnp.maximum(m_prev, jnp.max(sc, axis=0, keepdims=True))
        m_ref[:, cols] = m_new
        return jnp.exp2(m_prev - m_new), jnp.exp2(sc - m_new).astype(BF16)

    def pv(j, h, alpha, p):
        t0 = pl.multiple_of(j * tk, tk)
        cols = slice(h * tq, (h + 1) * tq)
        acc_ref[:, cols] = alpha * acc_ref[:, cols] + jnp.dot(vt_ref[0:PV_ROWS, pl.ds(t0, tk)], p,
                                                              preferred_element_type=F32)

    def step(j, slot, prefetch):
        pending = None
        for h in range(g):
            if prefetch:
                scores(j + 1, 1 - slot, h)
            current = softmax(slot, h)
            if pending is not None:
                pv(j, h - 1, *pending)
            pending = current
        pv(j, g - 1, *pending)

    for h in range(g):
        scores(0, 0, h)

    unroll = 4 if n % 4 == 0 else 2

    def body(i, carry):
        for u in range(unroll):
            step(unroll * i + u, u % 2, True)
        return carry

    lax.fori_loop(0, n // unroll - 1, body, 0)
    for u in range(unroll):
        step(n - unroll + u, u % 2, u < unroll - 1)
    acc = acc_ref[...]
    o = (acc[0:ATTN_HD, :] / acc[ATTN_HD:ATTN_HD + 1, :]).astype(BF16)
    r = lax.broadcasted_iota(jnp.int32, (LANES, LANES), 0)
    c = lax.broadcasted_iota(jnp.int32, (LANES, LANES), 1)
    eye = jnp.where(r == c, 1.0, 0.0).astype(BF16)
    for pair in range(g // 2):
        rows = jnp.concatenate([o[:, (2 * pair) * tq:(2 * pair + 1) * tq],
                                o[:, (2 * pair + 1) * tq:(2 * pair + 2) * tq]], axis=0)
        out = lax.dot_general(rows, eye, (((0,), (0,)), ((), ())), preferred_element_type=F32)
        o_ref[:, pair * LANES:(pair + 1) * LANES] = out.astype(o_ref.dtype)


def _attention(qt, kp, vt, *, tq, tk):
    b, nh, _, s = qt.shape
    g = nh // KV_HEADS
    assert (s // tk) % 2 == 0 and g % 2 == 0
    kern = functools.partial(_attn_kernel, tk=tk)
    return pl.pallas_call(
        kern,
        out_shape=jax.ShapeDtypeStruct((b, s, ATTN_W), BF16),
        grid=(b, KV_HEADS, s // tq),
        in_specs=[
            pl.BlockSpec((None, g, LANES, tq), lambda bi, kv, i: (bi, kv, 0, i)),
            pl.BlockSpec((None, None, s, LANES), lambda bi, kv, i: (bi, kv, 0, 0)),
            pl.BlockSpec((None, None, LANES, s), lambda bi, kv, i: (bi, kv, 0, 0)),
        ],
        out_specs=pl.BlockSpec((None, tq, g * ATTN_HD), lambda bi, kv, i: (bi, i, kv)),
        scratch_shapes=[
            pltpu.VMEM((1, g * tq), F32),
            pltpu.VMEM((PV_ROWS, g * tq), F32),
            pltpu.VMEM((2, tk, g * tq), F32),
        ],
        compiler_params=_cparams(("parallel", "parallel", "parallel"), VMEM_LIMIT),
        name="attn",
    )(qt, kp, vt)


def _merge_kernel(x_ref, ym_ref, ya_ref, gg_ref, mod_ref, wbm_ref, wba_ref, wo_ref, g2_ref, wr_ref, br_ref,
                  xn_ref, h2_ref, lt_ref):
    tm, d = x_ref.shape
    sub = tm
    for r0 in range(0, tm, sub):
        rows = slice(r0, r0 + sub)
        a = jnp.dot(ym_ref[rows, :], wbm_ref[...], preferred_element_type=F32)
        bm = jnp.dot(ya_ref[rows, :], wba_ref[...], preferred_element_type=F32)
        gm = jax.nn.sigmoid(gg_ref[rows, 0:d].astype(F32))
        ga = jax.nn.sigmoid(gg_ref[rows, d:2 * d].astype(F32))
        merged = (gm * a + ga * bm).astype(BF16)
        u = jnp.dot(merged, wo_ref[...], preferred_element_type=F32)
        xn = x_ref[rows, :] + mod_ref[2:3, :] * u
        xn_ref[rows, :] = xn
        h2 = _rms(xn, g2_ref[...]) * (1.0 + mod_ref[4:5, :]) + mod_ref[3:4, :]
        h2_ref[rows, :] = h2.astype(h2_ref.dtype)
        lt = lax.dot_general(wr_ref[...], h2, (((1,), (1,)), ((), ())), precision=HIGHEST,
                             preferred_element_type=F32)
        lt_ref[:, rows] = lt + br_ref[...]


def _merge(x, ym, ya, gg, mod_l, wbm, wba, wo, g2, wr_t, br_t, *, tm):
    b, s, d = x.shape
    ns = s // tm
    rr = wr_t.shape[0]
    full = lambda shp: pl.BlockSpec(shp, lambda bi, i: tuple(0 for _ in shp))
    return pl.pallas_call(
        _merge_kernel,
        out_shape=(
            jax.ShapeDtypeStruct((b, s, d), F32),
            jax.ShapeDtypeStruct((b * s, d), BF16),
            jax.ShapeDtypeStruct((rr, b * s), F32),
        ),
        grid=(b, ns),
        in_specs=[
            pl.BlockSpec((None, tm, d), lambda bi, i: (bi, i, 0)),
            pl.BlockSpec((None, tm, MLSTM_W), lambda bi, i: (bi, i, 0)),
            pl.BlockSpec((None, tm, ATTN_W), lambda bi, i: (bi, i, 0)),
            pl.BlockSpec((None, tm, 2 * d), lambda bi, i: (bi, i, 0)),
            pl.BlockSpec((None, 6, d), lambda bi, i: (bi, 0, 0)),
            full((MLSTM_W, d)),
            full((ATTN_W, d)),
            full((d, d)),
            full((1, d)),
            full((rr, d)),
            full((rr, 1)),
        ],
        out_specs=(
            pl.BlockSpec((None, tm, d), lambda bi, i: (bi, i, 0)),
            pl.BlockSpec((tm, d), lambda bi, i: (bi * ns + i, 0)),
            pl.BlockSpec((rr, tm), lambda bi, i: (0, bi * ns + i)),
        ),
        compiler_params=_cparams(("parallel", "parallel"), VMEM_LIMIT),
        name="merge",
    )(x, ym, ya, gg, mod_l, wbm, wba, wo, g2, wr_t, br_t)


def _route_kernel(lt_ref, pos_ref, wts_ref, runs_ref, carry_ref):
    tb = lt_ref.shape[1]
    epg = EXPERTS_PER_GROUP

    @pl.when(pl.program_id(0) == 0)
    def _():
        carry_ref[...] = jnp.zeros_like(carry_ref)

    row8 = lax.broadcasted_iota(jnp.int32, (8, tb), 0)
    gl = jnp.where(row8 < N_GROUPS, lt_ref[0:8, :], -jnp.inf)
    ge = jnp.exp(gl - jnp.max(gl, axis=0, keepdims=True))
    pg = ge / jnp.sum(ge, axis=0, keepdims=True)
    p_top = jnp.max(pg, axis=0, keepdims=True)
    g_idx = jnp.min(jnp.where(pg == p_top, row8, 8), axis=0, keepdims=True)

    el = jnp.zeros((epg, tb), F32)
    for g in range(N_GROUPS):
        el = jnp.where(g_idx == g, lt_ref[8 + g * epg:8 + (g + 1) * epg, :], el)
    ee = jnp.exp(el - jnp.max(el, axis=0, keepdims=True))
    pe = ee / jnp.sum(ee, axis=0, keepdims=True)
    v1 = jnp.max(pe, axis=0, keepdims=True)
    i1 = jnp.min(jnp.where(pe == v1, row8, 8), axis=0, keepdims=True)
    pe2 = jnp.where(row8 == i1, -1.0, pe)
    v2 = jnp.max(pe2, axis=0, keepdims=True)
    i2 = jnp.min(jnp.where(pe2 == v2, row8, 8), axis=0, keepdims=True)
    denom = v1 + v2
    w0 = v1 / denom * p_top
    w1 = v2 / denom * p_top
    e0 = g_idx * epg + i1
    e1 = g_idx * epg + i2

    rowe = lax.broadcasted_iota(jnp.int32, (N_EXPERTS, tb), 0)
    oh0 = rowe == e0
    oh1 = rowe == e1
    oh = jnp.where(oh0 | oh1, 1.0, 0.0)
    src = lax.broadcasted_iota(jnp.int32, (tb, tb), 0)
    dst = lax.broadcasted_iota(jnp.int32, (tb, tb), 1)
    strict = jnp.where(src < dst, 1.0, 0.0).astype(BF16)
    cum = jnp.dot(oh.astype(BF16), strict, preferred_element_type=F32)
    cnt_col = jnp.sum(oh, axis=1, keepdims=True)
    seg_col = jnp.floor((cnt_col + (SEG_ALIGN - 1.0)) * (1.0 / SEG_ALIGN)) * SEG_ALIGN
    er = lax.broadcasted_iota(jnp.int32, (N_EXPERTS, N_EXPERTS), 0)
    ec = lax.broadcasted_iota(jnp.int32, (N_EXPERTS, N_EXPERTS), 1)
    before = jnp.where(ec < er, 1.0, 0.0).astype(BF16)
    start_col = jnp.dot(before, jnp.broadcast_to(seg_col, (N_EXPERTS, LANES)).astype(BF16),
                        preferred_element_type=F32)[:, 0:1]
    base = start_col + cum
    p0 = jnp.sum(jnp.where(oh0, base, 0.0), axis=0, keepdims=True)
    p1 = jnp.sum(jnp.where(oh1, base, 0.0), axis=0, keepdims=True)
    pos_ref[...] = jnp.where(row8 == 0, p0, jnp.where(row8 == 1, p1, 0.0)).astype(jnp.int32)
    wts_ref[...] = jnp.where(row8 == 0, w0, jnp.where(row8 == 1, w1, 0.0))

    ohp = jnp.concatenate([oh, jnp.zeros((LANES - N_EXPERTS, tb), F32)], axis=0).astype(BF16)
    cnt_row = _nt(jnp.ones((8, tb), BF16), ohp)
    seg_row = jnp.floor((cnt_row + (SEG_ALIGN - 1.0)) * (1.0 / SEG_ALIGN)) * SEG_ALIGN
    carry = carry_ref[...]
    runs_ref[0] = (seg_row + pltpu.roll(carry, N_EXPERTS, axis=1)).astype(jnp.int32)
    carry_ref[...] = carry + seg_row


def _route(lt, *, tb):
    rr, t = lt.shape
    nblk = t // tb
    return pl.pallas_call(
        _route_kernel,
        out_shape=(
            jax.ShapeDtypeStruct((8, t), jnp.int32),
            jax.ShapeDtypeStruct((8, t), F32),
            jax.ShapeDtypeStruct((nblk, 8, LANES), jnp.int32),
        ),
        grid=(nblk,),
        in_specs=[pl.BlockSpec((rr, tb), lambda i: (0, i))],
        out_specs=(
            pl.BlockSpec((8, tb), lambda i: (0, i)),
            pl.BlockSpec((8, tb), lambda i: (0, i)),
            pl.BlockSpec((1, 8, LANES), lambda i: (i, 0, 0)),
        ),
        scratch_shapes=[pltpu.VMEM((8, LANES), F32)],
        compiler_params=_cparams(("arbitrary",), VMEM_LIMIT),
        name="route",
    )(lt)


def _segment_copies(blk, runs_ref, off_ref, local_ref, hbm_ref, sem, *, to_hbm, wait):
    big = 2 * SEG_ALIGN

    def piece(s, d, rows):
        s = pl.multiple_of(s, SEG_ALIGN)
        d = pl.multiple_of(d, SEG_ALIGN)
        loc = local_ref.at[pl.ds(s, rows)]
        far = hbm_ref.at[pl.ds(d, rows)]
        cp = pltpu.make_async_copy(loc, far, sem) if to_hbm else pltpu.make_async_copy(far, loc, sem)
        if wait:
            cp.wait()
        else:
            cp.start()

    def expert(e, s0):
        n = runs_ref[blk * 2 * N_EXPERTS + e]
        d0 = off_ref[e] + runs_ref[blk * 2 * N_EXPERTS + N_EXPERTS + e]
        nbig = n >> (SEG_SHIFT + 1)
        nsmall = (n >> SEG_SHIFT) & 1

        def big_piece(c, carry):
            piece(s0 + c * big, d0 + c * big, big)
            return carry

        def small_piece(c, carry):
            piece(s0 + nbig * big + c * SEG_ALIGN, d0 + nbig * big + c * SEG_ALIGN, SEG_ALIGN)
            return carry

        lax.fori_loop(0, nbig, big_piece, 0)
        lax.fori_loop(0, nsmall, small_piece, 0)
        return s0 + n

    lax.fori_loop(0, N_EXPERTS, expert, jnp.int32(0))


def _dispatch_kernel(runs_ref, off_ref, last_ref, h_ref, pos_ref, xg_ref, sbuf_ref, zero_ref, sem, zsem, *, tme):
    i = pl.program_id(0)
    tb = h_ref.shape[0]
    rp = sbuf_ref.shape[0]

    @pl.when(i == 0)
    def _():
        zero_ref[...] = jnp.zeros_like(zero_ref)
        for wait in (False, True):
            def tail_tile(j, carry, wait=wait):
                cp = pltpu.make_async_copy(zero_ref, xg_ref.at[pl.ds(pl.multiple_of(j * tme, tme), tme)], zsem)
                if wait:
                    cp.wait()
                else:
                    cp.start()
                return carry

            lax.fori_loop(last_ref[N_EXPERTS], xg_ref.shape[0] // tme, tail_tile, 0)
            for e in range(N_EXPERTS):
                @pl.when(last_ref[e] >= 0)
                def _():
                    cp = pltpu.make_async_copy(
                        zero_ref, xg_ref.at[pl.ds(pl.multiple_of(last_ref[e], SEG_ALIGN), tme)], zsem)
                    if wait:
                        cp.wait()
                    else:
                        cp.start()

    ii = lax.broadcasted_iota(jnp.int32, (rp, tb), 0)
    perm = jnp.where((ii == pos_ref[0:1, :]) | (ii == pos_ref[1:2, :]), 1.0, 0.0).astype(BF16)
    sbuf_ref[...] = jnp.dot(perm, h_ref[...], preferred_element_type=F32).astype(BF16)
    _segment_copies(i, runs_ref, off_ref, sbuf_ref, xg_ref, sem, to_hbm=True, wait=False)
    _segment_copies(i, runs_ref, off_ref, sbuf_ref, xg_ref, sem, to_hbm=True, wait=True)


def _dispatch(runs, offsets, last_tile, h2, pos, p_rows, *, tb, tme):
    t, d = h2.shape
    rp = 2 * tb + SEG_ALIGN * N_EXPERTS
    kern = functools.partial(_dispatch_kernel, tme=tme)
    return pl.pallas_call(
        kern,
        out_shape=jax.ShapeDtypeStruct((p_rows, d), BF16),
        grid_spec=pltpu.PrefetchScalarGridSpec(
            num_scalar_prefetch=3,
            grid=(t // tb,),
            in_specs=[
                pl.BlockSpec((tb, d), lambda i, *_: (i, 0)),
                pl.BlockSpec((8, tb), lambda i, *_: (0, i)),
            ],
            out_specs=pl.BlockSpec(memory_space=pl.ANY),
            scratch_shapes=[
                pltpu.VMEM((rp, d), BF16),
                pltpu.VMEM((tme, d), BF16),
                pltpu.SemaphoreType.DMA(()),
                pltpu.SemaphoreType.DMA(()),
            ],
        ),
        compiler_params=_cparams(("arbitrary",), VMEM_LIMIT),
        name="dispatch",
    )(runs, offsets, last_tile, h2, pos)


def _experts_kernel(te_ref, first_ref, used_ref, x_ref, wg_ref, wu_ref, wd_ref, y_ref, wgb, wub, wdb):
    i = pl.program_id(0)

    @pl.when(first_ref[i] == 1)
    def _():
        wgb[...] = wg_ref[...].astype(BF16)
        wub[...] = wu_ref[...].astype(BF16)
        wdb[...] = wd_ref[...].astype(BF16)

    @pl.when(used_ref[i] == 1)
    def _():
        tm = x_ref.shape[0]
        sub = tm // 2 if tm % 512 == 0 else tm
        halves = [slice(r, r + sub) for r in range(0, tm, sub)]
        ups = [(jnp.dot(x_ref[rs, :], wgb[...], preferred_element_type=F32),
                jnp.dot(x_ref[rs, :], wub[...], preferred_element_type=F32)) for rs in halves]
        acts = [(a * jax.nn.sigmoid(a) * u).astype(BF16) for a, u in ups]
        for rs, act in zip(halves, acts):
            y_ref[rs, :] = jnp.dot(act, wdb[...], preferred_element_type=F32).astype(y_ref.dtype)

    @pl.when(used_ref[i] == 0)
    def _():
        y_ref[...] = jnp.zeros_like(y_ref)


def _experts(tile_expert, tile_first, tile_used, xg, w_gate, w_up, w_down, *, layer, tm):
    p_rows, d = xg.shape
    f = w_gate.shape[-1]
    nt = p_rows // tm
    wmap = lambda i, te, fi, us: (layer, te[i], 0, 0)
    return pl.pallas_call(
        _experts_kernel,
        out_shape=jax.ShapeDtypeStruct((p_rows, d), BF16),
        grid_spec=pltpu.PrefetchScalarGridSpec(
            num_scalar_prefetch=3,
            grid=(nt,),
            in_specs=[
                pl.BlockSpec((tm, d), lambda i, te, fi, us: (i, 0)),
                pl.BlockSpec((None, None, d, f), wmap),
                pl.BlockSpec((None, None, d, f), wmap),
                pl.BlockSpec((None, None, f, d), wmap),
            ],
            out_specs=pl.BlockSpec((tm, d), lambda i, te, fi, us: (i, 0)),
            scratch_shapes=[
                pltpu.VMEM((d, f), BF16),
                pltpu.VMEM((d, f), BF16),
                pltpu.VMEM((f, d), BF16),
            ],
        ),
        compiler_params=_cparams(("arbitrary",), VMEM_LIMIT),
        name="experts",
    )(tile_expert, tile_first, tile_used, xg, w_gate, w_up, w_down)


def _combine_kernel(runs_ref, off_ref, x_ref, pos_ref, w_ref, mod_ref, fg_ref, y_ref, o_ref, ybuf_ref, sem, *, final):
    blk = pl.program_id(0) * pl.num_programs(1) + pl.program_id(1)
    tb = x_ref.shape[0]
    rp = ybuf_ref.shape[0]

    @pl.when(blk == 0)
    def _():
        ybuf_ref[...] = jnp.zeros_like(ybuf_ref)

    _segment_copies(blk, runs_ref, off_ref, ybuf_ref, y_ref, sem, to_hbm=False, wait=False)
    _segment_copies(blk, runs_ref, off_ref, ybuf_ref, y_ref, sem, to_hbm=False, wait=True)
    li = lax.broadcasted_iota(jnp.int32, (tb, rp), 1)
    wmat = (jnp.where(li == pos_ref[:, 0:1], w_ref[:, 0:1], 0.0)
            + jnp.where(li == pos_ref[:, 1:2], w_ref[:, 1:2], 0.0))
    moe = jnp.dot(wmat.astype(BF16), ybuf_ref[...], preferred_element_type=F32)
    xo = x_ref[...] + mod_ref[5:6, :] * moe
    if final:
        xo = _rms(xo, fg_ref[...])
    o_ref[...] = xo


def _combine(runs, offsets, xn, pos_col, wts_col, mod_l, fg, y, *, tb, final):
    b, s, d = xn.shape
    ns = s // tb
    rp = 2 * tb + SEG_ALIGN * N_EXPERTS
    kern = functools.partial(_combine_kernel, final=final)
    return pl.pallas_call(
        kern,
        out_shape=jax.ShapeDtypeStruct((b, s, d), F32),
        grid_spec=pltpu.PrefetchScalarGridSpec(
            num_scalar_prefetch=2,
            grid=(b, ns),
            in_specs=[
                pl.BlockSpec((None, tb, d), lambda bi, i, *_: (bi, i, 0)),
                pl.BlockSpec((tb, 8), lambda bi, i, *_: (bi * ns + i, 0)),
                pl.BlockSpec((tb, 8), lambda bi, i, *_: (bi * ns + i, 0)),
                pl.BlockSpec((None, 6, d), lambda bi, i, *_: (bi, 0, 0)),
                pl.BlockSpec((1, d), lambda bi, i, *_: (0, 0)),
                pl.BlockSpec(memory_space=pl.ANY),
            ],
            out_specs=pl.BlockSpec((None, tb, d), lambda bi, i, *_: (bi, i, 0)),
            scratch_shapes=[
                pltpu.VMEM((rp, d), BF16),
                pltpu.SemaphoreType.DMA(()),
            ],
        ),
        compiler_params=_cparams(("arbitrary", "arbitrary"), VMEM_LIMIT),
        name="combine",
    )(runs, offsets, xn, pos_col, wts_col, mod_l, fg, y)


def _rope_tables(seq):
    rows = seq // GRID_W
    row = jnp.repeat(jnp.arange(rows, dtype=F32), GRID_W)
    col = jnp.tile(jnp.arange(GRID_W, dtype=F32), rows)
    half = ATTN_HD // 2
    inv_freq = ROPE_BASE ** (-jnp.arange(0, half, 2, dtype=F32) / half)
    ang_r = row[:, None] * inv_freq
    ang_c = col[:, None] * inv_freq
    cos64 = jnp.concatenate([jnp.cos(ang_r), jnp.cos(ang_r), jnp.cos(ang_c), jnp.cos(ang_c)], axis=-1)
    sin64 = jnp.concatenate([-jnp.sin(ang_r), jnp.sin(ang_r), -jnp.sin(ang_c), jnp.sin(ang_c)], axis=-1)
    return jnp.tile(cos64, (1, LANES // ATTN_HD)), jnp.tile(sin64, (1, LANES // ATTN_HD))


def _plan_tiles(rows, tm, nt):
    tiles = (rows + tm - 1) // tm
    tile_end = jnp.cumsum(tiles)
    tile_start = tile_end - tiles
    offsets = (tile_start * tm).astype(jnp.int32)
    last_tile = jnp.where(tiles > 0, (tile_end - 1) * tm, -1).astype(jnp.int32)
    last_tile = jnp.concatenate([last_tile, tile_end[-1:].astype(jnp.int32)])
    tidx = jnp.arange(nt, dtype=jnp.int32)
    te = jnp.sum((tidx[:, None] >= tile_end[None, :]).astype(jnp.int32), axis=1)
    used = (tidx < tile_end[-1]).astype(jnp.int32)
    last_used = jnp.sum((tile_end[-1] - 1 >= tile_end).astype(jnp.int32))
    te = jnp.where(used == 1, te, last_used).astype(jnp.int32)
    prev = jnp.concatenate([jnp.full((1,), -1, jnp.int32), te[:-1]])
    first = ((te != prev) & (used == 1)).astype(jnp.int32)
    return offsets, last_tile, te, first, used


def _tiles(s, t):
    tm = 512 if s % 512 == 0 else 128
    tq = 256 if s % 256 == 0 else 128
    tk = 512 if s % 2048 == 0 else 128
    tme = 512
    tbr = 512 if s % 512 == 0 else 128
    nblk = t // tbr
    nt = -(-(2 * t + (SEG_ALIGN - 1) * N_EXPERTS * nblk) // tme) + N_EXPERTS
    return tm, tq, tk, tme, tbr, nt


def kernel(x, c, w_ada, b_ada, norm1_g, w_in, b_in, conv_w, conv_b, mlstm_norm_g, q_norm_g, k_norm_g,
           w_branch_m, w_branch_a, w_out, norm2_g, w_router_group, b_router_group, w_router_expert,
           b_router_expert, w_gate, w_up, w_down, final_norm_g):
    b, s, d = x.shape
    depth = w_ada.shape[0]
    t = b * s
    nh = MLSTM_HEADS
    tm, tq, tk, tme, tbr, nt = _tiles(s, t)
    p_rows = nt * tme

    mod = _adaln(c, w_ada, b_ada).reshape(depth, b, 6, d)
    cos_t, sin_t = _rope_tables(s)
    seg = np.arange(LANES) // ATTN_HD
    mseg = jnp.asarray((seg[:, None] == seg[None, :]).astype(np.float32) / ATTN_HD)
    mseg = jnp.concatenate([mseg, mseg], axis=0).astype(BF16)

    o_g = 4 * MLSTM_W
    o_aq = o_g + 4 * nh

    for l in range(depth):
        w_l = w_in[l]
        w_main = jnp.concatenate([w_l[:, :o_g], w_l[:, o_aq:]], axis=1).astype(BF16)
        b_main = jnp.concatenate([b_in[l][:o_g], b_in[l][o_aq:]])[None, :]
        wg = jnp.transpose(w_l[:, o_g:o_aq].reshape(d, 4, nh), (2, 1, 0))
        wgt = jnp.concatenate([wg, jnp.zeros((nh, 4, d), F32)], axis=1).reshape(nh * 8, d).astype(BF16)
        bg = jnp.transpose(b_in[l][o_g:o_aq].reshape(4, nh))
        bgt = jnp.concatenate([bg, jnp.zeros((nh, 4), F32)], axis=1).reshape(nh * 8, 1)

        pm, pa, gg, gt = _in_proj(x, mod[l], norm1_g[l][None, :], w_main, b_main, wgt, bgt, tm=tm)
        ym = _mlstm(pm, gt, conv_w[l].reshape(CONV_W, 2 * MLSTM_W), conv_b[l][None, :],
                    mlstm_norm_g[l][None, :])
        qg = jnp.tile(q_norm_g[l], LANES // ATTN_HD)[None, :]
        kg = jnp.tile(k_norm_g[l], LANES // ATTN_HD)[None, :]
        qp, kp, vp = _attn_prep(pa, cos_t, sin_t, qg, kg, mseg, tm=tm)
        ya = _attention(qp, kp, vp, tq=tq, tk=tk)

        wr = jnp.concatenate([w_router_group[l], jnp.zeros((d, 8 - N_GROUPS), F32), w_router_expert[l]], axis=1)
        br = jnp.concatenate([b_router_group[l], jnp.zeros((8 - N_GROUPS,), F32), b_router_expert[l]])
        xn, h2, lt = _merge(x, ym, ya, gg, mod[l], w_branch_m[l].astype(BF16), w_branch_a[l].astype(BF16),
                            w_out[l].astype(BF16), norm2_g[l][None, :], jnp.transpose(wr), br[:, None], tm=tm)

        pos, wts, runs = _route(lt, tb=tbr)
        runs = runs[:, 0, :2 * N_EXPERTS]
        rows = runs[-1, :N_EXPERTS] + runs[-1, N_EXPERTS:]
        offsets, last_tile, te, first, used = _plan_tiles(rows, tme, nt)
        runs = runs.reshape(-1)
        xg = _dispatch(runs, offsets, last_tile, h2, pos, p_rows, tb=tbr, tme=tme)
        y = _experts(te, first, used, xg, w_gate, w_up, w_down, layer=l, tm=tme)
        x = _combine(runs, offsets, xn, jnp.transpose(pos), jnp.transpose(wts), mod[l], final_norm_g[None, :], y,
                     tb=tbr, final=(l == depth - 1))
    return x
```

```python
import functools
import math

import jax
import jax.numpy as jnp
import numpy as np
from jax import lax
from jax.experimental import pallas as pl
from jax.experimental.pallas import tpu as pltpu

F32 = jnp.float32
BF16 = jnp.bfloat16
HIGHEST = lax.Precision.HIGHEST

GRID_W = 64
MLSTM_HEADS = 4
MLSTM_HD = 128
MLSTM_W = MLSTM_HEADS * MLSTM_HD
CONV_W = 5
ATTN_HEADS = 8
KV_HEADS = 2
ATTN_HD = 64
ATTN_W = ATTN_HEADS * ATTN_HD
KV_W = KV_HEADS * ATTN_HD
ROPE_BASE = 10000.0
N_GROUPS = 4
EXPERTS_PER_GROUP = 8
N_EXPERTS = N_GROUPS * EXPERTS_PER_GROUP
NORM_EPS = 1e-6

LANES = 128
MLSTM_CHUNK = 128
SEG_ALIGN = 16
SEG_SHIFT = 4
PV_ROWS = ATTN_HD + 16
VMEM_LIMIT = 56 * 1024 * 1024

PM_W = 4 * MLSTM_W
PA_W = ATTN_W + 2 * KV_W
GG_W = None
ROUTER_ROWS = 8 + N_EXPERTS


def _cparams(sem, vmem=None):
    return pltpu.CompilerParams(dimension_semantics=sem, vmem_limit_bytes=vmem)


def _rms(x, g):
    ms = jnp.mean(x * x, axis=-1, keepdims=True)
    return x * lax.rsqrt(ms + NORM_EPS) * g


def _adaln_kernel(c_ref, w_ref, b_ref, o_ref):
    c = c_ref[...]
    cond = c * jax.nn.sigmoid(c)
    o_ref[0] = jnp.dot(cond, w_ref[0], precision=HIGHEST, preferred_element_type=F32) + b_ref[0]


def _adaln(c, w_ada, b_ada):
    depth, d, n = w_ada.shape
    b = c.shape[0]
    tn = 1536 if n % 1536 == 0 else n
    return pl.pallas_call(
        _adaln_kernel,
        out_shape=jax.ShapeDtypeStruct((depth, b, n), F32),
        grid=(depth, n // tn),
        in_specs=[
            pl.BlockSpec((b, d), lambda l, j: (0, 0)),
            pl.BlockSpec((1, d, tn), lambda l, j: (l, 0, j)),
            pl.BlockSpec((1, 1, tn), lambda l, j: (l, 0, j)),
        ],
        out_specs=pl.BlockSpec((1, b, tn), lambda l, j: (l, 0, j)),
        compiler_params=_cparams(("parallel", "parallel"), VMEM_LIMIT),
        name="adaln",
    )(c, w_ada, b_ada.reshape(depth, 1, n))


def _in_proj_kernel(x_ref, mod_ref, g_ref, w_ref, b_ref, wgt_ref, bgt_ref,
                    pm_ref, pa_ref, gg_ref, gt_ref, *, col_chunk):
    x = x_ref[...]
    y = _rms(x, g_ref[...])
    h = y * (1.0 + mod_ref[1:2, :]) + mod_ref[0:1, :]
    hb = h.astype(BF16)
    c0 = 0
    for o_ref in (pm_ref, pa_ref, gg_ref):
        width = o_ref.shape[-1]
        for a in range(0, width, col_chunk):
            e = min(a + col_chunk, width)
            acc = jnp.dot(hb, w_ref[:, c0 + a:c0 + e], preferred_element_type=F32)
            o_ref[:, a:e] = (acc + b_ref[:, c0 + a:c0 + e]).astype(o_ref.dtype)
        c0 += width
    gt = lax.dot_general(wgt_ref[...], hb, (((1,), (1,)), ((), ())), preferred_element_type=F32)
    gt_ref[...] = gt + bgt_ref[...]


def _in_proj(x, mod_l, g1, w_main, b_main, wgt, bgt, *, tm):
    b, s, d = x.shape
    nw = w_main.shape[1]
    gg_w = nw - PM_W - PA_W
    grows = wgt.shape[0]
    kern = functools.partial(_in_proj_kernel, col_chunk=512)
    return pl.pallas_call(
        kern,
        out_shape=(
            jax.ShapeDtypeStruct((b, s, PM_W), BF16),
            jax.ShapeDtypeStruct((b, s, PA_W), BF16),
            jax.ShapeDtypeStruct((b, s, gg_w), BF16),
            jax.ShapeDtypeStruct((b, grows, s), F32),
        ),
        grid=(b, s // tm),
        in_specs=[
            pl.BlockSpec((None, tm, d), lambda bi, i: (bi, i, 0)),
            pl.BlockSpec((None, 6, d), lambda bi, i: (bi, 0, 0)),
            pl.BlockSpec((1, d), lambda bi, i: (0, 0)),
            pl.BlockSpec((d, nw), lambda bi, i: (0, 0)),
            pl.BlockSpec((1, nw), lambda bi, i: (0, 0)),
            pl.BlockSpec((grows, d), lambda bi, i: (0, 0)),
            pl.BlockSpec((grows, 1), lambda bi, i: (0, 0)),
        ],
        out_specs=(
            pl.BlockSpec((None, tm, PM_W), lambda bi, i: (bi, i, 0)),
            pl.BlockSpec((None, tm, PA_W), lambda bi, i: (bi, i, 0)),
            pl.BlockSpec((None, tm, gg_w), lambda bi, i: (bi, i, 0)),
            pl.BlockSpec((None, grows, tm), lambda bi, i: (bi, 0, i)),
        ),
        compiler_params=_cparams(("parallel", "parallel"), VMEM_LIMIT),
        name="in_proj",
    )(x, mod_l, g1, w_main, b_main, wgt, bgt)


def _mlstm_kernel(q_ref, k_ref, v_ref, o_ref, gt_ref, cwq_ref, cwk_ref, cbq_ref, cbk_ref, ng_ref,
                  y_ref, xs_ref, qc_ref, kt_ref, tab_ref, col_ref, hs_ref, stf_ref, stb_ref, mf_ref, mb_ref,
                  *, seq, blk):
    L = MLSTM_CHUNK
    nc = seq // L
    half = nc // 2
    pad = 8

    zeros_pad = jnp.zeros((pad, LANES), F32)
    xs_ref[0:pad, :] = zeros_pad
    xs_ref[pad + seq:pad + seq + pad, :] = zeros_pad

    def conv_pass(src_ref, w_ref, b_ref, dst_ref, scale, transposed):
        for r0 in range(0, seq, blk):
            xs_ref[pad + r0:pad + r0 + blk, :] = src_ref[r0:r0 + blk, :].astype(F32)
        for r0 in range(0, seq, blk):
            acc = jnp.zeros((blk, LANES), F32) + b_ref[...]
            for j in range(CONV_W):
                off = pad + r0 + j - CONV_W // 2
                acc = acc + w_ref[j:j + 1, :] * xs_ref[off:off + blk, :]
            yv = acc * jax.nn.sigmoid(acc) * scale
            if transposed:
                dst_ref[:, r0:r0 + blk] = jnp.transpose(yv).astype(BF16)
            else:
                dst_ref[r0:r0 + blk, :] = yv.astype(BF16)

    conv_pass(q_ref, cwq_ref, cbq_ref, qc_ref, MLSTM_HD ** -0.5, False)
    conv_pass(k_ref, cwk_ref, cbk_ref, kt_ref, 1.0, True)

    jj = lax.broadcasted_iota(jnp.int32, (L, L), 0)
    ll = lax.broadcasted_iota(jnp.int32, (L, L), 1)
    tri = jnp.concatenate([jnp.where(jj <= ll, 1.0, 0.0), jnp.where(jj >= ll, 1.0, 0.0)], axis=1).astype(BF16)

    def split3(x):
        hi = x.astype(BF16)
        r1 = x - hi.astype(F32)
        mid = r1.astype(BF16)
        lo = (r1 - mid.astype(F32)).astype(BF16)
        return jnp.concatenate([hi, mid, lo], axis=0)

    gate_unroll = 4 if nc % 4 == 0 else 1

    def gate_body(i, carry):
        t0s = [pl.multiple_of((i * gate_unroll + u) * L, L) for u in range(gate_unroll)]
        gs = [gt_ref[:, pl.ds(t0, L)] for t0 in t0s]
        c3s = [jnp.dot(split3(jax.nn.log_sigmoid(g)), tri, preferred_element_type=F32) for g in gs]
        row = lax.broadcasted_iota(jnp.int32, (8, L), 0)
        tabs = []
        for t0, g, c3 in zip(t0s, gs, c3s):
            c2 = c3[0:8] + c3[8:16] + c3[16:24]
            cf, cb = c2[:, 0:L], c2[:, L:2 * L]
            tab = jnp.where(row == 0, g, jnp.where(row == 1, cf, jnp.where(row == 2, g, cb)))
            tab = jnp.where(row < 4, tab, 0.0)
            tab_ref[:, pl.ds(t0, L)] = tab
            tabs.append(tab)
        colfs = [jnp.transpose(jnp.concatenate([tab, jnp.zeros((L - 8, L), F32)], axis=0)) for tab in tabs]
        for t0, colf in zip(t0s, colfs):
            hi = colf.astype(BF16).astype(F32)
            r1 = colf - hi
            mid = r1.astype(BF16).astype(F32)
            lo = (r1 - mid).astype(BF16).astype(F32)
            col_ref[pl.ds(t0, L), :] = (hi + pltpu.roll(mid, 8, axis=1) + pltpu.roll(lo, 16, axis=1)).astype(BF16)
        return carry

    lax.fori_loop(0, nc // gate_unroll, gate_body, 0)

    srow = lax.broadcasted_iota(jnp.int32, (L, 4 * L), 0)
    scol = lax.broadcasted_iota(jnp.int32, (L, 4 * L), 1) // L
    sel3 = jnp.where((srow < 24) & (srow % 8 == scol), 1.0, 0.0).astype(BF16)

    stf_ref[...] = jnp.zeros_like(stf_ref)
    stb_ref[...] = jnp.zeros_like(stb_ref)
    mf_ref[...] = jnp.zeros_like(mf_ref)
    mb_ref[...] = jnp.zeros_like(mb_ref)
    ones_blk = jnp.ones((L, L), BF16)

    def stage_a(t0, fwd):
        c = dict(t0=t0, fwd=fwd)
        c["q"] = qc_ref[pl.ds(t0, L), :]
        c["kt"] = kt_ref[:, pl.ds(t0, L)]
        c["v"] = v_ref[pl.ds(t0, L), :]
        c["cols"] = jnp.dot(col_ref[pl.ds(t0, L), :], sel3, preferred_element_type=F32)
        c["qk"] = jnp.dot(c["q"], c["kt"], preferred_element_type=F32)
        return c

    def stage_b(c):
        r8 = tab_ref[:, pl.ds(c["t0"], L)]
        cols = c["cols"]
        if c["fwd"]:
            li_c, b_c = cols[:, 0:L], cols[:, L:2 * L]
            li_r, b_r = r8[0:1, :], r8[1:2, :]
            bl = b_c[L - 1:L, :]
            mask = ll <= jj
        else:
            li_c, b_c = cols[:, 2 * L:3 * L], cols[:, 3 * L:4 * L]
            li_r, b_r = r8[2:3, :], r8[3:4, :]
            bl = b_c[0:1, :]
            mask = ll >= jj
        a_c = bl - b_c + li_c
        a_max = jnp.max(a_c, axis=0, keepdims=True)
        w_c = jnp.exp(a_c - a_max)
        vw = (c["v"].astype(F32) * w_c).astype(BF16)
        c["cat"] = jnp.concatenate([vw, w_c.astype(BF16)], axis=1)
        d = jnp.where(mask, b_c + (li_r - b_r), -jnp.inf)
        c.update(b_c=b_c, bl=bl, a_max=a_max, d=d, dmax=jnp.max(d, axis=1, keepdims=True))
        del c["cols"]

    def stage_c(c):
        c["chunk2"] = jnp.dot(c["kt"], c.pop("cat"), preferred_element_type=F32)

    def stage_d(c, st_ref, m_ref):
        m_prev = m_ref[...]
        c["st"] = st_ref[...]
        m_inter = c["b_c"] + m_prev
        m_out = jnp.maximum(m_inter, c["dmax"])
        c["inter_w"] = jnp.exp(m_inter - m_out)
        c["s"] = (c.pop("qk") * jnp.exp(c["d"] - m_out)).astype(BF16)
        c["m_out"] = m_out
        m_new = jnp.maximum(c["bl"] + m_prev, c["a_max"])
        c["decay"] = jnp.exp(c["bl"] + m_prev - m_new)
        c["inject"] = jnp.exp(c["a_max"] - m_new)
        m_ref[...] = m_new

    def stage_e(c):
        vcat = jnp.concatenate([c["v"], ones_blk], axis=1)
        c["r2"] = jnp.dot(c["s"], vcat, preferred_element_type=F32)
        c["q2"] = jnp.dot(c["q"], c["st"].astype(BF16), preferred_element_type=F32)

    def stage_f(c, st_ref):
        r2, q2, inter_w = c["r2"], c["q2"], c["inter_w"]
        num = r2[:, 0:L] + inter_w * q2[:, 0:L]
        den = r2[:, L:2 * L] + inter_w * q2[:, L:2 * L]
        st_ref[...] = c["decay"][:, 0:1] * c["st"] + c["inject"][:, 0:1] * c["chunk2"]
        return num / jnp.maximum(jnp.abs(den), jnp.exp(-c["m_out"]))

    def finalize(t0, h):
        yv = _rms(h, ng_ref[...]) * jax.nn.sigmoid(o_ref[pl.ds(t0, L), :].astype(F32))
        y_ref[pl.ds(t0, L), :] = yv.astype(y_ref.dtype)

    unroll = 4 if half % 4 == 0 else 1

    def body(first_chunk, emit):
        chunks = []
        for u in range(unroll):
            cidx = first_chunk + u
            chunks.append((stage_a(pl.multiple_of(cidx * L, L), True),
                           stage_a(pl.multiple_of((nc - 1 - cidx) * L, L), False)))
        for pair in chunks:
            for c in pair:
                stage_b(c)
        for pair in chunks:
            for c in pair:
                stage_c(c)
        for cf, cb in chunks:
            stage_d(cf, stf_ref, mf_ref)
            stage_d(cb, stb_ref, mb_ref)
            stage_e(cf)
            stage_e(cb)
            emit(cf["t0"], stage_f(cf, stf_ref))
            emit(cb["t0"], stage_f(cb, stb_ref))

    def keep(t0, h):
        hs_ref[pl.ds(t0, L), :] = h

    def add_finalize(t0, h):
        finalize(t0, hs_ref[pl.ds(t0, L), :] + h)

    def first_half(i, carry):
        body(i * unroll, keep)
        return carry

    def second_half(i, carry):
        body(half + i * unroll, add_finalize)
        return carry

    lax.fori_loop(0, half // unroll, first_half, 0)
    lax.fori_loop(0, half // unroll, second_half, 0)


def _mlstm(pm, gt, conv_w, conv_b, norm_g):
    b, s, _ = pm.shape
    nh = MLSTM_HEADS
    assert s % (2 * MLSTM_CHUNK) == 0
    blk = 512 if s % 512 == 0 else MLSTM_CHUNK
    kern = functools.partial(_mlstm_kernel, seq=s, blk=blk)
    L = MLSTM_CHUNK
    col = lambda off: (lambda bi, h: (bi, 0, off + h))
    return pl.pallas_call(
        kern,
        out_shape=jax.ShapeDtypeStruct((b, s, MLSTM_W), BF16),
        grid=(b, nh),
        in_specs=[
            pl.BlockSpec((None, s, LANES), col(0)),
            pl.BlockSpec((None, s, LANES), col(nh)),
            pl.BlockSpec((None, s, LANES), col(2 * nh)),
            pl.BlockSpec((None, s, LANES), col(3 * nh)),
            pl.BlockSpec((None, 8, s), lambda bi, h: (bi, h, 0)),
            pl.BlockSpec((CONV_W, LANES), lambda bi, h: (0, h)),
            pl.BlockSpec((CONV_W, LANES), lambda bi, h: (0, nh + h)),
            pl.BlockSpec((1, LANES), lambda bi, h: (0, h)),
            pl.BlockSpec((1, LANES), lambda bi, h: (0, nh + h)),
            pl.BlockSpec((1, LANES), lambda bi, h: (0, h)),
        ],
        out_specs=pl.BlockSpec((None, s, LANES), lambda bi, h: (bi, 0, h)),
        scratch_shapes=[
            pltpu.VMEM((s + 16, LANES), F32),
            pltpu.VMEM((s, LANES), BF16),
            pltpu.VMEM((LANES, s), BF16),
            pltpu.VMEM((8, s), F32),
            pltpu.VMEM((s, LANES), BF16),
            pltpu.VMEM((s, LANES), F32),
            pltpu.VMEM((L, 2 * L), F32),
            pltpu.VMEM((L, 2 * L), F32),
            pltpu.VMEM((1, L), F32),
            pltpu.VMEM((1, L), F32),
        ],
        compiler_params=_cparams(("parallel", "parallel"), VMEM_LIMIT),
        name="mlstm",
    )(pm, pm, pm, pm, gt, conv_w, conv_w, conv_b, conv_b, norm_g)


def _swap16(y, lane):
    fwd = pltpu.roll(y, LANES - 16, axis=1)
    bwd = pltpu.roll(y, 16, axis=1)
    return jnp.where((lane % 32) < 16, fwd, bwd)


def _nt(a, b):
    return lax.dot_general(a, b, (((1,), (1,)), ((), ())), preferred_element_type=F32)


def _attn_prep_kernel(pa_ref, cos_ref, sin_ref, qg_ref, kg_ref, mseg_ref, qt_ref, kp_ref, vt_ref, *, q_scale):
    tm = pa_ref.shape[0]
    lane = lax.broadcasted_iota(jnp.int32, (tm, LANES), 1)
    cos = cos_ref[...]
    sin = sin_ref[...]
    mseg = mseg_ref[...]
    r = lax.broadcasted_iota(jnp.int32, (LANES, LANES), 0)
    c = lax.broadcasted_iota(jnp.int32, (LANES, LANES), 1)
    sel_lo = jnp.where((r == c) & (r < ATTN_HD), 1.0, 0.0).astype(BF16)
    sel_hi = jnp.where((c == r + ATTN_HD) & (r < ATTN_HD), 1.0, 0.0).astype(BF16)
    ones_row = jnp.where(lax.broadcasted_iota(jnp.int32, (LANES, tm), 0) == ATTN_HD, 1.0, 0.0)

    def norm_rope(x, g, scale):
        xx = x * x
        hi = xx.astype(BF16)
        lo = (xx - hi.astype(F32)).astype(BF16)
        ms = jnp.dot(jnp.concatenate([hi, lo], axis=1), mseg, preferred_element_type=F32)
        y = x * lax.rsqrt(ms + NORM_EPS) * g
        return (y * cos + _swap16(y, lane) * sin) * scale

    for j in range(ATTN_W // LANES):
        x = pa_ref[:, j * LANES:(j + 1) * LANES].astype(F32)
        y = norm_rope(x, qg_ref[...], q_scale).astype(BF16)
        qt_ref[2 * j] = _nt(sel_lo, y).astype(BF16)
        qt_ref[2 * j + 1] = _nt(sel_hi, y).astype(BF16)
    for j in range(KV_W // LANES):
        x = pa_ref[:, ATTN_W + j * LANES:ATTN_W + (j + 1) * LANES].astype(F32)
        y = norm_rope(x, kg_ref[...], 1.0)
        kp_ref[2 * j] = jnp.where(lane < ATTN_HD, y, 0.0).astype(BF16)
        kp_ref[2 * j + 1] = jnp.where(lane < ATTN_HD, pltpu.roll(y, ATTN_HD, axis=1), 0.0).astype(BF16)
        xv = pa_ref[:, ATTN_W + KV_W + j * LANES:ATTN_W + KV_W + (j + 1) * LANES]
        vt_ref[2 * j] = (_nt(sel_lo, xv) + ones_row).astype(BF16)
        vt_ref[2 * j + 1] = (_nt(sel_hi, xv) + ones_row).astype(BF16)


def _attn_prep(pa, cos_t, sin_t, qg, kg, mseg, *, tm):
    b, s, _ = pa.shape
    q_scale = (ATTN_HD ** -0.5) * math.log2(math.e)
    kern = functools.partial(_attn_prep_kernel, q_scale=q_scale)
    return pl.pallas_call(
        kern,
        out_shape=(
            jax.ShapeDtypeStruct((b, ATTN_HEADS, LANES, s), BF16),
            jax.ShapeDtypeStruct((b, KV_HEADS, s, LANES), BF16),
            jax.ShapeDtypeStruct((b, KV_HEADS, LANES, s), BF16),
        ),
        grid=(b, s // tm),
        in_specs=[
            pl.BlockSpec((None, tm, PA_W), lambda bi, i: (bi, i, 0)),
            pl.BlockSpec((tm, LANES), lambda bi, i: (i, 0)),
            pl.BlockSpec((tm, LANES), lambda bi, i: (i, 0)),
            pl.BlockSpec((1, LANES), lambda bi, i: (0, 0)),
            pl.BlockSpec((1, LANES), lambda bi, i: (0, 0)),
            pl.BlockSpec((2 * LANES, LANES), lambda bi, i: (0, 0)),
        ],
        out_specs=(
            pl.BlockSpec((None, ATTN_HEADS, LANES, tm), lambda bi, i: (bi, 0, 0, i)),
            pl.BlockSpec((None, KV_HEADS, tm, LANES), lambda bi, i: (bi, 0, i, 0)),
            pl.BlockSpec((None, KV_HEADS, LANES, tm), lambda bi, i: (bi, 0, 0, i)),
        ),
        compiler_params=_cparams(("parallel", "parallel"), VMEM_LIMIT),
        name="attn_prep",
    )(pa, cos_t, sin_t, qg, kg, mseg)


def _attn_kernel(qt_ref, k_ref, vt_ref, o_ref, m_ref, acc_ref, s_ref, *, tk):
    g, _, tq = qt_ref.shape
    s = k_ref.shape[0]
    n = s // tk
    m_ref[...] = jnp.full(m_ref.shape, -jnp.inf, F32)
    acc_ref[...] = jnp.zeros(acc_ref.shape, F32)

    def scores(j, slot, h):
        t0 = pl.multiple_of(j * tk, tk)
        s_ref[slot, :, h * tq:(h + 1) * tq] = jnp.dot(k_ref[pl.ds(t0, tk), :], qt_ref[h],
                                                      preferred_element_type=F32)

    def softmax(slot, h):
        cols = slice(h * tq, (h + 1) * tq)
        sc = s_ref[slot, :, cols]
        m_prev = m_ref[:, cols]
        m_new = jnp.maximum(m_prev, jnp.max(sc, axis=0, keepdims=True))
        m_ref[:, cols] = m_new
        return jnp.exp2(m_prev - m_new), jnp.exp2(sc - m_new).astype(BF16)

    def pv(j, h, alpha, p):
        t0 = pl.multiple_of(j * tk, tk)
        cols = slice(h * tq, (h + 1) * tq)
        acc_ref[:, cols] = alpha * acc_ref[:, cols] + jnp.dot(vt_ref[0:PV_ROWS, pl.ds(t0, tk)], p,
                                                              preferred_element_type=F32)

    def step(j, slot, prefetch):
        pending = None
        for h in range(g):
            if prefetch:
                scores(j + 1, 1 - slot, h)
            current = softmax(slot, h)
            if pending is not None:
                pv(j, h - 1, *pending)
            pending = current
        pv(j, g - 1, *pending)

    for h in range(g):
        scores(0, 0, h)

    unroll = 4 if n % 4 == 0 else 2

    def body(i, carry):
        for u in range(unroll):
            step(unroll * i + u, u % 2, True)
        return carry

    lax.fori_loop(0, n // unroll - 1, body, 0)
    for u in range(unroll):
        step(n - unroll + u, u % 2, u < unroll - 1)
    acc = acc_ref[...]
    o = (acc[0:ATTN_HD, :] / acc[ATTN_HD:ATTN_HD + 1, :]).astype(BF16)
    r = lax.broadcasted_iota(jnp.int32, (LANES, LANES), 0)
    c = lax.broadcasted_iota(jnp.int32, (LANES, LANES), 1)
    eye = jnp.where(r == c, 1.0, 0.0).astype(BF16)
    for pair in range(g // 2):
        rows = jnp.concatenate([o[:, (2 * pair) * tq:(2 * pair + 1) * tq],
                                o[:, (2 * pair + 1) * tq:(2 * pair + 2) * tq]], axis=0)
        out = lax.dot_general(rows, eye, (((0,), (0,)), ((), ())), preferred_element_type=F32)
        o_ref[:, pair * LANES:(pair + 1) * LANES] = out.astype(o_ref.dtype)


def _attention(qt, kp, vt, *, tq, tk):
    b, nh, _, s = qt.shape
    g = nh // KV_HEADS
    assert (s // tk) % 2 == 0 and g % 2 == 0
    kern = functools.partial(_attn_kernel, tk=tk)
    return pl.pallas_call(
        kern,
        out_shape=jax.ShapeDtypeStruct((b, s, ATTN_W), BF16),
        grid=(b, KV_HEADS, s // tq),
        in_specs=[
            pl.BlockSpec((None, g, LANES, tq), lambda bi, kv, i: (bi, kv, 0, i)),
            pl.BlockSpec((None, None, s, LANES), lambda bi, kv, i: (bi, kv, 0, 0)),
            pl.BlockSpec((None, None, LANES, s), lambda bi, kv, i: (bi, kv, 0, 0)),
        ],
        out_specs=pl.BlockSpec((None, tq, g * ATTN_HD), lambda bi, kv, i: (bi, i, kv)),
        scratch_shapes=[
            pltpu.VMEM((1, g * tq), F32),
            pltpu.VMEM((PV_ROWS, g * tq), F32),
            pltpu.VMEM((2, tk, g * tq), F32),
        ],
        compiler_params=_cparams(("parallel", "parallel", "parallel"), VMEM_LIMIT),
        name="attn",
    )(qt, kp, vt)


def _merge_kernel(x_ref, ym_ref, ya_ref, gg_ref, mod_ref, wbm_ref, wba_ref, wo_ref, g2_ref, wr_ref, br_ref,
                  xn_ref, h2_ref, lt_ref):
    d = x_ref.shape[-1]
    a = jnp.dot(ym_ref[...], wbm_ref[...], preferred_element_type=F32)
    bm = jnp.dot(ya_ref[...], wba_ref[...], preferred_element_type=F32)
    gm = jax.nn.sigmoid(gg_ref[:, 0:d].astype(F32))
    ga = jax.nn.sigmoid(gg_ref[:, d:2 * d].astype(F32))
    merged = (gm * a + ga * bm).astype(BF16)
    u = jnp.dot(merged, wo_ref[...], preferred_element_type=F32)
    xn = x_ref[...] + mod_ref[2:3, :] * u
    xn_ref[...] = xn
    h2 = _rms(xn, g2_ref[...]) * (1.0 + mod_ref[4:5, :]) + mod_ref[3:4, :]
    h2_ref[...] = h2.astype(h2_ref.dtype)
    lt = lax.dot_general(wr_ref[...], h2, (((1,), (1,)), ((), ())), precision=HIGHEST,
                         preferred_element_type=F32)
    lt_ref[...] = lt + br_ref[...]


def _merge(x, ym, ya, gg, mod_l, wbm, wba, wo, g2, wr_t, br_t, *, tm):
    b, s, d = x.shape
    ns = s // tm
    rr = wr_t.shape[0]
    full = lambda shp: pl.BlockSpec(shp, lambda bi, i: tuple(0 for _ in shp))
    return pl.pallas_call(
        _merge_kernel,
        out_shape=(
            jax.ShapeDtypeStruct((b, s, d), F32),
            jax.ShapeDtypeStruct((b * s, d), BF16),
            jax.ShapeDtypeStruct((rr, b * s), F32),
        ),
        grid=(b, ns),
        in_specs=[
            pl.BlockSpec((None, tm, d), lambda bi, i: (bi, i, 0)),
            pl.BlockSpec((None, tm, MLSTM_W), lambda bi, i: (bi, i, 0)),
            pl.BlockSpec((None, tm, ATTN_W), lambda bi, i: (bi, i, 0)),
            pl.BlockSpec((None, tm, 2 * d), lambda bi, i: (bi, i, 0)),
            pl.BlockSpec((None, 6, d), lambda bi, i: (bi, 0, 0)),
            full((MLSTM_W, d)),
            full((ATTN_W, d)),
            full((d, d)),
            full((1, d)),
            full((rr, d)),
            full((rr, 1)),
        ],
        out_specs=(
            pl.BlockSpec((None, tm, d), lambda bi, i: (bi, i, 0)),
            pl.BlockSpec((tm, d), lambda bi, i: (bi * ns + i, 0)),
            pl.BlockSpec((rr, tm), lambda bi, i: (0, bi * ns + i)),
        ),
        compiler_params=_cparams(("parallel", "parallel"), VMEM_LIMIT),
        name="merge",
    )(x, ym, ya, gg, mod_l, wbm, wba, wo, g2, wr_t, br_t)


def _route_kernel(lt_ref, pos_ref, wts_ref, runs_ref, carry_ref):
    tb = lt_ref.shape[1]
    epg = EXPERTS_PER_GROUP

    @pl.when(pl.program_id(0) == 0)
    def _():
        carry_ref[...] = jnp.zeros_like(carry_ref)

    row8 = lax.broadcasted_iota(jnp.int32, (8, tb), 0)
    gl = jnp.where(row8 < N_GROUPS, lt_ref[0:8, :], -jnp.inf)
    ge = jnp.exp(gl - jnp.max(gl, axis=0, keepdims=True))
    pg = ge / jnp.sum(ge, axis=0, keepdims=True)
    p_top = jnp.max(pg, axis=0, keepdims=True)
    g_idx = jnp.min(jnp.where(pg == p_top, row8, 8), axis=0, keepdims=True)

    el = jnp.zeros((epg, tb), F32)
    for g in range(N_GROUPS):
        el = jnp.where(g_idx == g, lt_ref[8 + g * epg:8 + (g + 1) * epg, :], el)
    ee = jnp.exp(el - jnp.max(el, axis=0, keepdims=True))
    pe = ee / jnp.sum(ee, axis=0, keepdims=True)
    v1 = jnp.max(pe, axis=0, keepdims=True)
    i1 = jnp.min(jnp.where(pe == v1, row8, 8), axis=0, keepdims=True)
    pe2 = jnp.where(row8 == i1, -1.0, pe)
    v2 = jnp.max(pe2, axis=0, keepdims=True)
    i2 = jnp.min(jnp.where(pe2 == v2, row8, 8), axis=0, keepdims=True)
    denom = v1 + v2
    w0 = v1 / denom * p_top
    w1 = v2 / denom * p_top
    e0 = g_idx * epg + i1
    e1 = g_idx * epg + i2

    rowe = lax.broadcasted_iota(jnp.int32, (N_EXPERTS, tb), 0)
    oh0 = rowe == e0
    oh1 = rowe == e1
    oh = jnp.where(oh0 | oh1, 1.0, 0.0)
    src = lax.broadcasted_iota(jnp.int32, (tb, tb), 0)
    dst = lax.broadcasted_iota(jnp.int32, (tb, tb), 1)
    strict = jnp.where(src < dst, 1.0, 0.0).astype(BF16)
    cum = jnp.dot(oh.astype(BF16), strict, preferred_element_type=F32)
    cnt_col = jnp.sum(oh, axis=1, keepdims=True)
    seg_col = jnp.floor((cnt_col + (SEG_ALIGN - 1.0)) * (1.0 / SEG_ALIGN)) * SEG_ALIGN
    er = lax.broadcasted_iota(jnp.int32, (N_EXPERTS, N_EXPERTS), 0)
    ec = lax.broadcasted_iota(jnp.int32, (N_EXPERTS, N_EXPERTS), 1)
    before = jnp.where(ec < er, 1.0, 0.0).astype(BF16)
    start_col = jnp.dot(before, jnp.broadcast_to(seg_col, (N_EXPERTS, LANES)).astype(BF16),
                        preferred_element_type=F32)[:, 0:1]
    base = start_col + cum
    p0 = jnp.sum(jnp.where(oh0, base, 0.0), axis=0, keepdims=True)
    p1 = jnp.sum(jnp.where(oh1, base, 0.0), axis=0, keepdims=True)
    pos_ref[...] = jnp.where(row8 == 0, p0, jnp.where(row8 == 1, p1, 0.0)).astype(jnp.int32)
    wts_ref[...] = jnp.where(row8 == 0, w0, jnp.where(row8 == 1, w1, 0.0))

    ohp = jnp.concatenate([oh, jnp.zeros((LANES - N_EXPERTS, tb), F32)], axis=0).astype(BF16)
    cnt_row = _nt(jnp.ones((8, tb), BF16), ohp)
    seg_row = jnp.floor((cnt_row + (SEG_ALIGN - 1.0)) * (1.0 / SEG_ALIGN)) * SEG_ALIGN
    carry = carry_ref[...]
    runs_ref[0] = (seg_row + pltpu.roll(carry, N_EXPERTS, axis=1)).astype(jnp.int32)
    carry_ref[...] = carry + seg_row


def _route(lt, *, tb):
    rr, t = lt.shape
    nblk = t // tb
    return pl.pallas_call(
        _route_kernel,
        out_shape=(
            jax.ShapeDtypeStruct((8, t), jnp.int32),
            jax.ShapeDtypeStruct((8, t), F32),
            jax.ShapeDtypeStruct((nblk, 8, LANES), jnp.int32),
        ),
        grid=(nblk,),
        in_specs=[pl.BlockSpec((rr, tb), lambda i: (0, i))],
        out_specs=(
            pl.BlockSpec((8, tb), lambda i: (0, i)),
            pl.BlockSpec((8, tb), lambda i: (0, i)),
            pl.BlockSpec((1, 8, LANES), lambda i: (i, 0, 0)),
        ),
        scratch_shapes=[pltpu.VMEM((8, LANES), F32)],
        compiler_params=_cparams(("arbitrary",), VMEM_LIMIT),
        name="route",
    )(lt)


def _segment_copies(blk, runs_ref, off_ref, local_ref, hbm_ref, sem, *, to_hbm):
    big = 2 * SEG_ALIGN

    def piece(s, d, rows):
        loc = local_ref.at[pl.ds(pl.multiple_of(s, SEG_ALIGN), rows)]
        far = hbm_ref.at[pl.ds(pl.multiple_of(d, SEG_ALIGN), rows)]
        return pltpu.make_async_copy(loc, far, sem) if to_hbm else pltpu.make_async_copy(far, loc, sem)

    def expert(e, carry):
        s0, total_big, total_small = carry
        n = runs_ref[blk * 2 * N_EXPERTS + e]
        d0 = off_ref[e] + runs_ref[blk * 2 * N_EXPERTS + N_EXPERTS + e]
        nbig = n >> (SEG_SHIFT + 1)
        nsmall = (n >> SEG_SHIFT) & 1

        def big_piece(c, carry):
            piece(s0 + c * big, d0 + c * big, big).start()
            return carry

        def small_piece(c, carry):
            piece(s0 + nbig * big + c * SEG_ALIGN, d0 + nbig * big + c * SEG_ALIGN, SEG_ALIGN).start()
            return carry

        lax.fori_loop(0, nbig, big_piece, 0)
        lax.fori_loop(0, nsmall, small_piece, 0)
        return s0 + n, total_big + nbig, total_small + nsmall

    zero = jnp.int32(0)
    _, total_big, total_small = lax.fori_loop(0, N_EXPERTS, expert, (zero, zero, zero))

    def wait_big(c, carry):
        piece(0, 0, big).wait()
        return carry

    def wait_small(c, carry):
        piece(0, 0, SEG_ALIGN).wait()
        return carry

    lax.fori_loop(0, total_big, wait_big, 0)
    lax.fori_loop(0, total_small, wait_small, 0)


def _dispatch_kernel(runs_ref, off_ref, last_ref, h_ref, pos_ref, xg_ref, sbuf_ref, zero_ref, sem, zsem, *, tme):
    i = pl.program_id(0)
    tb = h_ref.shape[0]
    rp = sbuf_ref.shape[0]

    @pl.when(i == 0)
    def _():
        zero_ref[...] = jnp.zeros_like(zero_ref)
        for wait in (False, True):
            def tail_tile(j, carry, wait=wait):
                cp = pltpu.make_async_copy(zero_ref, xg_ref.at[pl.ds(pl.multiple_of(j * tme, tme), tme)], zsem)
                if wait:
                    cp.wait()
                else:
                    cp.start()
                return carry

            lax.fori_loop(last_ref[N_EXPERTS], xg_ref.shape[0] // tme, tail_tile, 0)
            for e in range(N_EXPERTS):
                @pl.when(last_ref[e] >= 0)
                def _():
                    cp = pltpu.make_async_copy(
                        zero_ref, xg_ref.at[pl.ds(pl.multiple_of(last_ref[e], SEG_ALIGN), tme)], zsem)
                    if wait:
                        cp.wait()
                    else:
                        cp.start()

    ii = lax.broadcasted_iota(jnp.int32, (rp, tb), 0)
    perm = jnp.where((ii == pos_ref[0:1, :]) | (ii == pos_ref[1:2, :]), 1.0, 0.0).astype(BF16)
    sbuf_ref[...] = jnp.dot(perm, h_ref[...], preferred_element_type=F32).astype(BF16)
    _segment_copies(i, runs_ref, off_ref, sbuf_ref, xg_ref, sem, to_hbm=True)


def _dispatch(runs, offsets, last_tile, h2, pos, p_rows, *, tb, tme):
    t, d = h2.shape
    rp = 2 * tb + SEG_ALIGN * N_EXPERTS
    kern = functools.partial(_dispatch_kernel, tme=tme)
    return pl.pallas_call(
        kern,
        out_shape=jax.ShapeDtypeStruct((p_rows, d), BF16),
        grid_spec=pltpu.PrefetchScalarGridSpec(
            num_scalar_prefetch=3,
            grid=(t // tb,),
            in_specs=[
                pl.BlockSpec((tb, d), lambda i, *_: (i, 0)),
                pl.BlockSpec((8, tb), lambda i, *_: (0, i)),
            ],
            out_specs=pl.BlockSpec(memory_space=pl.ANY),
            scratch_shapes=[
                pltpu.VMEM((rp, d), BF16),
                pltpu.VMEM((tme, d), BF16),
                pltpu.SemaphoreType.DMA(()),
                pltpu.SemaphoreType.DMA(()),
            ],
        ),
        compiler_params=_cparams(("arbitrary",), VMEM_LIMIT),
        name="dispatch",
    )(runs, offsets, last_tile, h2, pos)


def _experts_kernel(te_ref, first_ref, used_ref, x_ref, wg_ref, wu_ref, wd_ref, y_ref, wgb, wub, wdb):
    i = pl.program_id(0)

    @pl.when(first_ref[i] == 1)
    def _():
        wgb[...] = wg_ref[...].astype(BF16)
        wub[...] = wu_ref[...].astype(BF16)
        wdb[...] = wd_ref[...].astype(BF16)

    @pl.when(used_ref[i] == 1)
    def _():
        tm = x_ref.shape[0]
        sub = tm // 2 if tm % 512 == 0 else tm
        halves = [slice(r, r + sub) for r in range(0, tm, sub)]
        ups = [(jnp.dot(x_ref[rs, :], wgb[...], preferred_element_type=F32),
                jnp.dot(x_ref[rs, :], wub[...], preferred_element_type=F32)) for rs in halves]
        acts = [(a * jax.nn.sigmoid(a) * u).astype(BF16) for a, u in ups]
        for rs, act in zip(halves, acts):
            y_ref[rs, :] = jnp.dot(act, wdb[...], preferred_element_type=F32).astype(y_ref.dtype)

    @pl.when(used_ref[i] == 0)
    def _():
        y_ref[...] = jnp.zeros_like(y_ref)


def _experts(tile_expert, tile_first, tile_used, xg, w_gate, w_up, w_down, *, layer, tm):
    p_rows, d = xg.shape
    f = w_gate.shape[-1]
    nt = p_rows // tm
    wmap = lambda i, te, fi, us: (layer, te[i], 0, 0)
    return pl.pallas_call(
        _experts_kernel,
        out_shape=jax.ShapeDtypeStruct((p_rows, d), BF16),
        grid_spec=pltpu.PrefetchScalarGridSpec(
            num_scalar_prefetch=3,
            grid=(nt,),
            in_specs=[
                pl.BlockSpec((tm, d), lambda i, te, fi, us: (i, 0)),
                pl.BlockSpec((None, None, d, f), wmap),
                pl.BlockSpec((None, None, d, f), wmap),
                pl.BlockSpec((None, None, f, d), wmap),
            ],
            out_specs=pl.BlockSpec((tm, d), lambda i, te, fi, us: (i, 0)),
            scratch_shapes=[
                pltpu.VMEM((d, f), BF16),
                pltpu.VMEM((d, f), BF16),
                pltpu.VMEM((f, d), BF16),
            ],
        ),
        compiler_params=_cparams(("arbitrary",), VMEM_LIMIT),
        name="experts",
    )(tile_expert, tile_first, tile_used, xg, w_gate, w_up, w_down)


def _combine_kernel(runs_ref, off_ref, x_ref, pos_ref, w_ref, mod_ref, fg_ref, y_ref, o_ref, ybuf_ref, sem, *, final):
    blk = pl.program_id(0) * pl.num_programs(1) + pl.program_id(1)
    tb = x_ref.shape[0]
    rp = ybuf_ref.shape[0]

    @pl.when(blk == 0)
    def _():
        ybuf_ref[...] = jnp.zeros_like(ybuf_ref)

    _segment_copies(blk, runs_ref, off_ref, ybuf_ref, y_ref, sem, to_hbm=False)
    li = lax.broadcasted_iota(jnp.int32, (tb, rp), 1)
    wmat = (jnp.where(li == pos_ref[:, 0:1], w_ref[:, 0:1], 0.0)
            + jnp.where(li == pos_ref[:, 1:2], w_ref[:, 1:2], 0.0))
    moe = jnp.dot(wmat.astype(BF16), ybuf_ref[...], preferred_element_type=F32)
    xo = x_ref[...] + mod_ref[5:6, :] * moe
    if final:
        xo = _rms(xo, fg_ref[...])
    o_ref[...] = xo


def _combine(runs, offsets, xn, pos_col, wts_col, mod_l, fg, y, *, tb, final):
    b, s, d = xn.shape
    ns = s // tb
    rp = 2 * tb + SEG_ALIGN * N_EXPERTS
    kern = functools.partial(_combine_kernel, final=final)
    return pl.pallas_call(
        kern,
        out_shape=jax.ShapeDtypeStruct((b, s, d), F32),
        grid_spec=pltpu.PrefetchScalarGridSpec(
            num_scalar_prefetch=2,
            grid=(b, ns),
            in_specs=[
                pl.BlockSpec((None, tb, d), lambda bi, i, *_: (bi, i, 0)),
                pl.BlockSpec((tb, 8), lambda bi, i, *_: (bi * ns + i, 0)),
                pl.BlockSpec((tb, 8), lambda bi, i, *_: (bi * ns + i, 0)),
                pl.BlockSpec((None, 6, d), lambda bi, i, *_: (bi, 0, 0)),
                pl.BlockSpec((1, d), lambda bi, i, *_: (0, 0)),
                pl.BlockSpec(memory_space=pl.ANY),
            ],
            out_specs=pl.BlockSpec((None, tb, d), lambda bi, i, *_: (bi, i, 0)),
            scratch_shapes=[
                pltpu.VMEM((rp, d), BF16),
                pltpu.SemaphoreType.DMA(()),
            ],
        ),
        compiler_params=_cparams(("arbitrary", "arbitrary"), VMEM_LIMIT),
        name="combine",
    )(runs, offsets, xn, pos_col, wts_col, mod_l, fg, y)


def _rope_tables(seq):
    rows = seq // GRID_W
    row = jnp.repeat(jnp.arange(rows, dtype=F32), GRID_W)
    col = jnp.tile(jnp.arange(GRID_W, dtype=F32), rows)
    half = ATTN_HD // 2
    inv_freq = ROPE_BASE ** (-jnp.arange(0, half, 2, dtype=F32) / half)
    ang_r = row[:, None] * inv_freq
    ang_c = col[:, None] * inv_freq
    cos64 = jnp.concatenate([jnp.cos(ang_r), jnp.cos(ang_r), jnp.cos(ang_c), jnp.cos(ang_c)], axis=-1)
    sin64 = jnp.concatenate([-jnp.sin(ang_r), jnp.sin(ang_r), -jnp.sin(ang_c), jnp.sin(ang_c)], axis=-1)
    return jnp.tile(cos64, (1, LANES // ATTN_HD)), jnp.tile(sin64, (1, LANES // ATTN_HD))


def _plan_tiles(rows, tm, nt):
    tiles = (rows + tm - 1) // tm
    tile_end = jnp.cumsum(tiles)
    tile_start = tile_end - tiles
    offsets = (tile_start * tm).astype(jnp.int32)
    last_tile = jnp.where(tiles > 0, (tile_end - 1) * tm, -1).astype(jnp.int32)
    last_tile = jnp.concatenate([last_tile, tile_end[-1:].astype(jnp.int32)])
    tidx = jnp.arange(nt, dtype=jnp.int32)
    te = jnp.sum((tidx[:, None] >= tile_end[None, :]).astype(jnp.int32), axis=1)
    used = (tidx < tile_end[-1]).astype(jnp.int32)
    last_used = jnp.sum((tile_end[-1] - 1 >= tile_end).astype(jnp.int32))
    te = jnp.where(used == 1, te, last_used).astype(jnp.int32)
    prev = jnp.concatenate([jnp.full((1,), -1, jnp.int32), te[:-1]])
    first = ((te != prev) & (used == 1)).astype(jnp.int32)
    return offsets, last_tile, te, first, used


def _tiles(s, t):
    tm = 512 if s % 512 == 0 else 128
    tq = 512 if s % 2048 == 0 else (256 if s % 256 == 0 else 128)
    tk = 512 if s % 2048 == 0 else 128
    tme = 512
    tbr = 512 if s % 512 == 0 else 128
    nblk = t // tbr
    nt = -(-(2 * t + (SEG_ALIGN - 1) * N_EXPERTS * nblk) // tme) + N_EXPERTS
    return tm, tq, tk, tme, tbr, nt


def kernel(x, c, w_ada, b_ada, norm1_g, w_in, b_in, conv_w, conv_b, mlstm_norm_g, q_norm_g, k_norm_g,
           w_branch_m, w_branch_a, w_out, norm2_g, w_router_group, b_router_group, w_router_expert,
           b_router_expert, w_gate, w_up, w_down, final_norm_g):
    b, s, d = x.shape
    depth = w_ada.shape[0]
    t = b * s
    nh = MLSTM_HEADS
    tm, tq, tk, tme, tbr, nt = _tiles(s, t)
    p_rows = nt * tme

    mod = _adaln(c, w_ada, b_ada).reshape(depth, b, 6, d)
    cos_t, sin_t = _rope_tables(s)
    seg = np.arange(LANES) // ATTN_HD
    mseg = jnp.asarray((seg[:, None] == seg[None, :]).astype(np.float32) / ATTN_HD)
    mseg = jnp.concatenate([mseg, mseg], axis=0).astype(BF16)

    o_g = 4 * MLSTM_W
    o_aq = o_g + 4 * nh

    for l in range(depth):
        w_l = w_in[l]
        w_main = jnp.concatenate([w_l[:, :o_g], w_l[:, o_aq:]], axis=1).astype(BF16)
        b_main = jnp.concatenate([b_in[l][:o_g], b_in[l][o_aq:]])[None, :]
        wg = jnp.transpose(w_l[:, o_g:o_aq].reshape(d, 4, nh), (2, 1, 0))
        wgt = jnp.concatenate([wg, jnp.zeros((nh, 4, d), F32)], axis=1).reshape(nh * 8, d).astype(BF16)
        bg = jnp.transpose(b_in[l][o_g:o_aq].reshape(4, nh))
        bgt = jnp.concatenate([bg, jnp.zeros((nh, 4), F32)], axis=1).reshape(nh * 8, 1)

        pm, pa, gg, gt = _in_proj(x, mod[l], norm1_g[l][None, :], w_main, b_main, wgt, bgt, tm=tm)
        ym = _mlstm(pm, gt, conv_w[l].reshape(CONV_W, 2 * MLSTM_W), conv_b[l][None, :],
                    mlstm_norm_g[l][None, :])
        qg = jnp.tile(q_norm_g[l], LANES // ATTN_HD)[None, :]
        kg = jnp.tile(k_norm_g[l], LANES // ATTN_HD)[None, :]
        qp, kp, vp = _attn_prep(pa, cos_t, sin_t, qg, kg, mseg, tm=tm)
        ya = _attention(qp, kp, vp, tq=tq, tk=tk)

        wr = jnp.concatenate([w_router_group[l], jnp.zeros((d, 8 - N_GROUPS), F32), w_router_expert[l]], axis=1)
        br = jnp.concatenate([b_router_group[l], jnp.zeros((8 - N_GROUPS,), F32), b_router_expert[l]])
        xn, h2, lt = _merge(x, ym, ya, gg, mod[l], w_branch_m[l].astype(BF16), w_branch_a[l].astype(BF16),
                            w_out[l].astype(BF16), norm2_g[l][None, :], jnp.transpose(wr), br[:, None], tm=tm)

        pos, wts, runs = _route(lt, tb=tbr)
        runs = runs[:, 0, :2 * N_EXPERTS]
        rows = runs[-1, :N_EXPERTS] + runs[-1, N_EXPERTS:]
        offsets, last_tile, te, first, used = _plan_tiles(rows, tme, nt)
        runs = runs.reshape(-1)
        xg = _dispatch(runs, offsets, last_tile, h2, pos, p_rows, tb=tbr, tme=tme)
        y = _experts(te, first, used, xg, w_gate, w_up, w_down, layer=l, tm=tme)
        x = _combine(runs, offsets, xn, jnp.transpose(pos), jnp.transpose(wts), mod[l], final_norm_g[None, :], y,
                     tb=tbr, final=(l == depth - 1))
    return x
```

```python
import functools
import math

import jax
import jax.numpy as jnp
import numpy as np
from jax import lax
from jax.experimental import pallas as pl
from jax.experimental.pallas import tpu as pltpu

F32 = jnp.float32
BF16 = jnp.bfloat16
HIGHEST = lax.Precision.HIGHEST

GRID_W = 64
MLSTM_HEADS = 4
MLSTM_HD = 128
MLSTM_W = MLSTM_HEADS * MLSTM_HD
CONV_W = 5
ATTN_HEADS = 8
KV_HEADS = 2
ATTN_HD = 64
ATTN_W = ATTN_HEADS * ATTN_HD
KV_W = KV_HEADS * ATTN_HD
ROPE_BASE = 10000.0
N_GROUPS = 4
EXPERTS_PER_GROUP = 8
N_EXPERTS = N_GROUPS * EXPERTS_PER_GROUP
NORM_EPS = 1e-6

LANES = 128
MLSTM_CHUNK = 128
SEG_ALIGN = 16
SEG_SHIFT = 4
PV_ROWS = ATTN_HD + 16
VMEM_LIMIT = 56 * 1024 * 1024

PM_W = 4 * MLSTM_W
PA_W = ATTN_W + 2 * KV_W
GG_W = None
ROUTER_ROWS = 8 + N_EXPERTS


def _cparams(sem, vmem=None):
    return pltpu.CompilerParams(dimension_semantics=sem, vmem_limit_bytes=vmem)


def _rms(x, g):
    ms = jnp.mean(x * x, axis=-1, keepdims=True)
    return x * lax.rsqrt(ms + NORM_EPS) * g


def _adaln_kernel(c_ref, w_ref, b_ref, o_ref):
    c = c_ref[...]
    cond = c * jax.nn.sigmoid(c)
    o_ref[0] = jnp.dot(cond, w_ref[0], precision=HIGHEST, preferred_element_type=F32) + b_ref[0]


def _adaln(c, w_ada, b_ada):
    depth, d, n = w_ada.shape
    b = c.shape[0]
    tn = 1536 if n % 1536 == 0 else n
    return pl.pallas_call(
        _adaln_kernel,
        out_shape=jax.ShapeDtypeStruct((depth, b, n), F32),
        grid=(depth, n // tn),
        in_specs=[
            pl.BlockSpec((b, d), lambda l, j: (0, 0)),
            pl.BlockSpec((1, d, tn), lambda l, j: (l, 0, j)),
            pl.BlockSpec((1, 1, tn), lambda l, j: (l, 0, j)),
        ],
        out_specs=pl.BlockSpec((1, b, tn), lambda l, j: (l, 0, j)),
        compiler_params=_cparams(("parallel", "parallel"), VMEM_LIMIT),
        name="adaln",
    )(c, w_ada, b_ada.reshape(depth, 1, n))


def _in_proj_kernel(x_ref, mod_ref, g_ref, w_ref, b_ref, wgt_ref, bgt_ref,
                    pm_ref, pa_ref, gg_ref, gt_ref, *, col_chunk):
    x = x_ref[...]
    y = _rms(x, g_ref[...])
    h = y * (1.0 + mod_ref[1:2, :]) + mod_ref[0:1, :]
    hb = h.astype(BF16)
    c0 = 0
    for o_ref in (pm_ref, pa_ref, gg_ref):
        width = o_ref.shape[-1]
        for a in range(0, width, col_chunk):
            e = min(a + col_chunk, width)
            acc = jnp.dot(hb, w_ref[:, c0 + a:c0 + e], preferred_element_type=F32)
            o_ref[:, a:e] = (acc + b_ref[:, c0 + a:c0 + e]).astype(o_ref.dtype)
        c0 += width
    gt = lax.dot_general(wgt_ref[...], hb, (((1,), (1,)), ((), ())), preferred_element_type=F32)
    gt_ref[...] = gt + bgt_ref[...]


def _in_proj(x, mod_l, g1, w_main, b_main, wgt, bgt, *, tm):
    b, s, d = x.shape
    nw = w_main.shape[1]
    gg_w = nw - PM_W - PA_W
    grows = wgt.shape[0]
    kern = functools.partial(_in_proj_kernel, col_chunk=512)
    return pl.pallas_call(
        kern,
        out_shape=(
            jax.ShapeDtypeStruct((b, s, PM_W), BF16),
            jax.ShapeDtypeStruct((b, s, PA_W), BF16),
            jax.ShapeDtypeStruct((b, s, gg_w), BF16),
            jax.ShapeDtypeStruct((b, grows, s), F32),
        ),
        grid=(b, s // tm),
        in_specs=[
            pl.BlockSpec((None, tm, d), lambda bi, i: (bi, i, 0)),
            pl.BlockSpec((None, 6, d), lambda bi, i: (bi, 0, 0)),
            pl.BlockSpec((1, d), lambda bi, i: (0, 0)),
            pl.BlockSpec((d, nw), lambda bi, i: (0, 0)),
            pl.BlockSpec((1, nw), lambda bi, i: (0, 0)),
            pl.BlockSpec((grows, d), lambda bi, i: (0, 0)),
            pl.BlockSpec((grows, 1), lambda bi, i: (0, 0)),
        ],
        out_specs=(
            pl.BlockSpec((None, tm, PM_W), lambda bi, i: (bi, i, 0)),
            pl.BlockSpec((None, tm, PA_W), lambda bi, i: (bi, i, 0)),
            pl.BlockSpec((None, tm, gg_w), lambda bi, i: (bi, i, 0)),
            pl.BlockSpec((None, grows, tm), lambda bi, i: (bi, 0, i)),
        ),
        compiler_params=_cparams(("parallel", "parallel"), VMEM_LIMIT),
        name="in_proj",
    )(x, mod_l, g1, w_main, b_main, wgt, bgt)


def _mlstm_kernel(q_ref, k_ref, v_ref, o_ref, gt_ref, cwq_ref, cwk_ref, cbq_ref, cbk_ref, ng_ref,
                  y_ref, xs_ref, qc_ref, kt_ref, tab_ref, col_ref, hs_ref, stf_ref, stb_ref, mf_ref, mb_ref,
                  *, seq, blk):
    L = MLSTM_CHUNK
    nc = seq // L
    half = nc // 2
    pad = 8

    zeros_pad = jnp.zeros((pad, LANES), F32)
    xs_ref[0:pad, :] = zeros_pad
    xs_ref[pad + seq:pad + seq + pad, :] = zeros_pad

    def conv_pass(src_ref, w_ref, b_ref, dst_ref, scale, transposed):
        for r0 in range(0, seq, blk):
            xs_ref[pad + r0:pad + r0 + blk, :] = src_ref[r0:r0 + blk, :].astype(F32)
        for r0 in range(0, seq, blk):
            acc = jnp.zeros((blk, LANES), F32) + b_ref[...]
            for j in range(CONV_W):
                off = pad + r0 + j - CONV_W // 2
                acc = acc + w_ref[j:j + 1, :] * xs_ref[off:off + blk, :]
            yv = acc * jax.nn.sigmoid(acc) * scale
            if transposed:
                dst_ref[:, r0:r0 + blk] = jnp.transpose(yv).astype(BF16)
            else:
                dst_ref[r0:r0 + blk, :] = yv.astype(BF16)

    conv_pass(q_ref, cwq_ref, cbq_ref, qc_ref, MLSTM_HD ** -0.5, False)
    conv_pass(k_ref, cwk_ref, cbk_ref, kt_ref, 1.0, True)

    jj = lax.broadcasted_iota(jnp.int32, (L, L), 0)
    ll = lax.broadcasted_iota(jnp.int32, (L, L), 1)
    tri = jnp.concatenate([jnp.where(jj <= ll, 1.0, 0.0), jnp.where(jj >= ll, 1.0, 0.0)], axis=1).astype(BF16)

    def split3(x):
        hi = x.astype(BF16)
        r1 = x - hi.astype(F32)
        mid = r1.astype(BF16)
        lo = (r1 - mid.astype(F32)).astype(BF16)
        return jnp.concatenate([hi, mid, lo], axis=0)

    gate_unroll = 4 if nc % 4 == 0 else 1

    def gate_body(i, carry):
        t0s = [pl.multiple_of((i * gate_unroll + u) * L, L) for u in range(gate_unroll)]
        gs = [gt_ref[:, pl.ds(t0, L)] for t0 in t0s]
        c3s = [jnp.dot(split3(jax.nn.log_sigmoid(g)), tri, preferred_element_type=F32) for g in gs]
        row = lax.broadcasted_iota(jnp.int32, (8, L), 0)
        tabs = []
        for t0, g, c3 in zip(t0s, gs, c3s):
            c2 = c3[0:8] + c3[8:16] + c3[16:24]
            cf, cb = c2[:, 0:L], c2[:, L:2 * L]
            tab = jnp.where(row == 0, g, jnp.where(row == 1, cf, jnp.where(row == 2, g, cb)))
            tab = jnp.where(row < 4, tab, 0.0)
            tab_ref[:, pl.ds(t0, L)] = tab
            tabs.append(tab)
        colfs = [jnp.transpose(jnp.concatenate([tab, jnp.zeros((L - 8, L), F32)], axis=0)) for tab in tabs]
        for t0, colf in zip(t0s, colfs):
            hi = colf.astype(BF16).astype(F32)
            r1 = colf - hi
            mid = r1.astype(BF16).astype(F32)
            lo = (r1 - mid).astype(BF16).astype(F32)
            col_ref[pl.ds(t0, L), :] = (hi + pltpu.roll(mid, 8, axis=1) + pltpu.roll(lo, 16, axis=1)).astype(BF16)
        return carry

    lax.fori_loop(0, nc // gate_unroll, gate_body, 0)

    srow = lax.broadcasted_iota(jnp.int32, (L, 4 * L), 0)
    scol = lax.broadcasted_iota(jnp.int32, (L, 4 * L), 1) // L
    sel3 = jnp.where((srow < 24) & (srow % 8 == scol), 1.0, 0.0).astype(BF16)

    stf_ref[...] = jnp.zeros_like(stf_ref)
    stb_ref[...] = jnp.zeros_like(stb_ref)
    mf_ref[...] = jnp.zeros_like(mf_ref)
    mb_ref[...] = jnp.zeros_like(mb_ref)
    ones_blk = jnp.ones((L, L), BF16)

    def stage_a(t0, fwd):
        c = dict(t0=t0, fwd=fwd)
        c["q"] = qc_ref[pl.ds(t0, L), :]
        c["kt"] = kt_ref[:, pl.ds(t0, L)]
        c["v"] = v_ref[pl.ds(t0, L), :]
        c["cols"] = jnp.dot(col_ref[pl.ds(t0, L), :], sel3, preferred_element_type=F32)
        c["qk"] = jnp.dot(c["q"], c["kt"], preferred_element_type=F32)
        return c

    def stage_b(c):
        r8 = tab_ref[:, pl.ds(c["t0"], L)]
        cols = c["cols"]
        if c["fwd"]:
            li_c, b_c = cols[:, 0:L], cols[:, L:2 * L]
            li_r, b_r = r8[0:1, :], r8[1:2, :]
            bl = b_c[L - 1:L, :]
            mask = ll <= jj
        else:
            li_c, b_c = cols[:, 2 * L:3 * L], cols[:, 3 * L:4 * L]
            li_r, b_r = r8[2:3, :], r8[3:4, :]
            bl = b_c[0:1, :]
            mask = ll >= jj
        a_c = bl - b_c + li_c
        a_max = jnp.max(a_c, axis=0, keepdims=True)
        w_c = jnp.exp(a_c - a_max)
        vw = (c["v"].astype(F32) * w_c).astype(BF16)
        c["cat"] = jnp.concatenate([vw, w_c.astype(BF16)], axis=1)
        d = jnp.where(mask, b_c + (li_r - b_r), -jnp.inf)
        c.update(b_c=b_c, bl=bl, a_max=a_max, d=d, dmax=jnp.max(d, axis=1, keepdims=True))
        del c["cols"]

    def stage_c(c):
        c["chunk2"] = jnp.dot(c["kt"], c.pop("cat"), preferred_element_type=F32)

    def stage_d(c, st_ref, m_ref):
        m_prev = m_ref[...]
        c["st"] = st_ref[...]
        m_inter = c["b_c"] + m_prev
        m_out = jnp.maximum(m_inter, c["dmax"])
        c["inter_w"] = jnp.exp(m_inter - m_out)
        c["s"] = (c.pop("qk") * jnp.exp(c["d"] - m_out)).astype(BF16)
        c["m_out"] = m_out
        m_new = jnp.maximum(c["bl"] + m_prev, c["a_max"])
        c["decay"] = jnp.exp(c["bl"] + m_prev - m_new)
        c["inject"] = jnp.exp(c["a_max"] - m_new)
        m_ref[...] = m_new

    def stage_e(c):
        vcat = jnp.concatenate([c["v"], ones_blk], axis=1)
        c["r2"] = jnp.dot(c["s"], vcat, preferred_element_type=F32)
        c["q2"] = jnp.dot(c["q"], c["st"].astype(BF16), preferred_element_type=F32)

    def stage_f(c, st_ref):
        r2, q2, inter_w = c["r2"], c["q2"], c["inter_w"]
        num = r2[:, 0:L] + inter_w * q2[:, 0:L]
        den = r2[:, L:2 * L] + inter_w * q2[:, L:2 * L]
        st_ref[...] = c["decay"][:, 0:1] * c["st"] + c["inject"][:, 0:1] * c["chunk2"]
        return num / jnp.maximum(jnp.abs(den), jnp.exp(-c["m_out"]))

    def finalize(t0, h):
        yv = _rms(h, ng_ref[...]) * jax.nn.sigmoid(o_ref[pl.ds(t0, L), :].astype(F32))
        y_ref[pl.ds(t0, L), :] = yv.astype(y_ref.dtype)

    unroll = 4 if half % 4 == 0 else 1

    def body(first_chunk, emit):
        chunks = []
        for u in range(unroll):
            cidx = first_chunk + u
            chunks.append((stage_a(pl.multiple_of(cidx * L, L), True),
                           stage_a(pl.multiple_of((nc - 1 - cidx) * L, L), False)))
        for pair in chunks:
            for c in pair:
                stage_b(c)
        for pair in chunks:
            for c in pair:
                stage_c(c)
        for cf, cb in chunks:
            stage_d(cf, stf_ref, mf_ref)
            stage_d(cb, stb_ref, mb_ref)
            stage_e(cf)
            stage_e(cb)
            emit(cf["t0"], stage_f(cf, stf_ref))
            emit(cb["t0"], stage_f(cb, stb_ref))

    def keep(t0, h):
        hs_ref[pl.ds(t0, L), :] = h

    def add_finalize(t0, h):
        finalize(t0, hs_ref[pl.ds(t0, L), :] + h)

    def first_half(i, carry):
        body(i * unroll, keep)
        return carry

    def second_half(i, carry):
        body(half + i * unroll, add_finalize)
        return carry

    lax.fori_loop(0, half // unroll, first_half, 0)
    lax.fori_loop(0, half // unroll, second_half, 0)


def _mlstm(pm, gt, conv_w, conv_b, norm_g):
    b, s, _ = pm.shape
    nh = MLSTM_HEADS
    assert s % (2 * MLSTM_CHUNK) == 0
    blk = 512 if s % 512 == 0 else MLSTM_CHUNK
    kern = functools.partial(_mlstm_kernel, seq=s, blk=blk)
    L = MLSTM_CHUNK
    col = lambda off: (lambda bi, h: (bi, 0, off + h))
    return pl.pallas_call(
        kern,
        out_shape=jax.ShapeDtypeStruct((b, s, MLSTM_W), BF16),
        grid=(b, nh),
        in_specs=[
            pl.BlockSpec((None, s, LANES), col(0)),
            pl.BlockSpec((None, s, LANES), col(nh)),
            pl.BlockSpec((None, s, LANES), col(2 * nh)),
            pl.BlockSpec((None, s, LANES), col(3 * nh)),
            pl.BlockSpec((None, 8, s), lambda bi, h: (bi, h, 0)),
            pl.BlockSpec((CONV_W, LANES), lambda bi, h: (0, h)),
            pl.BlockSpec((CONV_W, LANES), lambda bi, h: (0, nh + h)),
            pl.BlockSpec((1, LANES), lambda bi, h: (0, h)),
            pl.BlockSpec((1, LANES), lambda bi, h: (0, nh + h)),
            pl.BlockSpec((1, LANES), lambda bi, h: (0, h)),
        ],
        out_specs=pl.BlockSpec((None, s, LANES), lambda bi, h: (bi, 0, h)),
        scratch_shapes=[
            pltpu.VMEM((s + 16, LANES), F32),
            pltpu.VMEM((s, LANES), BF16),
            pltpu.VMEM((LANES, s), BF16),
            pltpu.VMEM((8, s), F32),
            pltpu.VMEM((s, LANES), BF16),
            pltpu.VMEM((s, LANES), F32),
            pltpu.VMEM((L, 2 * L), F32),
            pltpu.VMEM((L, 2 * L), F32),
            pltpu.VMEM((1, L), F32),
            pltpu.VMEM((1, L), F32),
        ],
        compiler_params=_cparams(("parallel", "parallel"), VMEM_LIMIT),
        name="mlstm",
    )(pm, pm, pm, pm, gt, conv_w, conv_w, conv_b, conv_b, norm_g)


def _swap16(y, lane):
    fwd = pltpu.roll(y, LANES - 16, axis=1)
    bwd = pltpu.roll(y, 16, axis=1)
    return jnp.where((lane % 32) < 16, fwd, bwd)


def _nt(a, b):
    return lax.dot_general(a, b, (((1,), (1,)), ((), ())), preferred_element_type=F32)


def _attn_prep_kernel(pa_ref, cos_ref, sin_ref, qg_ref, kg_ref, mseg_ref, qt_ref, kp_ref, vt_ref, *, q_scale):
    tm = pa_ref.shape[0]
    lane = lax.broadcasted_iota(jnp.int32, (tm, LANES), 1)
    cos = cos_ref[...]
    sin = sin_ref[...]
    mseg = mseg_ref[...]
    r = lax.broadcasted_iota(jnp.int32, (LANES, LANES), 0)
    c = lax.broadcasted_iota(jnp.int32, (LANES, LANES), 1)
    sel_lo = jnp.where((r == c) & (r < ATTN_HD), 1.0, 0.0).astype(BF16)
    sel_hi = jnp.where((c == r + ATTN_HD) & (r < ATTN_HD), 1.0, 0.0).astype(BF16)
    ones_row = jnp.where(lax.broadcasted_iota(jnp.int32, (LANES, tm), 0) == ATTN_HD, 1.0, 0.0)

    def norm_rope(x, g, scale):
        xx = x * x
        hi = xx.astype(BF16)
        lo = (xx - hi.astype(F32)).astype(BF16)
        ms = jnp.dot(jnp.concatenate([hi, lo], axis=1), mseg, preferred_element_type=F32)
        y = x * lax.rsqrt(ms + NORM_EPS) * g
        return (y * cos + _swap16(y, lane) * sin) * scale

    for j in range(ATTN_W // LANES):
        x = pa_ref[:, j * LANES:(j + 1) * LANES].astype(F32)
        y = norm_rope(x, qg_ref[...], q_scale).astype(BF16)
        qt_ref[2 * j] = _nt(sel_lo, y).astype(BF16)
        qt_ref[2 * j + 1] = _nt(sel_hi, y).astype(BF16)
    for j in range(KV_W // LANES):
        x = pa_ref[:, ATTN_W + j * LANES:ATTN_W + (j + 1) * LANES].astype(F32)
        y = norm_rope(x, kg_ref[...], 1.0)
        kp_ref[2 * j] = jnp.where(lane < ATTN_HD, y, 0.0).astype(BF16)
        kp_ref[2 * j + 1] = jnp.where(lane < ATTN_HD, pltpu.roll(y, ATTN_HD, axis=1), 0.0).astype(BF16)
        xv = pa_ref[:, ATTN_W + KV_W + j * LANES:ATTN_W + KV_W + (j + 1) * LANES]
        vt_ref[2 * j] = (_nt(sel_lo, xv) + ones_row).astype(BF16)
        vt_ref[2 * j + 1] = (_nt(sel_hi, xv) + ones_row).astype(BF16)


def _attn_prep(pa, cos_t, sin_t, qg, kg, mseg, *, tm):
    b, s, _ = pa.shape
    q_scale = (ATTN_HD ** -0.5) * math.log2(math.e)
    kern = functools.partial(_attn_prep_kernel, q_scale=q_scale)
    return pl.pallas_call(
        kern,
        out_shape=(
            jax.ShapeDtypeStruct((b, ATTN_HEADS, LANES, s), BF16),
            jax.ShapeDtypeStruct((b, KV_HEADS, s, LANES), BF16),
            jax.ShapeDtypeStruct((b, KV_HEADS, LANES, s), BF16),
        ),
        grid=(b, s // tm),
        in_specs=[
            pl.BlockSpec((None, tm, PA_W), lambda bi, i: (bi, i, 0)),
            pl.BlockSpec((tm, LANES), lambda bi, i: (i, 0)),
            pl.BlockSpec((tm, LANES), lambda bi, i: (i, 0)),
            pl.BlockSpec((1, LANES), lambda bi, i: (0, 0)),
            pl.BlockSpec((1, LANES), lambda bi, i: (0, 0)),
            pl.BlockSpec((2 * LANES, LANES), lambda bi, i: (0, 0)),
        ],
        out_specs=(
            pl.BlockSpec((None, ATTN_HEADS, LANES, tm), lambda bi, i: (bi, 0, 0, i)),
            pl.BlockSpec((None, KV_HEADS, tm, LANES), lambda bi, i: (bi, 0, i, 0)),
            pl.BlockSpec((None, KV_HEADS, LANES, tm), lambda bi, i: (bi, 0, 0, i)),
        ),
        compiler_params=_cparams(("parallel", "parallel"), VMEM_LIMIT),
        name="attn_prep",
    )(pa, cos_t, sin_t, qg, kg, mseg)


def _attn_kernel(qt_ref, k_ref, vt_ref, o_ref, m_ref, acc_ref, s_ref, *, tk):
    g, _, tq = qt_ref.shape
    s = k_ref.shape[0]
    n = s // tk
    m_ref[...] = jnp.full(m_ref.shape, -jnp.inf, F32)
    acc_ref[...] = jnp.zeros(acc_ref.shape, F32)

    def scores(j, slot, h):
        t0 = pl.multiple_of(j * tk, tk)
        s_ref[slot, :, h * tq:(h + 1) * tq] = jnp.dot(k_ref[pl.ds(t0, tk), :], qt_ref[h],
                                                      preferred_element_type=F32)

    def softmax(slot, h):
        cols = slice(h * tq, (h + 1) * tq)
        sc = s_ref[slot, :, cols]
        m_prev = m_ref[:, cols]
        m_new = jnp.maximum(m_prev, jnp.max(sc, axis=0, keepdims=True))
        m_ref[:, cols] = m_new
        return jnp.exp2(m_prev - m_new), jnp.exp2(sc - m_new).astype(BF16)

    def pv(j, h, alpha, p):
        t0 = pl.multiple_of(j * tk, tk)
        cols = slice(h * tq, (h + 1) * tq)
        acc_ref[:, cols] = alpha * acc_ref[:, cols] + jnp.dot(vt_ref[0:PV_ROWS, pl.ds(t0, tk)], p,
                                                              preferred_element_type=F32)

    def step(j, slot, prefetch):
        pending = None
        for h in range(g):
            if prefetch:
                scores(j + 1, 1 - slot, h)
            current = softmax(slot, h)
            if pending is not None:
                pv(j, h - 1, *pending)
            pending = current
        pv(j, g - 1, *pending)

    for h in range(g):
        scores(0, 0, h)

    unroll = 4 if n % 4 == 0 else 2

    def body(i, carry):
        for u in range(unroll):
            step(unroll * i + u, u % 2, True)
        return carry

    lax.fori_loop(0, n // unroll - 1, body, 0)
    for u in range(unroll):
        step(n - unroll + u, u % 2, u < unroll - 1)
    acc = acc_ref[...]
    o = (acc[0:ATTN_HD, :] / acc[ATTN_HD:ATTN_HD + 1, :]).astype(BF16)
    r = lax.broadcasted_iota(jnp.int32, (LANES, LANES), 0)
    c = lax.broadcasted_iota(jnp.int32, (LANES, LANES), 1)
    eye = jnp.where(r == c, 1.0, 0.0).astype(BF16)
    for pair in range(g // 2):
        rows = jnp.concatenate([o[:, (2 * pair) * tq:(2 * pair + 1) * tq],
                                o[:, (2 * pair + 1) * tq:(2 * pair + 2) * tq]], axis=0)
        out = lax.dot_general(rows, eye, (((0,), (0,)), ((), ())), preferred_element_type=F32)
        o_ref[:, pair * LANES:(pair + 1) * LANES] = out.astype(o_ref.dtype)


def _attention(qt, kp, vt, *, tq, tk):
    b, nh, _, s = qt.shape
    g = nh // KV_HEADS
    assert (s // tk) % 2 == 0 and g % 2 == 0
    kern = functools.partial(_attn_kernel, tk=tk)
    return pl.pallas_call(
        kern,
        out_shape=jax.ShapeDtypeStruct((b, s, ATTN_W), BF16),
        grid=(b, KV_HEADS, s // tq),
        in_specs=[
            pl.BlockSpec((None, g, LANES, tq), lambda bi, kv, i: (bi, kv, 0, i)),
            pl.BlockSpec((None, None, s, LANES), lambda bi, kv, i: (bi, kv, 0, 0)),
            pl.BlockSpec((None, None, LANES, s), lambda bi, kv, i: (bi, kv, 0, 0)),
        ],
        out_specs=pl.BlockSpec((None, tq, g * ATTN_HD), lambda bi, kv, i: (bi, i, kv)),
        scratch_shapes=[
            pltpu.VMEM((1, g * tq), F32),
            pltpu.VMEM((PV_ROWS, g * tq), F32),
            pltpu.VMEM((2, tk, g * tq), F32),
        ],
        compiler_params=_cparams(("parallel", "parallel", "parallel"), VMEM_LIMIT),
        name="attn",
    )(qt, kp, vt)


def _merge_kernel(x_ref, ym_ref, ya_ref, gg_ref, mod_ref, wbm_ref, wba_ref, wo_ref, g2_ref, wr_ref, br_ref,
                  xn_ref, h2_ref, lt_ref):
    d = x_ref.shape[-1]
    a = jnp.dot(ym_ref[...], wbm_ref[...], preferred_element_type=F32)
    bm = jnp.dot(ya_ref[...], wba_ref[...], preferred_element_type=F32)
    gm = jax.nn.sigmoid(gg_ref[:, 0:d].astype(F32))
    ga = jax.nn.sigmoid(gg_ref[:, d:2 * d].astype(F32))
    merged = (gm * a + ga * bm).astype(BF16)
    u = jnp.dot(merged, wo_ref[...], preferred_element_type=F32)
    xn = x_ref[...] + mod_ref[2:3, :] * u
    xn_ref[...] = xn
    h2 = _rms(xn, g2_ref[...]) * (1.0 + mod_ref[4:5, :]) + mod_ref[3:4, :]
    h2_ref[...] = h2.astype(h2_ref.dtype)
    lt = lax.dot_general(wr_ref[...], h2, (((1,), (1,)), ((), ())), precision=HIGHEST,
                         preferred_element_type=F32)
    lt_ref[...] = lt + br_ref[...]


def _merge(x, ym, ya, gg, mod_l, wbm, wba, wo, g2, wr_t, br_t, *, tm):
    b, s, d = x.shape
    ns = s // tm
    rr = wr_t.shape[0]
    full = lambda shp: pl.BlockSpec(shp, lambda bi, i: tuple(0 for _ in shp))
    return pl.pallas_call(
        _merge_kernel,
        out_shape=(
            jax.ShapeDtypeStruct((b, s, d), F32),
            jax.ShapeDtypeStruct((b * s, d), BF16),
            jax.ShapeDtypeStruct((rr, b * s), F32),
        ),
        grid=(b, ns),
        in_specs=[
            pl.BlockSpec((None, tm, d), lambda bi, i: (bi, i, 0)),
            pl.BlockSpec((None, tm, MLSTM_W), lambda bi, i: (bi, i, 0)),
            pl.BlockSpec((None, tm, ATTN_W), lambda bi, i: (bi, i, 0)),
            pl.BlockSpec((None, tm, 2 * d), lambda bi, i: (bi, i, 0)),
            pl.BlockSpec((None, 6, d), lambda bi, i: (bi, 0, 0)),
            full((MLSTM_W, d)),
            full((ATTN_W, d)),
            full((d, d)),
            full((1, d)),
            full((rr, d)),
            full((rr, 1)),
        ],
        out_specs=(
            pl.BlockSpec((None, tm, d), lambda bi, i: (bi, i, 0)),
            pl.BlockSpec((tm, d), lambda bi, i: (bi * ns + i, 0)),
            pl.BlockSpec((rr, tm), lambda bi, i: (0, bi * ns + i)),
        ),
        compiler_params=_cparams(("parallel", "parallel"), VMEM_LIMIT),
        name="merge",
    )(x, ym, ya, gg, mod_l, wbm, wba, wo, g2, wr_t, br_t)


def _route_kernel(lt_ref, pos_ref, wts_ref, runs_ref, carry_ref):
    tb = lt_ref.shape[1]
    epg = EXPERTS_PER_GROUP

    @pl.when(pl.program_id(0) == 0)
    def _():
        carry_ref[...] = jnp.zeros_like(carry_ref)

    row8 = lax.broadcasted_iota(jnp.int32, (8, tb), 0)
    gl = jnp.where(row8 < N_GROUPS, lt_ref[0:8, :], -jnp.inf)
    ge = jnp.exp(gl - jnp.max(gl, axis=0, keepdims=True))
    pg = ge / jnp.sum(ge, axis=0, keepdims=True)
    p_top = jnp.max(pg, axis=0, keepdims=True)
    g_idx = jnp.min(jnp.where(pg == p_top, row8, 8), axis=0, keepdims=True)

    el = jnp.zeros((epg, tb), F32)
    for g in range(N_GROUPS):
        el = jnp.where(g_idx == g, lt_ref[8 + g * epg:8 + (g + 1) * epg, :], el)
    ee = jnp.exp(el - jnp.max(el, axis=0, keepdims=True))
    pe = ee / jnp.sum(ee, axis=0, keepdims=True)
    v1 = jnp.max(pe, axis=0, keepdims=True)
    i1 = jnp.min(jnp.where(pe == v1, row8, 8), axis=0, keepdims=True)
    pe2 = jnp.where(row8 == i1, -1.0, pe)
    v2 = jnp.max(pe2, axis=0, keepdims=True)
    i2 = jnp.min(jnp.where(pe2 == v2, row8, 8), axis=0, keepdims=True)
    denom = v1 + v2
    w0 = v1 / denom * p_top
    w1 = v2 / denom * p_top
    e0 = g_idx * epg + i1
    e1 = g_idx * epg + i2

    rowe = lax.broadcasted_iota(jnp.int32, (N_EXPERTS, tb), 0)
    oh0 = rowe == e0
    oh1 = rowe == e1
    oh = jnp.where(oh0 | oh1, 1.0, 0.0)
    src = lax.broadcasted_iota(jnp.int32, (tb, tb), 0)
    dst = lax.broadcasted_iota(jnp.int32, (tb, tb), 1)
    strict = jnp.where(src < dst, 1.0, 0.0).astype(BF16)
    cum = jnp.dot(oh.astype(BF16), strict, preferred_element_type=F32)
    cnt_col = jnp.sum(oh, axis=1, keepdims=True)
    seg_col = jnp.floor((cnt_col + (SEG_ALIGN - 1.0)) * (1.0 / SEG_ALIGN)) * SEG_ALIGN
    er = lax.broadcasted_iota(jnp.int32, (N_EXPERTS, N_EXPERTS), 0)
    ec = lax.broadcasted_iota(jnp.int32, (N_EXPERTS, N_EXPERTS), 1)
    before = jnp.where(ec < er, 1.0, 0.0).astype(BF16)
    start_col = jnp.dot(before, jnp.broadcast_to(seg_col, (N_EXPERTS, LANES)).astype(BF16),
                        preferred_element_type=F32)[:, 0:1]
    base = start_col + cum
    p0 = jnp.sum(jnp.where(oh0, base, 0.0), axis=0, keepdims=True)
    p1 = jnp.sum(jnp.where(oh1, base, 0.0), axis=0, keepdims=True)
    pos_ref[...] = jnp.where(row8 == 0, p0, jnp.where(row8 == 1, p1, 0.0)).astype(jnp.int32)
    wts_ref[...] = jnp.where(row8 == 0, w0, jnp.where(row8 == 1, w1, 0.0))

    ohp = jnp.concatenate([oh, jnp.zeros((LANES - N_EXPERTS, tb), F32)], axis=0).astype(BF16)
    cnt_row = _nt(jnp.ones((8, tb), BF16), ohp)
    seg_row = jnp.floor((cnt_row + (SEG_ALIGN - 1.0)) * (1.0 / SEG_ALIGN)) * SEG_ALIGN
    carry = carry_ref[...]
    runs_ref[0] = (seg_row + pltpu.roll(carry, N_EXPERTS, axis=1)).astype(jnp.int32)
    carry_ref[...] = carry + seg_row


def _route(lt, *, tb):
    rr, t = lt.shape
    nblk = t // tb
    return pl.pallas_call(
        _route_kernel,
        out_shape=(
            jax.ShapeDtypeStruct((8, t), jnp.int32),
            jax.ShapeDtypeStruct((8, t), F32),
            jax.ShapeDtypeStruct((nblk, 8, LANES), jnp.int32),
        ),
        grid=(nblk,),
        in_specs=[pl.BlockSpec((rr, tb), lambda i: (0, i))],
        out_specs=(
            pl.BlockSpec((8, tb), lambda i: (0, i)),
            pl.BlockSpec((8, tb), lambda i: (0, i)),
            pl.BlockSpec((1, 8, LANES), lambda i: (i, 0, 0)),
        ),
        scratch_shapes=[pltpu.VMEM((8, LANES), F32)],
        compiler_params=_cparams(("arbitrary",), VMEM_LIMIT),
        name="route",
    )(lt)


def _segment_piece(local_ref, hbm_ref, sem, s, d, rows, to_hbm):
    loc = local_ref.at[pl.ds(pl.multiple_of(s, SEG_ALIGN), rows)]
    far = hbm_ref.at[pl.ds(pl.multiple_of(d, SEG_ALIGN), rows)]
    return pltpu.make_async_copy(loc, far, sem) if to_hbm else pltpu.make_async_copy(far, loc, sem)


def _segment_starts(blk, runs_ref, off_ref, local_ref, hbm_ref, sem, cnt_ref, slot, *, to_hbm):
    big = 2 * SEG_ALIGN

    def expert(e, carry):
        s0, total_big, total_small = carry
        n = runs_ref[blk * 2 * N_EXPERTS + e]
        d0 = off_ref[e] + runs_ref[blk * 2 * N_EXPERTS + N_EXPERTS + e]
        nbig = n >> (SEG_SHIFT + 1)
        nsmall = (n >> SEG_SHIFT) & 1

        def big_piece(c, carry):
            _segment_piece(local_ref, hbm_ref, sem, s0 + c * big, d0 + c * big, big, to_hbm).start()
            return carry

        def small_piece(c, carry):
            _segment_piece(local_ref, hbm_ref, sem, s0 + nbig * big + c * SEG_ALIGN, d0 + nbig * big + c * SEG_ALIGN,
                           SEG_ALIGN, to_hbm).start()
            return carry

        lax.fori_loop(0, nbig, big_piece, 0)
        lax.fori_loop(0, nsmall, small_piece, 0)
        return s0 + n, total_big + nbig, total_small + nsmall

    zero = jnp.int32(0)
    _, total_big, total_small = lax.fori_loop(0, N_EXPERTS, expert, (zero, zero, zero))
    cnt_ref[slot, 0] = total_big
    cnt_ref[slot, 1] = total_small


def _segment_waits(local_ref, hbm_ref, sem, cnt_ref, slot, *, to_hbm):
    def wait_big(c, carry):
        _segment_piece(local_ref, hbm_ref, sem, 0, 0, 2 * SEG_ALIGN, to_hbm).wait()
        return carry

    def wait_small(c, carry):
        _segment_piece(local_ref, hbm_ref, sem, 0, 0, SEG_ALIGN, to_hbm).wait()
        return carry

    lax.fori_loop(0, cnt_ref[slot, 0], wait_big, 0)
    lax.fori_loop(0, cnt_ref[slot, 1], wait_small, 0)


def _dispatch_kernel(runs_ref, off_ref, last_ref, h_ref, pos_ref, xg_ref, sbuf_ref, zero_ref, cnt_ref, sem, zsem, *, tme):
    i = pl.program_id(0)
    nblk = pl.num_programs(0)
    slot = i % 2
    tb = h_ref.shape[0]
    rp = sbuf_ref.shape[1]

    @pl.when(i == 0)
    def _():
        zero_ref[...] = jnp.zeros_like(zero_ref)
        for wait in (False, True):
            def tail_tile(j, carry, wait=wait):
                cp = pltpu.make_async_copy(zero_ref, xg_ref.at[pl.ds(pl.multiple_of(j * tme, tme), tme)], zsem)
                if wait:
                    cp.wait()
                else:
                    cp.start()
                return carry

            lax.fori_loop(last_ref[N_EXPERTS], xg_ref.shape[0] // tme, tail_tile, 0)
            for e in range(N_EXPERTS):
                @pl.when(last_ref[e] >= 0)
                def _():
                    cp = pltpu.make_async_copy(
                        zero_ref, xg_ref.at[pl.ds(pl.multiple_of(last_ref[e], SEG_ALIGN), tme)], zsem)
                    if wait:
                        cp.wait()
                    else:
                        cp.start()

    @pl.when(i >= 2)
    def _():
        _segment_waits(sbuf_ref.at[slot], xg_ref, sem.at[slot], cnt_ref, slot, to_hbm=True)

    ii = lax.broadcasted_iota(jnp.int32, (rp, tb), 0)
    perm = jnp.where((ii == pos_ref[0:1, :]) | (ii == pos_ref[1:2, :]), 1.0, 0.0).astype(BF16)
    sbuf_ref[slot] = jnp.dot(perm, h_ref[...], preferred_element_type=F32).astype(BF16)
    _segment_starts(i, runs_ref, off_ref, sbuf_ref.at[slot], xg_ref, sem.at[slot], cnt_ref, slot, to_hbm=True)

    @pl.when(i == nblk - 1)
    def _():
        _segment_waits(sbuf_ref.at[slot], xg_ref, sem.at[slot], cnt_ref, slot, to_hbm=True)

        @pl.when(i >= 1)
        def _():
            _segment_waits(sbuf_ref.at[1 - slot], xg_ref, sem.at[1 - slot], cnt_ref, 1 - slot, to_hbm=True)


def _dispatch(runs, offsets, last_tile, h2, pos, p_rows, *, tb, tme):
    t, d = h2.shape
    rp = 2 * tb + SEG_ALIGN * N_EXPERTS
    kern = functools.partial(_dispatch_kernel, tme=tme)
    return pl.pallas_call(
        kern,
        out_shape=jax.ShapeDtypeStruct((p_rows, d), BF16),
        grid_spec=pltpu.PrefetchScalarGridSpec(
            num_scalar_prefetch=3,
            grid=(t // tb,),
            in_specs=[
                pl.BlockSpec((tb, d), lambda i, *_: (i, 0)),
                pl.BlockSpec((8, tb), lambda i, *_: (0, i)),
            ],
            out_specs=pl.BlockSpec(memory_space=pl.ANY),
            scratch_shapes=[
                pltpu.VMEM((2, rp, d), BF16),
                pltpu.VMEM((tme, d), BF16),
                pltpu.SMEM((2, 2), jnp.int32),
                pltpu.SemaphoreType.DMA((2,)),
                pltpu.SemaphoreType.DMA(()),
            ],
        ),
        compiler_params=_cparams(("arbitrary",), VMEM_LIMIT),
        name="dispatch",
    )(runs, offsets, last_tile, h2, pos)


def _experts_kernel(te_ref, first_ref, used_ref, x_ref, wg_ref, wu_ref, wd_ref, y_ref, wgb, wub, wdb):
    i = pl.program_id(0)

    @pl.when(first_ref[i] == 1)
    def _():
        wgb[...] = wg_ref[...].astype(BF16)
        wub[...] = wu_ref[...].astype(BF16)
        wdb[...] = wd_ref[...].astype(BF16)

    @pl.when(used_ref[i] == 1)
    def _():
        tm = x_ref.shape[0]
        sub = tm // 2 if tm % 512 == 0 else tm
        halves = [slice(r, r + sub) for r in range(0, tm, sub)]
        ups = [(jnp.dot(x_ref[rs, :], wgb[...], preferred_element_type=F32),
                jnp.dot(x_ref[rs, :], wub[...], preferred_element_type=F32)) for rs in halves]
        acts = [(a * jax.nn.sigmoid(a) * u).astype(BF16) for a, u in ups]
        for rs, act in zip(halves, acts):
            y_ref[rs, :] = jnp.dot(act, wdb[...], preferred_element_type=F32).astype(y_ref.dtype)

    @pl.when(used_ref[i] == 0)
    def _():
        y_ref[...] = jnp.zeros_like(y_ref)


def _experts(tile_expert, tile_first, tile_used, xg, w_gate, w_up, w_down, *, layer, tm):
    p_rows, d = xg.shape
    f = w_gate.shape[-1]
    nt = p_rows // tm
    wmap = lambda i, te, fi, us: (layer, te[i], 0, 0)
    return pl.pallas_call(
        _experts_kernel,
        out_shape=jax.ShapeDtypeStruct((p_rows, d), BF16),
        grid_spec=pltpu.PrefetchScalarGridSpec(
            num_scalar_prefetch=3,
            grid=(nt,),
            in_specs=[
                pl.BlockSpec((tm, d), lambda i, te, fi, us: (i, 0)),
                pl.BlockSpec((None, None, d, f), wmap),
                pl.BlockSpec((None, None, d, f), wmap),
                pl.BlockSpec((None, None, f, d), wmap),
            ],
            out_specs=pl.BlockSpec((tm, d), lambda i, te, fi, us: (i, 0)),
            scratch_shapes=[
                pltpu.VMEM((d, f), BF16),
                pltpu.VMEM((d, f), BF16),
                pltpu.VMEM((f, d), BF16),
            ],
        ),
        compiler_params=_cparams(("arbitrary",), VMEM_LIMIT),
        name="experts",
    )(tile_expert, tile_first, tile_used, xg, w_gate, w_up, w_down)


def _combine_kernel(runs_ref, off_ref, x_ref, pos_ref, w_ref, mod_ref, fg_ref, y_ref, o_ref, ybuf_ref, cnt_ref, sem, *, final):
    blk = pl.program_id(0) * pl.num_programs(1) + pl.program_id(1)
    nblk = pl.num_programs(0) * pl.num_programs(1)
    slot = blk % 2
    tb = x_ref.shape[0]
    rp = ybuf_ref.shape[1]

    @pl.when(blk == 0)
    def _():
        ybuf_ref[...] = jnp.zeros_like(ybuf_ref)
        _segment_starts(blk, runs_ref, off_ref, ybuf_ref.at[slot], y_ref, sem.at[slot], cnt_ref, slot, to_hbm=False)

    @pl.when(blk + 1 < nblk)
    def _():
        _segment_starts(blk + 1, runs_ref, off_ref, ybuf_ref.at[1 - slot], y_ref, sem.at[1 - slot], cnt_ref, 1 - slot,
                        to_hbm=False)

    _segment_waits(ybuf_ref.at[slot], y_ref, sem.at[slot], cnt_ref, slot, to_hbm=False)
    li = lax.broadcasted_iota(jnp.int32, (tb, rp), 1)
    wmat = (jnp.where(li == pos_ref[:, 0:1], w_ref[:, 0:1], 0.0)
            + jnp.where(li == pos_ref[:, 1:2], w_ref[:, 1:2], 0.0))
    moe = jnp.dot(wmat.astype(BF16), ybuf_ref[slot], preferred_element_type=F32)
    xo = x_ref[...] + mod_ref[5:6, :] * moe
    if final:
        xo = _rms(xo, fg_ref[...])
    o_ref[...] = xo


def _combine(runs, offsets, xn, pos_col, wts_col, mod_l, fg, y, *, tb, final):
    b, s, d = xn.shape
    ns = s // tb
    rp = 2 * tb + SEG_ALIGN * N_EXPERTS
    kern = functools.partial(_combine_kernel, final=final)
    return pl.pallas_call(
        kern,
        out_shape=jax.ShapeDtypeStruct((b, s, d), F32),
        grid_spec=pltpu.PrefetchScalarGridSpec(
            num_scalar_prefetch=2,
            grid=(b, ns),
            in_specs=[
                pl.BlockSpec((None, tb, d), lambda bi, i, *_: (bi, i, 0)),
                pl.BlockSpec((tb, 8), lambda bi, i, *_: (bi * ns + i, 0)),
                pl.BlockSpec((tb, 8), lambda bi, i, *_: (bi * ns + i, 0)),
                pl.BlockSpec((None, 6, d), lambda bi, i, *_: (bi, 0, 0)),
                pl.BlockSpec((1, d), lambda bi, i, *_: (0, 0)),
                pl.BlockSpec(memory_space=pl.ANY),
            ],
            out_specs=pl.BlockSpec((None, tb, d), lambda bi, i, *_: (bi, i, 0)),
            scratch_shapes=[
                pltpu.VMEM((2, rp, d), BF16),
                pltpu.SMEM((2, 2), jnp.int32),
                pltpu.SemaphoreType.DMA((2,)),
            ],
        ),
        compiler_params=_cparams(("arbitrary", "arbitrary"), VMEM_LIMIT),
        name="combine",
    )(runs, offsets, xn, pos_col, wts_col, mod_l, fg, y)


def _rope_tables(seq):
    rows = seq // GRID_W
    row = jnp.repeat(jnp.arange(rows, dtype=F32), GRID_W)
    col = jnp.tile(jnp.arange(GRID_W, dtype=F32), rows)
    half = ATTN_HD // 2
    inv_freq = ROPE_BASE ** (-jnp.arange(0, half, 2, dtype=F32) / half)
    ang_r = row[:, None] * inv_freq
    ang_c = col[:, None] * inv_freq
    cos64 = jnp.concatenate([jnp.cos(ang_r), jnp.cos(ang_r), jnp.cos(ang_c), jnp.cos(ang_c)], axis=-1)
    sin64 = jnp.concatenate([-jnp.sin(ang_r), jnp.sin(ang_r), -jnp.sin(ang_c), jnp.sin(ang_c)], axis=-1)
    return jnp.tile(cos64, (1, LANES // ATTN_HD)), jnp.tile(sin64, (1, LANES // ATTN_HD))


def _plan_tiles(rows, tm, nt):
    tiles = (rows + tm - 1) // tm
    tile_end = jnp.cumsum(tiles)
    tile_start = tile_end - tiles
    offsets = (tile_start * tm).astype(jnp.int32)
    last_tile = jnp.where(tiles > 0, (tile_end - 1) * tm, -1).astype(jnp.int32)
    last_tile = jnp.concatenate([last_tile, tile_end[-1:].astype(jnp.int32)])
    tidx = jnp.arange(nt, dtype=jnp.int32)
    te = jnp.sum((tidx[:, None] >= tile_end[None, :]).astype(jnp.int32), axis=1)
    used = (tidx < tile_end[-1]).astype(jnp.int32)
    last_used = jnp.sum((tile_end[-1] - 1 >= tile_end).astype(jnp.int32))
    te = jnp.where(used == 1, te, last_used).astype(jnp.int32)
    prev = jnp.concatenate([jnp.full((1,), -1, jnp.int32), te[:-1]])
    first = ((te != prev) & (used == 1)).astype(jnp.int32)
    return offsets, last_tile, te, first, used


def _tiles(s, t):
    tm = 512 if s % 512 == 0 else 128
    tq = 256 if s % 256 == 0 else 128
    tk = 512 if s % 2048 == 0 else 128
    tme = 512
    tbr = 512 if s % 512 == 0 else 128
    nblk = t // tbr
    nt = -(-(2 * t + (SEG_ALIGN - 1) * N_EXPERTS * nblk) // tme) + N_EXPERTS
    return tm, tq, tk, tme, tbr, nt


def kernel(x, c, w_ada, b_ada, norm1_g, w_in, b_in, conv_w, conv_b, mlstm_norm_g, q_norm_g, k_norm_g,
           w_branch_m, w_branch_a, w_out, norm2_g, w_router_group, b_router_group, w_router_expert,
           b_router_expert, w_gate, w_up, w_down, final_norm_g):
    b, s, d = x.shape
    depth = w_ada.shape[0]
    t = b * s
    nh = MLSTM_HEADS
    tm, tq, tk, tme, tbr, nt = _tiles(s, t)
    p_rows = nt * tme

    mod = _adaln(c, w_ada, b_ada).reshape(depth, b, 6, d)
    cos_t, sin_t = _rope_tables(s)
    seg = np.arange(LANES) // ATTN_HD
    mseg = jnp.asarray((seg[:, None] == seg[None, :]).astype(np.float32) / ATTN_HD)
    mseg = jnp.concatenate([mseg, mseg], axis=0).astype(BF16)

    o_g = 4 * MLSTM_W
    o_aq = o_g + 4 * nh

    for l in range(depth):
        w_l = w_in[l]
        w_main = jnp.concatenate([w_l[:, :o_g], w_l[:, o_aq:]], axis=1).astype(BF16)
        b_main = jnp.concatenate([b_in[l][:o_g], b_in[l][o_aq:]])[None, :]
        wg = jnp.transpose(w_l[:, o_g:o_aq].reshape(d, 4, nh), (2, 1, 0))
        wgt = jnp.concatenate([wg, jnp.zeros((nh, 4, d), F32)], axis=1).reshape(nh * 8, d).astype(BF16)
        bg = jnp.transpose(b_in[l][o_g:o_aq].reshape(4, nh))
        bgt = jnp.concatenate([bg, jnp.zeros((nh, 4), F32)], axis=1).reshape(nh * 8, 1)

        pm, pa, gg, gt = _in_proj(x, mod[l], norm1_g[l][None, :], w_main, b_main, wgt, bgt, tm=tm)
        ym = _mlstm(pm, gt, conv_w[l].reshape(CONV_W, 2 * MLSTM_W), conv_b[l][None, :],
                    mlstm_norm_g[l][None, :])
        qg = jnp.tile(q_norm_g[l], LANES // ATTN_HD)[None, :]
        kg = jnp.tile(k_norm_g[l], LANES // ATTN_HD)[None, :]
        qp, kp, vp = _attn_prep(pa, cos_t, sin_t, qg, kg, mseg, tm=tm)
        ya = _attention(qp, kp, vp, tq=tq, tk=tk)

        wr = jnp.concatenate([w_router_group[l], jnp.zeros((d, 8 - N_GROUPS), F32), w_router_expert[l]], axis=1)
        br = jnp.concatenate([b_router_group[l], jnp.zeros((8 - N_GROUPS,), F32), b_router_expert[l]])
        xn, h2, lt = _merge(x, ym, ya, gg, mod[l], w_branch_m[l].astype(BF16), w_branch_a[l].astype(BF16),
                            w_out[l].astype(BF16), norm2_g[l][None, :], jnp.transpose(wr), br[:, None], tm=tm)

        pos, wts, runs = _route(lt, tb=tbr)
        runs = runs[:, 0, :2 * N_EXPERTS]
        rows = runs[-1, :N_EXPERTS] + runs[-1, N_EXPERTS:]
        offsets, last_tile, te, first, used = _plan_tiles(rows, tme, nt)
        runs = runs.reshape(-1)
        xg = _dispatch(runs, offsets, last_tile, h2, pos, p_rows, tb=tbr, tme=tme)
        y = _experts(te, first, used, xg, w_gate, w_up, w_down, layer=l, tm=tme)
        x = _combine(runs, offsets, xn, jnp.transpose(pos), jnp.transpose(wts), mod[l], final_norm_g[None, :], y,
                     tb=tbr, final=(l == depth - 1))
    return x
```

```python
import functools
import math

import jax
import jax.numpy as jnp
import numpy as np
from jax import lax
from jax.experimental import pallas as pl
from jax.experimental.pallas import tpu as pltpu

F32 = jnp.float32
BF16 = jnp.bfloat16
HIGHEST = lax.Precision.HIGHEST

GRID_W = 64
MLSTM_HEADS = 4
MLSTM_HD = 128
MLSTM_W = MLSTM_HEADS * MLSTM_HD
CONV_W = 5
ATTN_HEADS = 8
KV_HEADS = 2
ATTN_HD = 64
ATTN_W = ATTN_HEADS * ATTN_HD
KV_W = KV_HEADS * ATTN_HD
ROPE_BASE = 10000.0
N_GROUPS = 4
EXPERTS_PER_GROUP = 8
N_EXPERTS = N_GROUPS * EXPERTS_PER_GROUP
NORM_EPS = 1e-6

LANES = 128
MLSTM_CHUNK = 128
SEG_ALIGN = 16
SEG_SHIFT = 4
PV_ROWS = ATTN_HD + 16
VMEM_LIMIT = 56 * 1024 * 1024

PM_W = 4 * MLSTM_W
PA_W = ATTN_W + 2 * KV_W
GG_W = None
ROUTER_ROWS = 8 + N_EXPERTS


def _cparams(sem, vmem=None):
    return pltpu.CompilerParams(dimension_semantics=sem, vmem_limit_bytes=vmem)


def _rms(x, g):
    ms = jnp.mean(x * x, axis=-1, keepdims=True)
    return x * lax.rsqrt(ms + NORM_EPS) * g


def _adaln_kernel(c_ref, w_ref, b_ref, o_ref):
    c = c_ref[...]
    cond = c * jax.nn.sigmoid(c)
    o_ref[0] = jnp.dot(cond, w_ref[0], precision=HIGHEST, preferred_element_type=F32) + b_ref[0]


def _adaln(c, w_ada, b_ada):
    depth, d, n = w_ada.shape
    b = c.shape[0]
    tn = 1536 if n % 1536 == 0 else n
    return pl.pallas_call(
        _adaln_kernel,
        out_shape=jax.ShapeDtypeStruct((depth, b, n), F32),
        grid=(depth, n // tn),
        in_specs=[
            pl.BlockSpec((b, d), lambda l, j: (0, 0)),
            pl.BlockSpec((1, d, tn), lambda l, j: (l, 0, j)),
            pl.BlockSpec((1, 1, tn), lambda l, j: (l, 0, j)),
        ],
        out_specs=pl.BlockSpec((1, b, tn), lambda l, j: (l, 0, j)),
        compiler_params=_cparams(("parallel", "parallel"), VMEM_LIMIT),
        name="adaln",
    )(c, w_ada, b_ada.reshape(depth, 1, n))


def _in_proj_kernel(x_ref, mod_ref, g_ref, w_ref, b_ref, wgt_ref, bgt_ref,
                    pm_ref, pa_ref, gg_ref, gt_ref, *, col_chunk):
    x = x_ref[...]
    y = _rms(x, g_ref[...])
    h = y * (1.0 + mod_ref[1:2, :]) + mod_ref[0:1, :]
    hb = h.astype(BF16)
    c0 = 0
    for o_ref in (pm_ref, pa_ref, gg_ref):
        width = o_ref.shape[-1]
        for a in range(0, width, col_chunk):
            e = min(a + col_chunk, width)
            acc = jnp.dot(hb, w_ref[:, c0 + a:c0 + e], preferred_element_type=F32)
            o_ref[:, a:e] = (acc + b_ref[:, c0 + a:c0 + e]).astype(o_ref.dtype)
        c0 += width
    gt = lax.dot_general(wgt_ref[...], hb, (((1,), (1,)), ((), ())), preferred_element_type=F32)
    gt_ref[...] = gt + bgt_ref[...]


def _in_proj(x, mod_l, g1, w_main, b_main, wgt, bgt, *, tm):
    b, s, d = x.shape
    nw = w_main.shape[1]
    gg_w = nw - PM_W - PA_W
    grows = wgt.shape[0]
    kern = functools.partial(_in_proj_kernel, col_chunk=512)
    return pl.pallas_call(
        kern,
        out_shape=(
            jax.ShapeDtypeStruct((b, s, PM_W), BF16),
            jax.ShapeDtypeStruct((b, s, PA_W), BF16),
            jax.ShapeDtypeStruct((b, s, gg_w), BF16),
            jax.ShapeDtypeStruct((b, grows, s), F32),
        ),
        grid=(b, s // tm),
        in_specs=[
            pl.BlockSpec((None, tm, d), lambda bi, i: (bi, i, 0)),
            pl.BlockSpec((None, 6, d), lambda bi, i: (bi, 0, 0)),
            pl.BlockSpec((1, d), lambda bi, i: (0, 0)),
            pl.BlockSpec((d, nw), lambda bi, i: (0, 0)),
            pl.BlockSpec((1, nw), lambda bi, i: (0, 0)),
            pl.BlockSpec((grows, d), lambda bi, i: (0, 0)),
            pl.BlockSpec((grows, 1), lambda bi, i: (0, 0)),
        ],
        out_specs=(
            pl.BlockSpec((None, tm, PM_W), lambda bi, i: (bi, i, 0)),
            pl.BlockSpec((None, tm, PA_W), lambda bi, i: (bi, i, 0)),
            pl.BlockSpec((None, tm, gg_w), lambda bi, i: (bi, i, 0)),
            pl.BlockSpec((None, grows, tm), lambda bi, i: (bi, 0, i)),
        ),
        compiler_params=_cparams(("parallel", "parallel"), VMEM_LIMIT),
        name="in_proj",
    )(x, mod_l, g1, w_main, b_main, wgt, bgt)


def _mlstm_kernel(q_ref, k_ref, v_ref, o_ref, gt_ref, cwq_ref, cwk_ref, cbq_ref, cbk_ref, ng_ref,
                  y_ref, xs_ref, qc_ref, kt_ref, tab_ref, col_ref, hs_ref, stf_ref, stb_ref, mf_ref, mb_ref,
                  *, seq, blk):
    L = MLSTM_CHUNK
    nc = seq // L
    half = nc // 2
    pad = 8

    zeros_pad = jnp.zeros((pad, LANES), F32)
    xs_ref[0:pad, :] = zeros_pad
    xs_ref[pad + seq:pad + seq + pad, :] = zeros_pad

    def conv_pass(src_ref, w_ref, b_ref, dst_ref, scale, transposed):
        for r0 in range(0, seq, blk):
            xs_ref[pad + r0:pad + r0 + blk, :] = src_ref[r0:r0 + blk, :].astype(F32)
        for r0 in range(0, seq, blk):
            acc = jnp.zeros((blk, LANES), F32) + b_ref[...]
            for j in range(CONV_W):
                off = pad + r0 + j - CONV_W // 2
                acc = acc + w_ref[j:j + 1, :] * xs_ref[off:off + blk, :]
            yv = acc * jax.nn.sigmoid(acc) * scale
            if transposed:
                dst_ref[:, r0:r0 + blk] = jnp.transpose(yv).astype(BF16)
            else:
                dst_ref[r0:r0 + blk, :] = yv.astype(BF16)

    conv_pass(q_ref, cwq_ref, cbq_ref, qc_ref, MLSTM_HD ** -0.5, False)
    conv_pass(k_ref, cwk_ref, cbk_ref, kt_ref, 1.0, True)

    jj = lax.broadcasted_iota(jnp.int32, (L, L), 0)
    ll = lax.broadcasted_iota(jnp.int32, (L, L), 1)
    tri = jnp.concatenate([jnp.where(jj <= ll, 1.0, 0.0), jnp.where(jj >= ll, 1.0, 0.0)], axis=1).astype(BF16)

    def split3(x):
        hi = x.astype(BF16)
        r1 = x - hi.astype(F32)
        mid = r1.astype(BF16)
        lo = (r1 - mid.astype(F32)).astype(BF16)
        return jnp.concatenate([hi, mid, lo], axis=0)

    gate_unroll = 8 if nc % 8 == 0 else 1

    def gate_body(i, carry):
        t0s = [pl.multiple_of((i * gate_unroll + u) * L, L) for u in range(gate_unroll)]
        gs = [gt_ref[:, pl.ds(t0, L)] for t0 in t0s]
        c3s = [jnp.dot(split3(jax.nn.log_sigmoid(g)), tri, preferred_element_type=F32) for g in gs]
        row = lax.broadcasted_iota(jnp.int32, (8, L), 0)
        tabs = []
        for t0, g, c3 in zip(t0s, gs, c3s):
            c2 = c3[0:8] + c3[8:16] + c3[16:24]
            cf, cb = c2[:, 0:L], c2[:, L:2 * L]
            tab = jnp.where(row == 0, g, jnp.where(row == 1, cf, jnp.where(row == 2, g, cb)))
            tab = jnp.where(row < 4, tab, 0.0)
            tab_ref[:, pl.ds(t0, L)] = tab
            tabs.append(tab)
        colfs = [jnp.transpose(jnp.concatenate([tab, jnp.zeros((L - 8, L), F32)], axis=0)) for tab in tabs]
        for t0, colf in zip(t0s, colfs):
            hi = colf.astype(BF16).astype(F32)
            r1 = colf - hi
            mid = r1.astype(BF16).astype(F32)
            lo = (r1 - mid).astype(BF16).astype(F32)
            col_ref[pl.ds(t0, L), :] = (hi + pltpu.roll(mid, 8, axis=1) + pltpu.roll(lo, 16, axis=1)).astype(BF16)
        return carry

    lax.fori_loop(0, nc // gate_unroll, gate_body, 0)

    srow = lax.broadcasted_iota(jnp.int32, (L, 4 * L), 0)
    scol = lax.broadcasted_iota(jnp.int32, (L, 4 * L), 1) // L
    sel3 = jnp.where((srow < 24) & (srow % 8 == scol), 1.0, 0.0).astype(BF16)

    stf_ref[...] = jnp.zeros_like(stf_ref)
    stb_ref[...] = jnp.zeros_like(stb_ref)
    mf_ref[...] = jnp.zeros_like(mf_ref)
    mb_ref[...] = jnp.zeros_like(mb_ref)
    ones_blk = jnp.ones((L, L), BF16)

    def stage_a(t0, fwd):
        c = dict(t0=t0, fwd=fwd)
        c["q"] = qc_ref[pl.ds(t0, L), :]
        c["kt"] = kt_ref[:, pl.ds(t0, L)]
        c["v"] = v_ref[pl.ds(t0, L), :]
        c["cols"] = jnp.dot(col_ref[pl.ds(t0, L), :], sel3, preferred_element_type=F32)
        c["qk"] = jnp.dot(c["q"], c["kt"], preferred_element_type=F32)
        return c

    def stage_b(c):
        r8 = tab_ref[:, pl.ds(c["t0"], L)]
        cols = c["cols"]
        if c["fwd"]:
            li_c, b_c = cols[:, 0:L], cols[:, L:2 * L]
            li_r, b_r = r8[0:1, :], r8[1:2, :]
            bl = b_c[L - 1:L, :]
            mask = ll <= jj
        else:
            li_c, b_c = cols[:, 2 * L:3 * L], cols[:, 3 * L:4 * L]
            li_r, b_r = r8[2:3, :], r8[3:4, :]
            bl = b_c[0:1, :]
            mask = ll >= jj
        a_c = bl - b_c + li_c
        a_max = jnp.max(a_c, axis=0, keepdims=True)
        w_c = jnp.exp(a_c - a_max)
        vw = (c["v"].astype(F32) * w_c).astype(BF16)
        c["cat"] = jnp.concatenate([vw, w_c.astype(BF16)], axis=1)
        d = jnp.where(mask, b_c + (li_r - b_r), -jnp.inf)
        c.update(b_c=b_c, bl=bl, a_max=a_max, d=d, dmax=jnp.max(d, axis=1, keepdims=True))
        del c["cols"]

    def stage_c(c):
        c["chunk2"] = jnp.dot(c["kt"], c.pop("cat"), preferred_element_type=F32)

    def stage_d(c, st_ref, m_ref):
        m_prev = m_ref[...]
        c["st"] = st_ref[...]
        m_inter = c["b_c"] + m_prev
        m_out = jnp.maximum(m_inter, c["dmax"])
        c["inter_w"] = jnp.exp(m_inter - m_out)
        c["s"] = (c.pop("qk") * jnp.exp(c["d"] - m_out)).astype(BF16)
        c["m_out"] = m_out
        m_new = jnp.maximum(c["bl"] + m_prev, c["a_max"])
        c["decay"] = jnp.exp(c["bl"] + m_prev - m_new)
        c["inject"] = jnp.exp(c["a_max"] - m_new)
        m_ref[...] = m_new

    def stage_e(c):
        vcat = jnp.concatenate([c["v"], ones_blk], axis=1)
        c["r2"] = jnp.dot(c["s"], vcat, preferred_element_type=F32)
        c["q2"] = jnp.dot(c["q"], c["st"].astype(BF16), preferred_element_type=F32)

    def stage_f(c, st_ref):
        r2, q2, inter_w = c["r2"], c["q2"], c["inter_w"]
        num = r2[:, 0:L] + inter_w * q2[:, 0:L]
        den = r2[:, L:2 * L] + inter_w * q2[:, L:2 * L]
        st_ref[...] = c["decay"][:, 0:1] * c["st"] + c["inject"][:, 0:1] * c["chunk2"]
        return num / jnp.maximum(jnp.abs(den), jnp.exp(-c["m_out"]))

    def finalize(t0, h):
        yv = _rms(h, ng_ref[...]) * jax.nn.sigmoid(o_ref[pl.ds(t0, L), :].astype(F32))
        y_ref[pl.ds(t0, L), :] = yv.astype(y_ref.dtype)

    unroll = 8 if half % 8 == 0 else 1

    def body(first_chunk, emit):
        chunks = []
        for u in range(unroll):
            cidx = first_chunk + u
            chunks.append((stage_a(pl.multiple_of(cidx * L, L), True),
                           stage_a(pl.multiple_of((nc - 1 - cidx) * L, L), False)))
        for pair in chunks:
            for c in pair:
                stage_b(c)
        for pair in chunks:
            for c in pair:
                stage_c(c)
        for cf, cb in chunks:
            stage_d(cf, stf_ref, mf_ref)
            stage_d(cb, stb_ref, mb_ref)
            stage_e(cf)
            stage_e(cb)
            emit(cf["t0"], stage_f(cf, stf_ref))
            emit(cb["t0"], stage_f(cb, stb_ref))

    def keep(t0, h):
        hs_ref[pl.ds(t0, L), :] = h

    def add_finalize(t0, h):
        finalize(t0, hs_ref[pl.ds(t0, L), :] + h)

    def first_half(i, carry):
        body(i * unroll, keep)
        return carry

    def second_half(i, carry):
        body(half + i * unroll, add_finalize)
        return carry

    lax.fori_loop(0, half // unroll, first_half, 0)
    lax.fori_loop(0, half // unroll, second_half, 0)


def _mlstm(pm, gt, conv_w, conv_b, norm_g):
    b, s, _ = pm.shape
    nh = MLSTM_HEADS
    assert s % (2 * MLSTM_CHUNK) == 0
    blk = 512 if s % 512 == 0 else MLSTM_CHUNK
    kern = functools.partial(_mlstm_kernel, seq=s, blk=blk)
    L = MLSTM_CHUNK
    col = lambda off: (lambda bi, h: (bi, 0, off + h))
    return pl.pallas_call(
        kern,
        out_shape=jax.ShapeDtypeStruct((b, s, MLSTM_W), BF16),
        grid=(b, nh),
        in_specs=[
            pl.BlockSpec((None, s, LANES), col(0)),
            pl.BlockSpec((None, s, LANES), col(nh)),
            pl.BlockSpec((None, s, LANES), col(2 * nh)),
            pl.BlockSpec((None, s, LANES), col(3 * nh)),
            pl.BlockSpec((None, 8, s), lambda bi, h: (bi, h, 0)),
            pl.BlockSpec((CONV_W, LANES), lambda bi, h: (0, h)),
            pl.BlockSpec((CONV_W, LANES), lambda bi, h: (0, nh + h)),
            pl.BlockSpec((1, LANES), lambda bi, h: (0, h)),
            pl.BlockSpec((1, LANES), lambda bi, h: (0, nh + h)),
            pl.BlockSpec((1, LANES), lambda bi, h: (0, h)),
        ],
        out_specs=pl.BlockSpec((None, s, LANES), lambda bi, h: (bi, 0, h)),
        scratch_shapes=[
            pltpu.VMEM((s + 16, LANES), F32),
            pltpu.VMEM((s, LANES), BF16),
            pltpu.VMEM((LANES, s), BF16),
            pltpu.VMEM((8, s), F32),
            pltpu.VMEM((s, LANES), BF16),
            pltpu.VMEM((s, LANES), F32),
            pltpu.VMEM((L, 2 * L), F32),
            pltpu.VMEM((L, 2 * L), F32),
            pltpu.VMEM((1, L), F32),
            pltpu.VMEM((1, L), F32),
        ],
        compiler_params=_cparams(("parallel", "parallel"), VMEM_LIMIT),
        name="mlstm",
    )(pm, pm, pm, pm, gt, conv_w, conv_w, conv_b, conv_b, norm_g)


def _swap16(y, lane):
    fwd = pltpu.roll(y, LANES - 16, axis=1)
    bwd = pltpu.roll(y, 16, axis=1)
    return jnp.where((lane % 32) < 16, fwd, bwd)


def _nt(a, b):
    return lax.dot_general(a, b, (((1,), (1,)), ((), ())), preferred_element_type=F32)


def _attn_prep_kernel(pa_ref, cos_ref, sin_ref, qg_ref, kg_ref, mseg_ref, qt_ref, kp_ref, vt_ref, *, q_scale):
    tm = pa_ref.shape[0]
    lane = lax.broadcasted_iota(jnp.int32, (tm, LANES), 1)
    cos = cos_ref[...]
    sin = sin_ref[...]
    mseg = mseg_ref[...]
    r = lax.broadcasted_iota(jnp.int32, (LANES, LANES), 0)
    c = lax.broadcasted_iota(jnp.int32, (LANES, LANES), 1)
    sel_lo = jnp.where((r == c) & (r < ATTN_HD), 1.0, 0.0).astype(BF16)
    sel_hi = jnp.where((c == r + ATTN_HD) & (r < ATTN_HD), 1.0, 0.0).astype(BF16)
    ones_row = jnp.where(lax.broadcasted_iota(jnp.int32, (LANES, tm), 0) == ATTN_HD, 1.0, 0.0)

    def norm_rope(x, g, scale):
        xx = x * x
        hi = xx.astype(BF16)
        lo = (xx - hi.astype(F32)).astype(BF16)
        ms = jnp.dot(jnp.concatenate([hi, lo], axis=1), mseg, preferred_element_type=F32)
        y = x * lax.rsqrt(ms + NORM_EPS) * g
        return (y * cos + _swap16(y, lane) * sin) * scale

    for j in range(ATTN_W // LANES):
        x = pa_ref[:, j * LANES:(j + 1) * LANES].astype(F32)
        y = norm_rope(x, qg_ref[...], q_scale).astype(BF16)
        qt_ref[2 * j] = _nt(sel_lo, y).astype(BF16)
        qt_ref[2 * j + 1] = _nt(sel_hi, y).astype(BF16)
    for j in range(KV_W // LANES):
        x = pa_ref[:, ATTN_W + j * LANES:ATTN_W + (j + 1) * LANES].astype(F32)
        y = norm_rope(x, kg_ref[...], 1.0)
        kp_ref[2 * j] = jnp.where(lane < ATTN_HD, y, 0.0).astype(BF16)
        kp_ref[2 * j + 1] = jnp.where(lane < ATTN_HD, pltpu.roll(y, ATTN_HD, axis=1), 0.0).astype(BF16)
        xv = pa_ref[:, ATTN_W + KV_W + j * LANES:ATTN_W + KV_W + (j + 1) * LANES]
        vt_ref[2 * j] = (_nt(sel_lo, xv) + ones_row).astype(BF16)
        vt_ref[2 * j + 1] = (_nt(sel_hi, xv) + ones_row).astype(BF16)


def _attn_prep(pa, cos_t, sin_t, qg, kg, mseg, *, tm):
    b, s, _ = pa.shape
    q_scale = (ATTN_HD ** -0.5) * math.log2(math.e)
    kern = functools.partial(_attn_prep_kernel, q_scale=q_scale)
    return pl.pallas_call(
        kern,
        out_shape=(
            jax.ShapeDtypeStruct((b, ATTN_HEADS, LANES, s), BF16),
            jax.ShapeDtypeStruct((b, KV_HEADS, s, LANES), BF16),
            jax.ShapeDtypeStruct((b, KV_HEADS, LANES, s), BF16),
        ),
        grid=(b, s // tm),
        in_specs=[
            pl.BlockSpec((None, tm, PA_W), lambda bi, i: (bi, i, 0)),
            pl.BlockSpec((tm, LANES), lambda bi, i: (i, 0)),
            pl.BlockSpec((tm, LANES), lambda bi, i: (i, 0)),
            pl.BlockSpec((1, LANES), lambda bi, i: (0, 0)),
            pl.BlockSpec((1, LANES), lambda bi, i: (0, 0)),
            pl.BlockSpec((2 * LANES, LANES), lambda bi, i: (0, 0)),
        ],
        out_specs=(
            pl.BlockSpec((None, ATTN_HEADS, LANES, tm), lambda bi, i: (bi, 0, 0, i)),
            pl.BlockSpec((None, KV_HEADS, tm, LANES), lambda bi, i: (bi, 0, i, 0)),
            pl.BlockSpec((None, KV_HEADS, LANES, tm), lambda bi, i: (bi, 0, 0, i)),
        ),
        compiler_params=_cparams(("parallel", "parallel"), VMEM_LIMIT),
        name="attn_prep",
    )(pa, cos_t, sin_t, qg, kg, mseg)


def _attn_kernel(qt_ref, k_ref, vt_ref, o_ref, m_ref, acc_ref, s_ref, *, tk):
    g, _, tq = qt_ref.shape
    s = k_ref.shape[0]
    n = s // tk
    m_ref[...] = jnp.full(m_ref.shape, -jnp.inf, F32)
    acc_ref[...] = jnp.zeros(acc_ref.shape, F32)

    def scores(j, slot, h):
        t0 = pl.multiple_of(j * tk, tk)
        s_ref[slot, :, h * tq:(h + 1) * tq] = jnp.dot(k_ref[pl.ds(t0, tk), :], qt_ref[h],
                                                      preferred_element_type=F32)

    def softmax(slot, h):
        cols = slice(h * tq, (h + 1) * tq)
        sc = s_ref[slot, :, cols]
        m_prev = m_ref[:, cols]
        m_new = jnp.maximum(m_prev, jnp.max(sc, axis=0, keepdims=True))
        m_ref[:, cols] = m_new
        return jnp.exp2(m_prev - m_new), jnp.exp2(sc - m_new).astype(BF16)

    def pv(j, h, alpha, p):
        t0 = pl.multiple_of(j * tk, tk)
        cols = slice(h * tq, (h + 1) * tq)
        acc_ref[:, cols] = alpha * acc_ref[:, cols] + jnp.dot(vt_ref[0:PV_ROWS, pl.ds(t0, tk)], p,
                                                              preferred_element_type=F32)

    def step(j, slot, prefetch):
        pending = None
        for h in range(g):
            if prefetch:
                scores(j + 1, 1 - slot, h)
            current = softmax(slot, h)
            if pending is not None:
                pv(j, h - 1, *pending)
            pending = current
        pv(j, g - 1, *pending)

    for h in range(g):
        scores(0, 0, h)

    unroll = 4 if n % 4 == 0 else 2

    def body(i, carry):
        for u in range(unroll):
            step(unroll * i + u, u % 2, True)
        return carry

    lax.fori_loop(0, n // unroll - 1, body, 0)
    for u in range(unroll):
        step(n - unroll + u, u % 2, u < unroll - 1)
    acc = acc_ref[...]
    o = (acc[0:ATTN_HD, :] / acc[ATTN_HD:ATTN_HD + 1, :]).astype(BF16)
    r = lax.broadcasted_iota(jnp.int32, (LANES, LANES), 0)
    c = lax.broadcasted_iota(jnp.int32, (LANES, LANES), 1)
    eye = jnp.where(r == c, 1.0, 0.0).astype(BF16)
    for pair in range(g // 2):
        rows = jnp.concatenate([o[:, (2 * pair) * tq:(2 * pair + 1) * tq],
                                o[:, (2 * pair + 1) * tq:(2 * pair + 2) * tq]], axis=0)
        out = lax.dot_general(rows, eye, (((0,), (0,)), ((), ())), preferred_element_type=F32)
        o_ref[:, pair * LANES:(pair + 1) * LANES] = out.astype(o_ref.dtype)


def _attention(qt, kp, vt, *, tq, tk):
    b, nh, _, s = qt.shape
    g = nh // KV_HEADS
    assert (s // tk) % 2 == 0 and g % 2 == 0
    kern = functools.partial(_attn_kernel, tk=tk)
    return pl.pallas_call(
        kern,
        out_shape=jax.ShapeDtypeStruct((b, s, ATTN_W), BF16),
        grid=(b, KV_HEADS, s // tq),
        in_specs=[
            pl.BlockSpec((None, g, LANES, tq), lambda bi, kv, i: (bi, kv, 0, i)),
            pl.BlockSpec((None, None, s, LANES), lambda bi, kv, i: (bi, kv, 0, 0)),
            pl.BlockSpec((None, None, LANES, s), lambda bi, kv, i: (bi, kv, 0, 0)),
        ],
        out_specs=pl.BlockSpec((None, tq, g * ATTN_HD), lambda bi, kv, i: (bi, i, kv)),
        scratch_shapes=[
            pltpu.VMEM((1, g * tq), F32),
            pltpu.VMEM((PV_ROWS, g * tq), F32),
            pltpu.VMEM((2, tk, g * tq), F32),
        ],
        compiler_params=_cparams(("parallel", "parallel", "parallel"), VMEM_LIMIT),
        name="attn",
    )(qt, kp, vt)


def _merge_kernel(x_ref, ym_ref, ya_ref, gg_ref, mod_ref, wbm_ref, wba_ref, wo_ref, g2_ref, wr_ref, br_ref,
                  xn_ref, h2_ref, lt_ref):
    d = x_ref.shape[-1]
    a = jnp.dot(ym_ref[...], wbm_ref[...], preferred_element_type=F32)
    bm = jnp.dot(ya_ref[...], wba_ref[...], preferred_element_type=F32)
    gm = jax.nn.sigmoid(gg_ref[:, 0:d].astype(F32))
    ga = jax.nn.sigmoid(gg_ref[:, d:2 * d].astype(F32))
    merged = (gm * a + ga * bm).astype(BF16)
    u = jnp.dot(merged, wo_ref[...], preferred_element_type=F32)
    xn = x_ref[...] + mod_ref[2:3, :] * u
    xn_ref[...] = xn
    h2 = _rms(xn, g2_ref[...]) * (1.0 + mod_ref[4:5, :]) + mod_ref[3:4, :]
    h2_ref[...] = h2.astype(h2_ref.dtype)
    lt = lax.dot_general(wr_ref[...], h2, (((1,), (1,)), ((), ())), precision=HIGHEST,
                         preferred_element_type=F32)
    lt_ref[...] = lt + br_ref[...]


def _merge(x, ym, ya, gg, mod_l, wbm, wba, wo, g2, wr_t, br_t, *, tm):
    b, s, d = x.shape
    ns = s // tm
    rr = wr_t.shape[0]
    full = lambda shp: pl.BlockSpec(shp, lambda bi, i: tuple(0 for _ in shp))
    return pl.pallas_call(
        _merge_kernel,
        out_shape=(
            jax.ShapeDtypeStruct((b, s, d), F32),
            jax.ShapeDtypeStruct((b * s, d), BF16),
            jax.ShapeDtypeStruct((rr, b * s), F32),
        ),
        grid=(b, ns),
        in_specs=[
            pl.BlockSpec((None, tm, d), lambda bi, i: (bi, i, 0)),
            pl.BlockSpec((None, tm, MLSTM_W), lambda bi, i: (bi, i, 0)),
            pl.BlockSpec((None, tm, ATTN_W), lambda bi, i: (bi, i, 0)),
            pl.BlockSpec((None, tm, 2 * d), lambda bi, i: (bi, i, 0)),
            pl.BlockSpec((None, 6, d), lambda bi, i: (bi, 0, 0)),
            full((MLSTM_W, d)),
            full((ATTN_W, d)),
            full((d, d)),
            full((1, d)),
            full((rr, d)),
            full((rr, 1)),
        ],
        out_specs=(
            pl.BlockSpec((None, tm, d), lambda bi, i: (bi, i, 0)),
            pl.BlockSpec((tm, d), lambda bi, i: (bi * ns + i, 0)),
            pl.BlockSpec((rr, tm), lambda bi, i: (0, bi * ns + i)),
        ),
        compiler_params=_cparams(("parallel", "parallel"), VMEM_LIMIT),
        name="merge",
    )(x, ym, ya, gg, mod_l, wbm, wba, wo, g2, wr_t, br_t)


def _route_kernel(lt_ref, pos_ref, wts_ref, runs_ref, carry_ref):
    tb = lt_ref.shape[1]
    epg = EXPERTS_PER_GROUP

    @pl.when(pl.program_id(0) == 0)
    def _():
        carry_ref[...] = jnp.zeros_like(carry_ref)

    row8 = lax.broadcasted_iota(jnp.int32, (8, tb), 0)
    gl = jnp.where(row8 < N_GROUPS, lt_ref[0:8, :], -jnp.inf)
    ge = jnp.exp(gl - jnp.max(gl, axis=0, keepdims=True))
    pg = ge / jnp.sum(ge, axis=0, keepdims=True)
    p_top = jnp.max(pg, axis=0, keepdims=True)
    g_idx = jnp.min(jnp.where(pg == p_top, row8, 8), axis=0, keepdims=True)

    el = jnp.zeros((epg, tb), F32)
    for g in range(N_GROUPS):
        el = jnp.where(g_idx == g, lt_ref[8 + g * epg:8 + (g + 1) * epg, :], el)
    ee = jnp.exp(el - jnp.max(el, axis=0, keepdims=True))
    pe = ee / jnp.sum(ee, axis=0, keepdims=True)
    v1 = jnp.max(pe, axis=0, keepdims=True)
    i1 = jnp.min(jnp.where(pe == v1, row8, 8), axis=0, keepdims=True)
    pe2 = jnp.where(row8 == i1, -1.0, pe)
    v2 = jnp.max(pe2, axis=0, keepdims=True)
    i2 = jnp.min(jnp.where(pe2 == v2, row8, 8), axis=0, keepdims=True)
    denom = v1 + v2
    w0 = v1 / denom * p_top
    w1 = v2 / denom * p_top
    e0 = g_idx * epg + i1
    e1 = g_idx * epg + i2

    rowe = lax.broadcasted_iota(jnp.int32, (N_EXPERTS, tb), 0)
    oh0 = rowe == e0
    oh1 = rowe == e1
    oh = jnp.where(oh0 | oh1, 1.0, 0.0)
    src = lax.broadcasted_iota(jnp.int32, (tb, tb), 0)
    dst = lax.broadcasted_iota(jnp.int32, (tb, tb), 1)
    strict = jnp.where(src < dst, 1.0, 0.0).astype(BF16)
    cum = jnp.dot(oh.astype(BF16), strict, preferred_element_type=F32)
    cnt_col = jnp.sum(oh, axis=1, keepdims=True)
    seg_col = jnp.floor((cnt_col + (SEG_ALIGN - 1.0)) * (1.0 / SEG_ALIGN)) * SEG_ALIGN
    er = lax.broadcasted_iota(jnp.int32, (N_EXPERTS, N_EXPERTS), 0)
    ec = lax.broadcasted_iota(jnp.int32, (N_EXPERTS, N_EXPERTS), 1)
    before = jnp.where(ec < er, 1.0, 0.0).astype(BF16)
    start_col = jnp.dot(before, jnp.broadcast_to(seg_col, (N_EXPERTS, LANES)).astype(BF16),
                        preferred_element_type=F32)[:, 0:1]
    base = start_col + cum
    p0 = jnp.sum(jnp.where(oh0, base, 0.0), axis=0, keepdims=True)
    p1 = jnp.sum(jnp.where(oh1, base, 0.0), axis=0, keepdims=True)
    pos_ref[...] = jnp.where(row8 == 0, p0, jnp.where(row8 == 1, p1, 0.0)).astype(jnp.int32)
    wts_ref[...] = jnp.where(row8 == 0, w0, jnp.where(row8 == 1, w1, 0.0))

    ohp = jnp.concatenate([oh, jnp.zeros((LANES - N_EXPERTS, tb), F32)], axis=0).astype(BF16)
    cnt_row = _nt(jnp.ones((8, tb), BF16), ohp)
    seg_row = jnp.floor((cnt_row + (SEG_ALIGN - 1.0)) * (1.0 / SEG_ALIGN)) * SEG_ALIGN
    carry = carry_ref[...]
    runs_ref[0] = (seg_row + pltpu.roll(carry, N_EXPERTS, axis=1)).astype(jnp.int32)
    carry_ref[...] = carry + seg_row


def _route(lt, *, tb):
    rr, t = lt.shape
    nblk = t // tb
    return pl.pallas_call(
        _route_kernel,
        out_shape=(
            jax.ShapeDtypeStruct((8, t), jnp.int32),
            jax.ShapeDtypeStruct((8, t), F32),
            jax.ShapeDtypeStruct((nblk, 8, LANES), jnp.int32),
        ),
        grid=(nblk,),
        in_specs=[pl.BlockSpec((rr, tb), lambda i: (0, i))],
        out_specs=(
            pl.BlockSpec((8, tb), lambda i: (0, i)),
            pl.BlockSpec((8, tb), lambda i: (0, i)),
            pl.BlockSpec((1, 8, LANES), lambda i: (i, 0, 0)),
        ),
        scratch_shapes=[pltpu.VMEM((8, LANES), F32)],
        compiler_params=_cparams(("arbitrary",), VMEM_LIMIT),
        name="route",
    )(lt)


def _segment_piece(local_ref, hbm_ref, sem, s, d, rows, to_hbm):
    loc = local_ref.at[pl.ds(pl.multiple_of(s, SEG_ALIGN), rows)]
    far = hbm_ref.at[pl.ds(pl.multiple_of(d, SEG_ALIGN), rows)]
    return pltpu.make_async_copy(loc, far, sem) if to_hbm else pltpu.make_async_copy(far, loc, sem)


def _segment_starts(blk, runs_ref, off_ref, local_ref, hbm_ref, sem, cnt_ref, slot, *, to_hbm):
    big = 2 * SEG_ALIGN

    def expert(e, carry):
        s0, total_big, total_small = carry
        n = runs_ref[blk * 2 * N_EXPERTS + e]
        d0 = off_ref[e] + runs_ref[blk * 2 * N_EXPERTS + N_EXPERTS + e]
        nbig = n >> (SEG_SHIFT + 1)
        nsmall = (n >> SEG_SHIFT) & 1

        def big_piece(c, carry):
            _segment_piece(local_ref, hbm_ref, sem, s0 + c * big, d0 + c * big, big, to_hbm).start()
            return carry

        def small_piece(c, carry):
            _segment_piece(local_ref, hbm_ref, sem, s0 + nbig * big + c * SEG_ALIGN, d0 + nbig * big + c * SEG_ALIGN,
                           SEG_ALIGN, to_hbm).start()
            return carry

        lax.fori_loop(0, nbig, big_piece, 0)
        lax.fori_loop(0, nsmall, small_piece, 0)
        return s0 + n, total_big + nbig, total_small + nsmall

    zero = jnp.int32(0)
    _, total_big, total_small = lax.fori_loop(0, N_EXPERTS, expert, (zero, zero, zero))
    cnt_ref[slot, 0] = total_big
    cnt_ref[slot, 1] = total_small


def _segment_waits(local_ref, hbm_ref, sem, cnt_ref, slot, *, to_hbm):
    def wait_big(c, carry):
        _segment_piece(local_ref, hbm_ref, sem, 0, 0, 2 * SEG_ALIGN, to_hbm).wait()
        return carry

    def wait_small(c, carry):
        _segment_piece(local_ref, hbm_ref, sem, 0, 0, SEG_ALIGN, to_hbm).wait()
        return carry

    lax.fori_loop(0, cnt_ref[slot, 0], wait_big, 0)
    lax.fori_loop(0, cnt_ref[slot, 1], wait_small, 0)


def _dispatch_kernel(runs_ref, off_ref, last_ref, h_ref, pos_ref, xg_ref, sbuf_ref, zero_ref, cnt_ref, sem, zsem, *, tme):
    i = pl.program_id(0)
    nblk = pl.num_programs(0)
    slot = i % 2
    tb = h_ref.shape[0]
    rp = sbuf_ref.shape[1]

    @pl.when(i == 0)
    def _():
        zero_ref[...] = jnp.zeros_like(zero_ref)
        for wait in (False, True):
            def tail_tile(j, carry, wait=wait):
                cp = pltpu.make_async_copy(zero_ref, xg_ref.at[pl.ds(pl.multiple_of(j * tme, tme), tme)], zsem)
                if wait:
                    cp.wait()
                else:
                    cp.start()
                return carry

            lax.fori_loop(last_ref[N_EXPERTS], xg_ref.shape[0] // tme, tail_tile, 0)
            for e in range(N_EXPERTS):
                @pl.when(last_ref[e] >= 0)
                def _():
                    cp = pltpu.make_async_copy(
                        zero_ref, xg_ref.at[pl.ds(pl.multiple_of(last_ref[e], SEG_ALIGN), tme)], zsem)
                    if wait:
                        cp.wait()
                    else:
                        cp.start()

    @pl.when(i >= 2)
    def _():
        _segment_waits(sbuf_ref.at[slot], xg_ref, sem.at[slot], cnt_ref, slot, to_hbm=True)

    ii = lax.broadcasted_iota(jnp.int32, (rp, tb), 0)
    perm = jnp.where((ii == pos_ref[0:1, :]) | (ii == pos_ref[1:2, :]), 1.0, 0.0).astype(BF16)
    sbuf_ref[slot] = jnp.dot(perm, h_ref[...], preferred_element_type=F32).astype(BF16)
    _segment_starts(i, runs_ref, off_ref, sbuf_ref.at[slot], xg_ref, sem.at[slot], cnt_ref, slot, to_hbm=True)

    @pl.when(i == nblk - 1)
    def _():
        _segment_waits(sbuf_ref.at[slot], xg_ref, sem.at[slot], cnt_ref, slot, to_hbm=True)

        @pl.when(i >= 1)
        def _():
            _segment_waits(sbuf_ref.at[1 - slot], xg_ref, sem.at[1 - slot], cnt_ref, 1 - slot, to_hbm=True)


def _dispatch(runs, offsets, last_tile, h2, pos, p_rows, *, tb, tme):
    t, d = h2.shape
    rp = 2 * tb + SEG_ALIGN * N_EXPERTS
    kern = functools.partial(_dispatch_kernel, tme=tme)
    return pl.pallas_call(
        kern,
        out_shape=jax.ShapeDtypeStruct((p_rows, d), BF16),
        grid_spec=pltpu.PrefetchScalarGridSpec(
            num_scalar_prefetch=3,
            grid=(t // tb,),
            in_specs=[
                pl.BlockSpec((tb, d), lambda i, *_: (i, 0)),
                pl.BlockSpec((8, tb), lambda i, *_: (0, i)),
            ],
            out_specs=pl.BlockSpec(memory_space=pl.ANY),
            scratch_shapes=[
                pltpu.VMEM((2, rp, d), BF16),
                pltpu.VMEM((tme, d), BF16),
                pltpu.SMEM((2, 2), jnp.int32),
                pltpu.SemaphoreType.DMA((2,)),
                pltpu.SemaphoreType.DMA(()),
            ],
        ),
        compiler_params=_cparams(("arbitrary",), VMEM_LIMIT),
        name="dispatch",
    )(runs, offsets, last_tile, h2, pos)


def _experts_kernel(te_ref, first_ref, used_ref, x_ref, wg_ref, wu_ref, wd_ref, y_ref, wgb, wub, wdb):
    i = pl.program_id(0)

    @pl.when(first_ref[i] == 1)
    def _():
        wgb[...] = wg_ref[...].astype(BF16)
        wub[...] = wu_ref[...].astype(BF16)
        wdb[...] = wd_ref[...].astype(BF16)

    @pl.when(used_ref[i] == 1)
    def _():
        tm = x_ref.shape[0]
        sub = tm // 2 if tm % 512 == 0 else tm
        halves = [slice(r, r + sub) for r in range(0, tm, sub)]
        ups = [(jnp.dot(x_ref[rs, :], wgb[...], preferred_element_type=F32),
                jnp.dot(x_ref[rs, :], wub[...], preferred_element_type=F32)) for rs in halves]
        acts = [(a * jax.nn.sigmoid(a) * u).astype(BF16) for a, u in ups]
        for rs, act in zip(halves, acts):
            y_ref[rs, :] = jnp.dot(act, wdb[...], preferred_element_type=F32).astype(y_ref.dtype)

    @pl.when(used_ref[i] == 0)
    def _():
        y_ref[...] = jnp.zeros_like(y_ref)


def _experts(tile_expert, tile_first, tile_used, xg, w_gate, w_up, w_down, *, layer, tm):
    p_rows, d = xg.shape
    f = w_gate.shape[-1]
    nt = p_rows // tm
    wmap = lambda i, te, fi, us: (layer, te[i], 0, 0)
    return pl.pallas_call(
        _experts_kernel,
        out_shape=jax.ShapeDtypeStruct((p_rows, d), BF16),
        grid_spec=pltpu.PrefetchScalarGridSpec(
            num_scalar_prefetch=3,
            grid=(nt,),
            in_specs=[
                pl.BlockSpec((tm, d), lambda i, te, fi, us: (i, 0)),
                pl.BlockSpec((None, None, d, f), wmap),
                pl.BlockSpec((None, None, d, f), wmap),
                pl.BlockSpec((None, None, f, d), wmap),
            ],
            out_specs=pl.BlockSpec((tm, d), lambda i, te, fi, us: (i, 0)),
            scratch_shapes=[
                pltpu.VMEM((d, f), BF16),
                pltpu.VMEM((d, f), BF16),
                pltpu.VMEM((f, d), BF16),
            ],
        ),
        compiler_params=_cparams(("arbitrary",), VMEM_LIMIT),
        name="experts",
    )(tile_expert, tile_first, tile_used, xg, w_gate, w_up, w_down)


def _combine_kernel(runs_ref, off_ref, x_ref, pos_ref, w_ref, mod_ref, fg_ref, y_ref, o_ref, ybuf_ref, cnt_ref, sem, *, final):
    blk = pl.program_id(0) * pl.num_programs(1) + pl.program_id(1)
    nblk = pl.num_programs(0) * pl.num_programs(1)
    slot = blk % 2
    tb = x_ref.shape[0]
    rp = ybuf_ref.shape[1]

    @pl.when(blk == 0)
    def _():
        ybuf_ref[...] = jnp.zeros_like(ybuf_ref)
        _segment_starts(blk, runs_ref, off_ref, ybuf_ref.at[slot], y_ref, sem.at[slot], cnt_ref, slot, to_hbm=False)

    @pl.when(blk + 1 < nblk)
    def _():
        _segment_starts(blk + 1, runs_ref, off_ref, ybuf_ref.at[1 - slot], y_ref, sem.at[1 - slot], cnt_ref, 1 - slot,
                        to_hbm=False)

    _segment_waits(ybuf_ref.at[slot], y_ref, sem.at[slot], cnt_ref, slot, to_hbm=False)
    li = lax.broadcasted_iota(jnp.int32, (tb, rp), 1)
    wmat = (jnp.where(li == pos_ref[:, 0:1], w_ref[:, 0:1], 0.0)
            + jnp.where(li == pos_ref[:, 1:2], w_ref[:, 1:2], 0.0))
    moe = jnp.dot(wmat.astype(BF16), ybuf_ref[slot], preferred_element_type=F32)
    xo = x_ref[...] + mod_ref[5:6, :] * moe
    if final:
        xo = _rms(xo, fg_ref[...])
    o_ref[...] = xo


def _combine(runs, offsets, xn, pos_col, wts_col, mod_l, fg, y, *, tb, final):
    b, s, d = xn.shape
    ns = s // tb
    rp = 2 * tb + SEG_ALIGN * N_EXPERTS
    kern = functools.partial(_combine_kernel, final=final)
    return pl.pallas_call(
        kern,
        out_shape=jax.ShapeDtypeStruct((b, s, d), F32),
        grid_spec=pltpu.PrefetchScalarGridSpec(
            num_scalar_prefetch=2,
            grid=(b, ns),
            in_specs=[
                pl.BlockSpec((None, tb, d), lambda bi, i, *_: (bi, i, 0)),
                pl.BlockSpec((tb, 8), lambda bi, i, *_: (bi * ns + i, 0)),
                pl.BlockSpec((tb, 8), lambda bi, i, *_: (bi * ns + i, 0)),
                pl.BlockSpec((None, 6, d), lambda bi, i, *_: (bi, 0, 0)),
                pl.BlockSpec((1, d), lambda bi, i, *_: (0, 0)),
                pl.BlockSpec(memory_space=pl.ANY),
            ],
            out_specs=pl.BlockSpec((None, tb, d), lambda bi, i, *_: (bi, i, 0)),
            scratch_shapes=[
                pltpu.VMEM((2, rp, d), BF16),
                pltpu.SMEM((2, 2), jnp.int32),
                pltpu.SemaphoreType.DMA((2,)),
            ],
        ),
        compiler_params=_cparams(("arbitrary", "arbitrary"), VMEM_LIMIT),
        name="combine",
    )(runs, offsets, xn, pos_col, wts_col, mod_l, fg, y)


def _rope_tables(seq):
    rows = seq // GRID_W
    row = jnp.repeat(jnp.arange(rows, dtype=F32), GRID_W)
    col = jnp.tile(jnp.arange(GRID_W, dtype=F32), rows)
    half = ATTN_HD // 2
    inv_freq = ROPE_BASE ** (-jnp.arange(0, half, 2, dtype=F32) / half)
    ang_r = row[:, None] * inv_freq
    ang_c = col[:, None] * inv_freq
    cos64 = jnp.concatenate([jnp.cos(ang_r), jnp.cos(ang_r), jnp.cos(ang_c), jnp.cos(ang_c)], axis=-1)
    sin64 = jnp.concatenate([-jnp.sin(ang_r), jnp.sin(ang_r), -jnp.sin(ang_c), jnp.sin(ang_c)], axis=-1)
    return jnp.tile(cos64, (1, LANES // ATTN_HD)), jnp.tile(sin64, (1, LANES // ATTN_HD))


def _plan_tiles(rows, tm, nt):
    tiles = (rows + tm - 1) // tm
    tile_end = jnp.cumsum(tiles)
    tile_start = tile_end - tiles
    offsets = (tile_start * tm).astype(jnp.int32)
    last_tile = jnp.where(tiles > 0, (tile_end - 1) * tm, -1).astype(jnp.int32)
    last_tile = jnp.concatenate([last_tile, tile_end[-1:].astype(jnp.int32)])
    tidx = jnp.arange(nt, dtype=jnp.int32)
    te = jnp.sum((tidx[:, None] >= tile_end[None, :]).astype(jnp.int32), axis=1)
    used = (tidx < tile_end[-1]).astype(jnp.int32)
    last_used = jnp.sum((tile_end[-1] - 1 >= tile_end).astype(jnp.int32))
    te = jnp.where(used == 1, te, last_used).astype(jnp.int32)
    prev = jnp.concatenate([jnp.full((1,), -1, jnp.int32), te[:-1]])
    first = ((te != prev) & (used == 1)).astype(jnp.int32)
    return offsets, last_tile, te, first, used


def _tiles(s, t):
    tm = 512 if s % 512 == 0 else 128
    tq = 256 if s % 256 == 0 else 128
    tk = 512 if s % 2048 == 0 else 128
    tme = 512
    tbr = 512 if s % 512 == 0 else 128
    nblk = t // tbr
    nt = -(-(2 * t + (SEG_ALIGN - 1) * N_EXPERTS * nblk) // tme) + N_EXPERTS
    return tm, tq, tk, tme, tbr, nt


def kernel(x, c, w_ada, b_ada, norm1_g, w_in, b_in, conv_w, conv_b, mlstm_norm_g, q_norm_g, k_norm_g,
           w_branch_m, w_branch_a, w_out, norm2_g, w_router_group, b_router_group, w_router_expert,
           b_router_expert, w_gate, w_up, w_down, final_norm_g):
    b, s, d = x.shape
    depth = w_ada.shape[0]
    t = b * s
    nh = MLSTM_HEADS
    tm, tq, tk, tme, tbr, nt = _tiles(s, t)
    p_rows = nt * tme

    mod = _adaln(c, w_ada, b_ada).reshape(depth, b, 6, d)
    cos_t, sin_t = _rope_tables(s)
    seg = np.arange(LANES) // ATTN_HD
    mseg = jnp.asarray((seg[:, None] == seg[None, :]).astype(np.float32) / ATTN_HD)
    mseg = jnp.concatenate([mseg, mseg], axis=0).astype(BF16)

    o_g = 4 * MLSTM_W
    o_aq = o_g + 4 * nh

    for l in range(depth):
        w_l = w_in[l]
        w_main = jnp.concatenate([w_l[:, :o_g], w_l[:, o_aq:]], axis=1).astype(BF16)
        b_main = jnp.concatenate([b_in[l][:o_g], b_in[l][o_aq:]])[None, :]
        wg = jnp.transpose(w_l[:, o_g:o_aq].reshape(d, 4, nh), (2, 1, 0))
        wgt = jnp.concatenate([wg, jnp.zeros((nh, 4, d), F32)], axis=1).reshape(nh * 8, d).astype(BF16)
        bg = jnp.transpose(b_in[l][o_g:o_aq].reshape(4, nh))
        bgt = jnp.concatenate([bg, jnp.zeros((nh, 4), F32)], axis=1).reshape(nh * 8, 1)

        pm, pa, gg, gt = _in_proj(x, mod[l], norm1_g[l][None, :], w_main, b_main, wgt, bgt, tm=tm)
        ym = _mlstm(pm, gt, conv_w[l].reshape(CONV_W, 2 * MLSTM_W), conv_b[l][None, :],
                    mlstm_norm_g[l][None, :])
        qg = jnp.tile(q_norm_g[l], LANES // ATTN_HD)[None, :]
        kg = jnp.tile(k_norm_g[l], LANES // ATTN_HD)[None, :]
        qp, kp, vp = _attn_prep(pa, cos_t, sin_t, qg, kg, mseg, tm=tm)
        ya = _attention(qp, kp, vp, tq=tq, tk=tk)

        wr = jnp.concatenate([w_router_group[l], jnp.zeros((d, 8 - N_GROUPS), F32), w_router_expert[l]], axis=1)
        br = jnp.concatenate([b_router_group[l], jnp.zeros((8 - N_GROUPS,), F32), b_router_expert[l]])
        xn, h2, lt = _merge(x, ym, ya, gg, mod[l], w_branch_m[l].astype(BF16), w_branch_a[l].astype(BF16),
                            w_out[l].astype(BF16), norm2_g[l][None, :], jnp.transpose(wr), br[:, None], tm=tm)

        pos, wts, runs = _route(lt, tb=tbr)
        runs = runs[:, 0, :2 * N_EXPERTS]
        rows = runs[-1, :N_EXPERTS] + runs[-1, N_EXPERTS:]
        offsets, last_tile, te, first, used = _plan_tiles(rows, tme, nt)
        runs = runs.reshape(-1)
        xg = _dispatch(runs, offsets, last_tile, h2, pos, p_rows, tb=tbr, tme=tme)
        y = _experts(te, first, used, xg, w_gate, w_up, w_down, layer=l, tm=tme)
        x = _combine(runs, offsets, xn, jnp.transpose(pos), jnp.transpose(wts), mod[l], final_norm_g[None, :], y,
                     tb=tbr, final=(l == depth - 1))
    return x
```
